```python
import math
import jax, jax.numpy as jnp
from jax import lax
import numpy as np

D_MODEL = 1024
BATCH = 4
SEQ = 4096
DEPTH = 2
DEC_BATCH = 16
DEC_SEQ = 64
PAST_LEN = 2048

CHUNK = 64
N_EVEN = (DEPTH + 1) // 2
N_ODD = DEPTH // 2
FF_DIM = 2816
EPS = 1e-6
CONV_W = 4
RET_HEADS = 4
RET_DK = 128
RET_DV = 128
ROPE_BASE = 10000.0
HG_HEADS = 4
HG_DK = 128
HG_DV = 128
LRU_WIDTH = 512
LRU_BLOCKS = 8
LRU_BS = LRU_WIDTH // LRU_BLOCKS
LRU_C = 8.0
DN_HEADS = 4
DN_DK = 128
DN_DV = 128
DN_CONV_CH = 2 * DN_HEADS * DN_DK + DN_HEADS * DN_DV

EVEN_SPLITS = [RET_HEADS * RET_DK, RET_HEADS * RET_DK, RET_HEADS * RET_DV, RET_HEADS * RET_DV,
               HG_HEADS * HG_DK, HG_HEADS * HG_DK, HG_HEADS * HG_DV, HG_HEADS * HG_DV]
EVEN_IN = sum(EVEN_SPLITS)
EVEN_MIX = RET_HEADS * RET_DV + HG_HEADS * HG_DV
ODD_SPLITS = [LRU_WIDTH, LRU_WIDTH, DN_HEADS * DN_DK, DN_HEADS * DN_DK, DN_HEADS * DN_DV,
              DN_HEADS * DN_DV, DN_HEADS, DN_HEADS]
ODD_IN = sum(ODD_SPLITS)
ODD_MIX = LRU_WIDTH + DN_HEADS * DN_DV

kernel_name = 'chunk_causal_hybrid_retention_hgrn2_rglru_gdn_step'


def split_cols(z, sizes):
    idx = np.cumsum(sizes)[:-1].tolist()
    return jnp.split(z, idx, axis=-1)


def rmsnorm(x, g):
    xf = x.astype(jnp.float32)
    y = xf * lax.rsqrt(jnp.mean(xf * xf, axis=-1, keepdims=True) + EPS)
    return (y * g).astype(x.dtype)


def head_rmsnorm(o, g):
    B, T, H, d = o.shape
    o = o * lax.rsqrt(jnp.mean(o * o, axis=-1, keepdims=True) + EPS)
    return o.reshape(B, T, H * d) * g.astype(o.dtype)


def l2norm(x):
    return x * lax.rsqrt(jnp.sum(x * x, axis=-1, keepdims=True) + EPS)


def swiglu(x, w_in, w_out):
    gate, up = jnp.split(x @ w_in, 2, axis=-1)
    return (jax.nn.silu(gate) * up) @ w_out


def rope(x, pos):
    half = x.shape[-1] // 2
    freq = ROPE_BASE ** (-jnp.arange(half, dtype=jnp.float32) / half)
    ang = pos.astype(jnp.float32)[:, None] * freq[None, :]
    cos = jnp.cos(ang)[None, :, None, :]
    sin = jnp.sin(ang)[None, :, None, :]
    x1, x2 = x[..., :half], x[..., half:]
    return jnp.concatenate([x1 * cos - x2 * sin, x1 * sin + x2 * cos], axis=-1)


def to_chunks(x, c):
    B, T, H = x.shape[:3]
    rest = x.shape[3:]
    x = x.reshape((B, T // c, c, H) + rest)
    perm = (1, 0, 3, 2) + tuple(range(4, x.ndim))
    return x.transpose(perm)


def from_chunks(o):
    N, B, H, C, d = o.shape
    return o.transpose(1, 0, 3, 2, 4).reshape(B, N * C, H, d)


def causal_conv(x, buf, w):
    T = x.shape[1]
    xp = jnp.concatenate([buf.astype(x.dtype), x], axis=1)
    y = sum(xp[:, j:j + T] * w[j] for j in range(CONV_W))
    return y, xp[:, -(CONV_W - 1):]


def retention(q, k, v, s0, c):
    ar = jnp.arange(RET_HEADS, dtype=jnp.float32)
    log_g = jnp.log1p(-jnp.exp2(-5.0 - ar))
    p = jnp.arange(c, dtype=jnp.float32)
    intra = jnp.exp(log_g[:, None, None] * jnp.abs(p[:, None] - p[None, :]))
    q_dec = jnp.exp(log_g[:, None] * (p + 1.0))[..., None]
    k_dec = jnp.exp(log_g[:, None] * (c - 1.0 - p))[..., None]
    s_dec = jnp.exp(log_g * c)[:, None, None]

    def step(s, inp):
        qc, kc, vc = inp
        att = jnp.einsum('bhtd,bhsd->bhts', qc, kc) * intra
        o = (jnp.einsum('bhts,bhse->bhte', att, vc)
             + jnp.einsum('bhtd,bhde->bhte', qc * q_dec, s))
        s = s * s_dec + jnp.einsum('bhsd,bhse->bhde', kc * k_dec, vc)
        return s, o

    s, o = lax.scan(step, s0, (to_chunks(q, c), to_chunks(k, c), to_chunks(v, c)))
    return from_chunks(o), s


def hgrn2_chunked(q, k, v, logf, s0, c):
    incl = jnp.tril(jnp.ones((c, c), bool))[:, :, None]

    def step(s, inp):
        qc, kc, vc, gc = inp
        b = jnp.cumsum(gc, axis=2)
        rel = b[:, :, :, None, :] - b[:, :, None, :, :]
        dec = jnp.exp(jnp.where(incl, rel, -jnp.inf))
        att = jnp.einsum('bhtd,bhsd,bhtsd->bhts', qc, kc, dec)
        o = (jnp.einsum('bhts,bhse->bhte', att, vc)
             + jnp.einsum('bhtd,bhde->bhte', qc * jnp.exp(b), s))
        b_end = b[:, :, -1:, :]
        s = (s * jnp.exp(b_end)[:, :, 0, :, None]
             + jnp.einsum('bhsd,bhse->bhde', kc * jnp.exp(b_end - b), vc))
        return s, o

    xs = (to_chunks(q, c), to_chunks(k, c), to_chunks(v, c), to_chunks(logf, c))
    s, o = lax.scan(step, s0, xs)
    return from_chunks(o), s


def gated_delta_chunked(q, k, v, beta, loga, s0, c):
    strict = jnp.tril(jnp.ones((c, c), bool), -1)
    incl = jnp.tril(jnp.ones((c, c), bool))
    eye = jnp.eye(c, dtype=jnp.float32)

    def step(s, inp):
        qc, kc, vc, bc, gc = inp
        g = jnp.cumsum(gc, axis=-1)
        rel = g[..., :, None] - g[..., None, :]
        a_mat = (bc[..., None] * jnp.einsum('bhtd,bhsd->bhts', kc, kc)
                 * jnp.exp(jnp.where(strict, rel, -jnp.inf)))
        eg = jnp.exp(g)[..., None]
        rhs = bc[..., None] * (vc - eg * jnp.einsum('bhtd,bhde->bhte', kc, s))
        w = lax.linalg.triangular_solve(eye + a_mat, rhs, left_side=True, lower=True,
                                        unit_diagonal=True)
        qk = jnp.einsum('bhtd,bhsd->bhts', qc, kc) * jnp.exp(jnp.where(incl, rel, -jnp.inf))
        o = eg * jnp.einsum('bhtd,bhde->bhte', qc, s) + jnp.einsum('bhts,bhse->bhte', qk, w)
        g_end = g[..., -1:]
        s = (s * jnp.exp(g_end)[..., None]
             + jnp.einsum('bhsd,bhse->bhde', kc * jnp.exp(g_end - g)[..., None], w))
        return s, o

    xs = (to_chunks(q, c), to_chunks(k, c), to_chunks(v, c), to_chunks(beta, c), to_chunks(loga, c))
    s, o = lax.scan(step, s0, xs)
    return from_chunks(o), s


def rg_lru(x, h0, w_a, b_a, w_x, b_x, lam):
    B, T, W = x.shape
    xb = x.reshape(B, T, LRU_BLOCKS, LRU_BS)
    r = jax.nn.sigmoid(jnp.einsum('btnd,nde->btne', xb, w_a).reshape(B, T, W) + b_a)
    i = jax.nn.sigmoid(jnp.einsum('btnd,nde->btne', xb, w_x).reshape(B, T, W) + b_x)
    log_a = -LRU_C * r * jax.nn.softplus(-lam)
    a = jnp.exp(log_a)
    u = jnp.sqrt(-jnp.expm1(2.0 * log_a)) * (i * x)

    def combine(l, rr):
        return (l[0] * rr[0], rr[0] * l[1] + rr[1])

    a_cum, h_part = lax.associative_scan(combine, (a, u), axis=1)
    h = a_cum * h0[:, None, :] + h_part
    return h, h[:, -1]


def even_mixer(h, pos, c, ret_s, hg_s, lb, p, j):
    B, T, _ = h.shape
    f32 = jnp.float32
    z = (h @ p['even_w_in'][j]).astype(f32)
    rq, rk, rv, rg, hq, hf, hi, hog = split_cols(z, EVEN_SPLITS)

    def heads(t, n):
        return t.reshape(B, T, n, -1)

    rq = rope(heads(rq, RET_HEADS), pos) * (RET_DK ** -0.5)
    rk = rope(heads(rk, RET_HEADS), pos)
    ret_o, ret_new = retention(rq, rk, heads(rv, RET_HEADS), ret_s.astype(f32), c)
    ret_o = head_rmsnorm(ret_o, p['ret_out_norm'][j]) * jax.nn.silu(rg)
    logf = jnp.logaddexp(jnp.log(lb), jnp.log1p(-lb) + jax.nn.log_sigmoid(hf))
    hk = (1.0 - lb) * jax.nn.sigmoid(-hf)
    hg_o, hg_new = hgrn2_chunked(heads(jax.nn.silu(hq), HG_HEADS), heads(hk, HG_HEADS),
                                 heads(hi, HG_HEADS), heads(logf, HG_HEADS), hg_s.astype(f32), c)
    hg_o = head_rmsnorm(hg_o, p['hg_out_norm'][j]) * jax.nn.silu(hog)
    out = jnp.concatenate([ret_o, hg_o], axis=-1).astype(h.dtype) @ p['even_w_out'][j]
    return out, ret_new, hg_new


def odd_mixer(h, c, lru_h, lru_conv, dn_s, dn_conv, p, j):
    B, T, _ = h.shape
    f32 = jnp.float32
    z = (h @ p['odd_w_in'][j]).astype(f32)
    lx, lg, dq, dk, dv, dog, db, da = split_cols(z, ODD_SPLITS)
    lx, lru_conv_new = causal_conv(lx, lru_conv.astype(f32), p['lru_conv_w'][j])
    lx = lx + p['lru_conv_b'][j]
    hseq, lru_h_new = rg_lru(lx, lru_h.astype(f32), p['lru_w_a'][j], p['lru_b_a'][j],
                             p['lru_w_x'][j], p['lru_b_x'][j], p['lru_lambda'][j])
    lru_o = jax.nn.gelu(lg) * hseq
    qkv, dn_conv_new = causal_conv(jnp.concatenate([dq, dk, dv], axis=-1), dn_conv.astype(f32),
                                   p['dn_conv_w'][j])
    qkv = jax.nn.silu(qkv)
    q, k, v = split_cols(qkv, [DN_HEADS * DN_DK, DN_HEADS * DN_DK, DN_HEADS * DN_DV])
    q = l2norm(q.reshape(B, T, DN_HEADS, DN_DK)) * (DN_DK ** -0.5)
    k = l2norm(k.reshape(B, T, DN_HEADS, DN_DK))
    v = v.reshape(B, T, DN_HEADS, DN_DV)
    beta = jax.nn.sigmoid(db)
    loga = -jnp.exp(p['dn_a_log'][j]) * jax.nn.softplus(da + p['dn_dt_bias'][j])
    dn_o, dn_new = gated_delta_chunked(q, k, v, beta, loga, dn_s.astype(f32), c)
    dn_o = head_rmsnorm(dn_o, p['dn_out_norm'][j]) * jax.nn.silu(dog)
    out = jnp.concatenate([lru_o, dn_o], axis=-1).astype(h.dtype) @ p['odd_w_out'][j]
    return out, lru_h_new, lru_conv_new, dn_new, dn_conv_new


def trunk(x, pos, ret_s, hg_s, lru_h, lru_conv, dn_s, dn_conv, p):
    c = min(CHUNK, x.shape[1])
    lb_all = jnp.cumsum(jax.nn.softmax(p['hg_lb_logits'].astype(jnp.float32), axis=0), axis=0)
    n_ret, n_hg, n_lh, n_lc, n_dn, n_dc = [], [], [], [], [], []
    for l in range(DEPTH):
        j = l // 2
        x = x + 0.5 * swiglu(rmsnorm(x, p['ffn1_norm'][l]), p['ffn1_w_in'][l], p['ffn1_w_out'][l])
        hn = rmsnorm(x, p['mix_norm'][l])
        if l % 2 == 0:
            mix, rs, hs = even_mixer(hn, pos, c, ret_s[j], hg_s[j], lb_all[j], p, j)
            n_ret.append(rs)
            n_hg.append(hs)
        else:
            mix, lh, lc, ds, dc = odd_mixer(hn, c, lru_h[j], lru_conv[j], dn_s[j], dn_conv[j], p, j)
            n_lh.append(lh)
            n_lc.append(lc)
            n_dn.append(ds)
            n_dc.append(dc)
        x = x + mix
        x = x + 0.5 * swiglu(rmsnorm(x, p['ffn2_norm'][l]), p['ffn2_w_in'][l], p['ffn2_w_out'][l])
    y = rmsnorm(x, p['final_norm'])
    return (y, jnp.stack(n_ret), jnp.stack(n_hg), jnp.stack(n_lh), jnp.stack(n_lc),
            jnp.stack(n_dn), jnp.stack(n_dc))


def setup_inputs(seed: int = 0) -> dict:
    key = jax.random.key(seed)
    keys = iter(jax.random.split(key, 40))
    f32 = jnp.float32

    def nrm(shape, scale):
        return jax.random.normal(next(keys), shape, f32) * scale

    def unif(shape, lo, hi):
        return jax.random.uniform(next(keys), shape, f32, lo, hi)

    dsc = D_MODEL ** -0.5
    a8 = unif((N_ODD, LRU_WIDTH), 0.9, 0.999)
    a = a8 ** (1.0 / LRU_C)
    dt = jnp.exp(unif((N_ODD, DN_HEADS), math.log(1e-3), math.log(1e-1)))
    a_heads = unif((N_ODD, DN_HEADS), 1.0, 16.0)
    return {
        'x_prompt': nrm((BATCH, SEQ, D_MODEL), 1.0),
        'x_sample': nrm((DEC_BATCH, DEC_SEQ, D_MODEL), 1.0),
        'state_ret': nrm((N_EVEN, DEC_BATCH, RET_HEADS, RET_DK, RET_DV), 2.0),
        'state_hgrn': nrm((N_EVEN, DEC_BATCH, HG_HEADS, HG_DK, HG_DV), 1.0),
        'state_lru_h': nrm((N_ODD, DEC_BATCH, LRU_WIDTH), 0.5),
        'state_lru_conv': nrm((N_ODD, DEC_BATCH, CONV_W - 1, LRU_WIDTH), 1.0),
        'state_dn': nrm((N_ODD, DEC_BATCH, DN_HEADS, DN_DK, DN_DV), 0.3),
        'state_dn_conv': nrm((N_ODD, DEC_BATCH, CONV_W - 1, DN_CONV_CH), 1.0),
        'ffn1_norm': 1.0 + nrm((DEPTH, D_MODEL), 0.02),
        'ffn1_w_in': nrm((DEPTH, D_MODEL, 2 * FF_DIM), dsc),
        'ffn1_w_out': nrm((DEPTH, FF_DIM, D_MODEL), FF_DIM ** -0.5),
        'mix_norm': 1.0 + nrm((DEPTH, D_MODEL), 0.02),
        'ffn2_norm': 1.0 + nrm((DEPTH, D_MODEL), 0.02),
        'ffn2_w_in': nrm((DEPTH, D_MODEL, 2 * FF_DIM), dsc),
        'ffn2_w_out': nrm((DEPTH, FF_DIM, D_MODEL), FF_DIM ** -0.5),
        'final_norm': 1.0 + nrm((D_MODEL,), 0.02),
        'even_w_in': nrm((N_EVEN, D_MODEL, EVEN_IN), dsc),
        'even_w_out': nrm((N_EVEN, EVEN_MIX, D_MODEL), EVEN_MIX ** -0.5),
        'ret_out_norm': 1.0 + nrm((N_EVEN, RET_HEADS * RET_DV), 0.02),
        'hg_out_norm': 1.0 + nrm((N_EVEN, HG_HEADS * HG_DV), 0.02),
        'hg_lb_logits': nrm((N_EVEN + 1, HG_HEADS * HG_DK), 0.5),
        'odd_w_in': nrm((N_ODD, D_MODEL, ODD_IN), dsc),
        'odd_w_out': nrm((N_ODD, ODD_MIX, D_MODEL), ODD_MIX ** -0.5),
        'lru_conv_w': nrm((N_ODD, CONV_W, LRU_WIDTH), CONV_W ** -0.5),
        'lru_conv_b': nrm((N_ODD, LRU_WIDTH), 0.02),
        'lru_w_a': nrm((N_ODD, LRU_BLOCKS, LRU_BS, LRU_BS), LRU_BS ** -0.5),
        'lru_b_a': nrm((N_ODD, LRU_WIDTH), 0.02),
        'lru_w_x': nrm((N_ODD, LRU_BLOCKS, LRU_BS, LRU_BS), LRU_BS ** -0.5),
        'lru_b_x': nrm((N_ODD, LRU_WIDTH), 0.02),
        'lru_lambda': jnp.log(a) - jnp.log1p(-a),
        'dn_conv_w': nrm((N_ODD, CONV_W, DN_CONV_CH), CONV_W ** -0.5),
        'dn_a_log': jnp.log(a_heads),
        'dn_dt_bias': dt + jnp.log(-jnp.expm1(-dt)),
        'dn_out_norm': 1.0 + nrm((N_ODD, DN_HEADS * DN_DV), 0.02),
    }


def reference(x_prompt, x_sample, state_ret, state_hgrn, state_lru_h, state_lru_conv, state_dn,
              state_dn_conv, ffn1_norm, ffn1_w_in, ffn1_w_out, mix_norm, ffn2_norm, ffn2_w_in,
              ffn2_w_out, final_norm, even_w_in, even_w_out, ret_out_norm, hg_out_norm,
              hg_lb_logits, odd_w_in, odd_w_out, lru_conv_w, lru_conv_b, lru_w_a, lru_b_a, lru_w_x,
              lru_b_x, lru_lambda, dn_conv_w, dn_a_log, dn_dt_bias, dn_out_norm):
    p = {
        'ffn1_norm': ffn1_norm, 'ffn1_w_in': ffn1_w_in, 'ffn1_w_out': ffn1_w_out,
        'mix_norm': mix_norm, 'ffn2_norm': ffn2_norm, 'ffn2_w_in': ffn2_w_in,
        'ffn2_w_out': ffn2_w_out, 'final_norm': final_norm, 'even_w_in': even_w_in,
        'even_w_out': even_w_out, 'ret_out_norm': ret_out_norm, 'hg_out_norm': hg_out_norm,
        'hg_lb_logits': hg_lb_logits, 'odd_w_in': odd_w_in, 'odd_w_out': odd_w_out,
        'lru_conv_w': lru_conv_w, 'lru_conv_b': lru_conv_b, 'lru_w_a': lru_w_a,
        'lru_b_a': lru_b_a, 'lru_w_x': lru_w_x, 'lru_b_x': lru_b_x, 'lru_lambda': lru_lambda,
        'dn_conv_w': dn_conv_w, 'dn_a_log': dn_a_log, 'dn_dt_bias': dn_dt_bias,
        'dn_out_norm': dn_out_norm,
    }
    f32 = jnp.float32
    bp = x_prompt.shape[0]
    z_ret = jnp.zeros((N_EVEN, bp, RET_HEADS, RET_DK, RET_DV), f32)
    z_hg = jnp.zeros((N_EVEN, bp, HG_HEADS, HG_DK, HG_DV), f32)
    z_lh = jnp.zeros((N_ODD, bp, LRU_WIDTH), f32)
    z_lc = jnp.zeros((N_ODD, bp, CONV_W - 1, LRU_WIDTH), f32)
    z_dn = jnp.zeros((N_ODD, bp, DN_HEADS, DN_DK, DN_DV), f32)
    z_dc = jnp.zeros((N_ODD, bp, CONV_W - 1, DN_CONV_CH), f32)
    pos_p = jnp.arange(x_prompt.shape[1])
    y_prompt, ret_p, hg_p, lh_p, lc_p, dn_p, dc_p = trunk(
        x_prompt, pos_p, z_ret, z_hg, z_lh, z_lc, z_dn, z_dc, p)
    pos_s = PAST_LEN + jnp.arange(x_sample.shape[1])
    y_sample, ret_s, hg_s, lh_s, lc_s, dn_s, dc_s = trunk(
        x_sample, pos_s, state_ret, state_hgrn, state_lru_h, state_lru_conv, state_dn,
        state_dn_conv, p)
    return (y_prompt, y_sample, ret_p, ret_s, hg_p, hg_s, lh_p, lh_s, lc_p, lc_s, dn_p, dn_s,
            dc_p, dc_s)
```

```python
import functools
import math

import jax
import jax.numpy as jnp
from jax import lax
from jax.experimental import pallas as pl
from jax.experimental.pallas import tpu as pltpu

F32 = jnp.float32
BF16 = jnp.bfloat16

D_MODEL = 1024
FF_DIM = 2816
EPS = 1e-6
CHUNK = 64
CONV_W = 4
HEADS = 4
HD = 128
ROPE_BASE = 10000.0
LRU_WIDTH = 512
LRU_C = 8.0
GRP = HEADS * HD
EVEN_IN = 8 * GRP
ODD_IN_PAD = 6 * GRP + HD
SUB = 16

VMEM_LIMIT = 56 * 1024 * 1024

FFN_TM = 1024
FFN_TF = 256
SEQ_TILE = 512

RET_LOG_G = [math.log1p(-(2.0 ** (-5.0 - h))) for h in range(HEADS)]


def _rms(x, g):
    return x * lax.rsqrt(jnp.mean(x * x, axis=-1, keepdims=True) + EPS) * g


def _silu(x):
    return x * jax.nn.sigmoid(x)


def _dot(a, b):
    return jnp.dot(a, b, preferred_element_type=F32)


def _dot_nt(a, b):
    return lax.dot_general(a, b, (((1,), (1,)), ((), ())), preferred_element_type=F32)


def _dot_tn(a, b):
    return lax.dot_general(a, b, (((0,), (0,)), ((), ())), preferred_element_type=F32)


def _head_norm_gate(o, g, gate):
    o = o * lax.rsqrt(jnp.mean(o * o, axis=-1, keepdims=True) + EPS) * g
    return o * _silu(gate)


def _cumsum_rows(x):
    rows = lax.broadcasted_iota(jnp.int32, x.shape, 0)
    k = 1
    while k < x.shape[0]:
        x = jnp.where(rows >= k, x + pltpu.roll(x, k, 0), x)
        k *= 2
    return x


def _ffn_kernel(x_ref, g_ref, wg_ref, wu_ref, wo_ref, fin_ref, o_ref, h_ref, acc_ref, *,
                apply_final):
    j = pl.program_id(1)

    @pl.when(j == 0)
    def _():
        h_ref[...] = _rms(x_ref[...], g_ref[...]).astype(BF16)
        acc_ref[...] = jnp.zeros_like(acc_ref)

    h = h_ref[...]
    gate = _dot(h, wg_ref[...].astype(BF16))
    up = _dot(h, wu_ref[...].astype(BF16))
    act = (_silu(gate) * up).astype(BF16)
    acc_ref[...] += _dot(act, wo_ref[...].astype(BF16))

    @pl.when(j == pl.num_programs(1) - 1)
    def _():
        y = x_ref[...] + 0.5 * acc_ref[...]
        if apply_final:
            y = _rms(y, fin_ref[...])
        o_ref[...] = y


def _ffn(x, norm_g, w_in, w_out, final_g, apply_final):
    shape = x.shape
    x2 = x.reshape(-1, D_MODEL)
    m = x2.shape[0]
    tm = min(FFN_TM, m)
    nf = FF_DIM // FFN_TF
    out = pl.pallas_call(
        functools.partial(_ffn_kernel, apply_final=apply_final),
        grid=(m // tm, nf),
        in_specs=[
            pl.BlockSpec((tm, D_MODEL), lambda i, j: (i, 0)),
            pl.BlockSpec((1, D_MODEL), lambda i, j: (0, 0)),
            pl.BlockSpec((D_MODEL, FFN_TF), lambda i, j: (0, j)),
            pl.BlockSpec((D_MODEL, FFN_TF), lambda i, j: (0, j + nf)),
            pl.BlockSpec((FFN_TF, D_MODEL), lambda i, j: (j, 0)),
            pl.BlockSpec((1, D_MODEL), lambda i, j: (0, 0)),
        ],
        out_specs=pl.BlockSpec((tm, D_MODEL), lambda i, j: (i, 0)),
        out_shape=jax.ShapeDtypeStruct((m, D_MODEL), F32),
        scratch_shapes=[pltpu.VMEM((tm, D_MODEL), BF16), pltpu.VMEM((tm, D_MODEL), F32)],
        compiler_params=pltpu.CompilerParams(
            dimension_semantics=("parallel", "arbitrary"), vmem_limit_bytes=VMEM_LIMIT),
        name="ffn",
    )(x2, norm_g.reshape(1, D_MODEL), w_in, w_in, w_out, final_g.reshape(1, D_MODEL))
    return out.reshape(shape)


def _even_kernel(x_ref, cos_ref, sin_ref, ng_ref, win_ref, wout_ref, rg_ref, hg_ref, lbl_ref,
                 sret_ref, shg_ref, xo_ref, sret_o, shg_o, z_ref, mix_ref, hgt_ref, *,
                 ns, nc, layer_j):
    t = pl.program_id(1)
    tt = ns * nc * CHUNK

    @pl.when(t == 0)
    def _():
        sret_o[...] = sret_ref[...]
        for s in range(ns):
            for h in range(HEADS):
                hgt_ref[s, h] = shg_ref[s, h].T

    x = x_ref[...].reshape(tt, D_MODEL)
    hn = _rms(x, ng_ref[...]).astype(BF16)
    z_ref[...] = _dot(hn, win_ref[...])

    ri = lax.broadcasted_iota(jnp.int32, (CHUNK, CHUNK), 0)
    ci = lax.broadcasted_iota(jnp.int32, (CHUNK, CHUNK), 1)
    absd = jnp.abs(ri - ci).astype(F32)
    pcol = lax.broadcasted_iota(jnp.int32, (CHUNK, 1), 0).astype(F32)
    intra = [jnp.exp(RET_LOG_G[h] * absd) for h in range(HEADS)]
    qdec = [jnp.exp(RET_LOG_G[h] * (pcol + 1.0)) * (HD ** -0.5) for h in range(HEADS)]
    kdec = [jnp.exp(RET_LOG_G[h] * (CHUNK - 1.0 - pcol)) for h in range(HEADS)]
    sdec = [math.exp(RET_LOG_G[h] * CHUNK) for h in range(HEADS)]

    lbl = lbl_ref[...]
    e = jnp.exp(lbl - jnp.max(lbl, axis=0, keepdims=True))
    sm = e / jnp.sum(e, axis=0, keepdims=True)
    lb_all = jnp.sum(sm[:layer_j + 1], axis=0, keepdims=True)

    nb = CHUNK // SUB
    rows128 = lax.broadcasted_iota(jnp.int32, (CHUNK, HD), 0)
    t_io = lax.broadcasted_iota(jnp.int32, (nb, SUB, HD), 1)
    lane_io = lax.broadcasted_iota(jnp.int32, (nb, SUB, CHUNK), 2)
    blk_io = lax.broadcasted_iota(jnp.int32, (nb, SUB, CHUNK), 0)

    def chunk_body(i, carry):
        if nc == 1:
            s, c = i, 0
        else:
            s, c = 0, i
        r0 = pl.multiple_of(i * CHUNK, CHUNK)
        c0 = pl.multiple_of(c * CHUNK, CHUNK)
        cos = cos_ref[pl.ds(c0, CHUNK), :]
        sin = sin_ref[pl.ds(c0, CHUNK), :]

        def zcol(g, h):
            return z_ref[pl.ds(r0, CHUNK), pl.ds(g * GRP + h * HD, HD)]

        for h in range(HEADS):
            q = zcol(0, h)
            k = zcol(1, h)
            v = zcol(2, h).astype(BF16)
            q = q * cos + pltpu.roll(q, HD // 2, 1) * sin
            k = k * cos + pltpu.roll(k, HD // 2, 1) * sin
            qs = (q * (HD ** -0.5)).astype(BF16)
            st = sret_o[s, h]
            att = _dot_nt(qs, k.astype(BF16)) * intra[h]
            o = _dot(att.astype(BF16), v) + _dot((q * qdec[h]).astype(BF16), st.astype(BF16))
            sret_o[s, h] = st * sdec[h] + _dot_tn((k * kdec[h]).astype(BF16), v)
            o = _head_norm_gate(o, rg_ref[:, pl.ds(h * HD, HD)], zcol(3, h))
            mix_ref[pl.ds(r0, CHUNK), pl.ds(h * HD, HD)] = o.astype(BF16)

        for h in range(HEADS):
            lb = lb_all[:, h * HD:(h + 1) * HD]
            hq = zcol(4, h)
            sig = jax.nn.sigmoid(zcol(5, h))
            v = zcol(6, h).astype(BF16)
            q = _silu(hq)
            k = (1.0 - lb) * (1.0 - sig)
            b = _cumsum_rows(jnp.log(lb + (1.0 - lb) * sig))
            stt = hgt_ref[s, h]
            b4 = b.reshape(nb, SUB, HD)
            q4 = q.reshape(nb, SUB, HD)
            k4 = k.reshape(nb, SUB, HD)
            diag = jnp.zeros((nb, SUB, CHUNK), F32)
            for sp in range(SUB):
                arg = jnp.where(t_io >= sp, b4 - b4[:, sp:sp + 1, :], -jnp.inf)
                r = jnp.sum(jnp.exp(arg) * q4 * k4[:, sp:sp + 1, :], axis=-1, keepdims=True)
                diag = jnp.where(lane_io == blk_io * SUB + sp, r, diag)
            diag = diag.reshape(CHUNK, CHUNK)
            parts = [diag[0:SUB]]
            for blk in range(1, nb):
                lo = blk * SUB
                rb = b[lo - 1:lo, :]
                qt = (q[lo:lo + SUB] * jnp.exp(b[lo:lo + SUB] - rb)).astype(BF16)
                kt = (k * jnp.exp(jnp.where(rows128 < lo, rb - b, -jnp.inf))).astype(BF16)
                parts.append(diag[lo:lo + SUB] + _dot_nt(qt, kt))
            att = jnp.concatenate(parts, axis=0)
            o = _dot(att.astype(BF16), v) + _dot_nt((q * jnp.exp(b)).astype(BF16),
                                                    stt.astype(BF16))
            bend = b[CHUNK - 1:CHUNK, :]
            kd = (k * jnp.exp(bend - b)).astype(BF16)
            hgt_ref[s, h] = stt * jnp.exp(bend) + _dot_tn(v, kd)
            o = _head_norm_gate(o, hg_ref[:, pl.ds(h * HD, HD)], zcol(7, h))
            mix_ref[pl.ds(r0, CHUNK), pl.ds(GRP + h * HD, HD)] = o.astype(BF16)
        return carry

    lax.fori_loop(0, ns * nc, chunk_body, 0)

    xo_ref[...] = (x + _dot(mix_ref[...], wout_ref[...])).reshape(xo_ref.shape)

    @pl.when(t == pl.num_programs(1) - 1)
    def _():
        for s in range(ns):
            for h in range(HEADS):
                shg_o[s, h] = hgt_ref[s, h].T


def _seq_tiling(b, t):
    if t >= SEQ_TILE:
        return 1, SEQ_TILE // CHUNK
    return min(b, SEQ_TILE // t), t // CHUNK


def _even_mixer(x, cos, sin, norm_g, w_in, w_out, ret_g, hg_g, lb_logits, s_ret, s_hg, layer_j):
    b, t, _ = x.shape
    ns, nc = _seq_tiling(b, t)
    tt = ns * nc * CHUNK
    grid = (b // ns, t // (nc * CHUNK))
    st_spec = pl.BlockSpec((ns, HEADS, HD, HD), lambda i, j: (i, 0, 0, 0))
    x_spec = pl.BlockSpec((ns, nc * CHUNK, D_MODEL), lambda i, j: (i, j, 0))

    def full(a):
        return pl.BlockSpec(a.shape, lambda i, j: (0,) * a.ndim)

    tab_spec = pl.BlockSpec((nc * CHUNK, HD), lambda i, j: (j, 0))
    args = (x, cos, sin, norm_g.reshape(1, D_MODEL), w_in, w_out, ret_g.reshape(1, GRP),
            hg_g.reshape(1, GRP), lb_logits, s_ret, s_hg)
    in_specs = [x_spec, tab_spec, tab_spec] + [full(a) for a in args[3:9]] + [st_spec, st_spec]
    return pl.pallas_call(
        functools.partial(_even_kernel, ns=ns, nc=nc, layer_j=layer_j),
        grid=grid,
        in_specs=in_specs,
        out_specs=[x_spec, st_spec, st_spec],
        out_shape=[jax.ShapeDtypeStruct(x.shape, F32),
                   jax.ShapeDtypeStruct(s_ret.shape, F32),
                   jax.ShapeDtypeStruct(s_hg.shape, F32)],
        scratch_shapes=[pltpu.VMEM((tt, EVEN_IN), F32), pltpu.VMEM((tt, D_MODEL), BF16),
                        pltpu.VMEM((ns, HEADS, HD, HD), F32)],
        compiler_params=pltpu.CompilerParams(
            dimension_semantics=("parallel", "arbitrary"), vmem_limit_bytes=VMEM_LIMIT),
        name="even_mixer",
    )(*args)


def _softplus(x):
    return jnp.maximum(x, 0.0) + jnp.log1p(jnp.exp(-jnp.abs(x)))


def _gelu_tanh(x):
    return 0.5 * x * (1.0 + jnp.tanh(math.sqrt(2.0 / math.pi) * (x + 0.044715 * (x * x * x))))


def _odd_kernel(x_ref, ng_ref, win_ref, wout_ref, cw_ref, lcb_ref, wa_ref, ba_ref, wx_ref,
                bx_ref, lam_ref, dnp_ref, dg_ref, lh_ref, lc_ref, ds_ref, dc_ref,
                xo_ref, lh_o, lc_o, ds_o, dc_o, z_ref, mix_ref, xp_ref, *, ns, nc):
    t = pl.program_id(1)
    tt = ns * nc * CHUNK
    conv_ch = LRU_WIDTH + 3 * GRP
    hist = CONV_W - 1
    top = 8

    @pl.when(t == 0)
    def _():
        lh_o[...] = lh_ref[...]
        lc_o[...] = lc_ref[...]
        ds_o[...] = ds_ref[...]
        dc_o[...] = dc_ref[...]

    x = x_ref[...].reshape(tt, D_MODEL)
    hn = _rms(x, ng_ref[...]).astype(BF16)
    z_ref[...] = _dot(hn, win_ref[...])

    sp_lam = _softplus(-lam_ref[...])
    dt_bias = dnp_ref[0:1, :]
    neg_a = -jnp.exp(dnp_ref[1:2, :])
    ri = lax.broadcasted_iota(jnp.int32, (CHUNK, CHUNK), 0)
    ci = lax.broadcasted_iota(jnp.int32, (CHUNK, CHUNK), 1)
    rows_w = lax.broadcasted_iota(jnp.int32, (CHUNK, LRU_WIDTH), 0)

    def chunk_body(i, carry):
        s = i if nc == 1 else 0
        r0 = pl.multiple_of(i * CHUNK, CHUNK)

        xp_ref[top - hist:top, 0:LRU_WIDTH] = lc_o[s]
        xp_ref[top - hist:top, LRU_WIDTH:conv_ch] = dc_o[s]
        xp_ref[top:top + CHUNK, 0:LRU_WIDTH] = z_ref[pl.ds(r0, CHUNK), 0:LRU_WIDTH]
        xp_ref[top:top + CHUNK, LRU_WIDTH:conv_ch] = z_ref[pl.ds(r0, CHUNK), 2 * GRP:5 * GRP]
        lc_o[s] = xp_ref[top + CHUNK - hist:top + CHUNK, 0:LRU_WIDTH]
        dc_o[s] = xp_ref[top + CHUNK - hist:top + CHUNK, LRU_WIDTH:conv_ch]
        y = xp_ref[top - hist:top - hist + CHUNK, :] * cw_ref[0:1, :]
        for jj in range(1, CONV_W):
            y = y + xp_ref[top - hist + jj:top - hist + jj + CHUNK, :] * cw_ref[jj:jj + 1, :]

        lx = y[:, 0:LRU_WIDTH] + lcb_ref[...]
        xb = lx.astype(BF16)
        r = jax.nn.sigmoid(_dot(xb, wa_ref[...]) + ba_ref[...])
        ig = jax.nn.sigmoid(_dot(xb, wx_ref[...]) + bx_ref[...])
        a = jnp.exp(-LRU_C * r * sp_lam)
        u = jnp.sqrt(1.0 - a * a) * (ig * lx)
        k = 1
        while k < CHUNK:
            m = rows_w >= k
            u = jnp.where(m, a * pltpu.roll(u, k, 0) + u, u)
            a = jnp.where(m, a * pltpu.roll(a, k, 0), a)
            k *= 2
        hseq = a * lh_o[s] + u
        lh_o[s] = hseq[CHUNK - 1:CHUNK]
        lg = z_ref[pl.ds(r0, CHUNK), GRP:2 * GRP]
        mix_ref[pl.ds(r0, CHUNK), 0:LRU_WIDTH] = (_gelu_tanh(lg) * hseq).astype(BF16)

        zs = z_ref[pl.ds(r0, CHUNK), 6 * GRP:6 * GRP + HD]
        beta_t = jax.nn.sigmoid(zs)
        g_t = _cumsum_rows(neg_a * _softplus(zs + dt_bias))
        g_tt = g_t.T
        for h in range(HEADS):
            base = LRU_WIDTH + h * HD
            q = _silu(y[:, base:base + HD])
            kk_ = _silu(y[:, base + GRP:base + GRP + HD])
            v = _silu(y[:, base + 2 * GRP:base + 2 * GRP + HD])
            q = q * lax.rsqrt(jnp.sum(q * q, axis=-1, keepdims=True) + EPS) * (HD ** -0.5)
            kn = kk_ * lax.rsqrt(jnp.sum(kk_ * kk_, axis=-1, keepdims=True) + EPS)
            knb = kn.astype(BF16)
            qk_in = jnp.concatenate([q.astype(BF16), knb], axis=0)
            st = ds_o[s, h]
            both_k = _dot_nt(qk_in, knb)
            both_s = _dot(qk_in, st.astype(BF16))
            gcol = g_t[:, HEADS + h:HEADS + h + 1]
            grow = g_tt[HEADS + h:HEADS + h + 1, :]
            rel = gcol - grow
            e_strict = jnp.exp(jnp.where(ri > ci, rel, -jnp.inf))
            e_incl = jnp.where(ri == ci, 1.0, e_strict)
            beta = beta_t[:, h:h + 1]
            eg = jnp.exp(gcol)
            amat = beta * both_k[CHUNK:] * e_strict
            w = beta * (v - eg * both_s[CHUNK:])
            for jj in range(CHUNK - 1):
                w = w - amat[:, jj:jj + 1] * w[jj:jj + 1, :]
            o = eg * both_s[:CHUNK] + _dot((both_k[:CHUNK] * e_incl).astype(BF16),
                                            w.astype(BF16))
            gend = gcol[CHUNK - 1:CHUNK, :]
            kd = (kn * jnp.exp(gend - gcol)).astype(BF16)
            ds_o[s, h] = st * jnp.exp(gend) + _dot_tn(kd, w.astype(BF16))
            o = _head_norm_gate(o, dg_ref[:, pl.ds(h * HD, HD)],
                                z_ref[pl.ds(r0, CHUNK), pl.ds(5 * GRP + h * HD, HD)])
            mix_ref[pl.ds(r0, CHUNK), pl.ds(LRU_WIDTH + h * HD, HD)] = o.astype(BF16)
        return carry

    lax.fori_loop(0, ns * nc, chunk_body, 0)

    xo_ref[...] = (x + _dot(mix_ref[...], wout_ref[...])).reshape(xo_ref.shape)


def _odd_mixer(x, norm_g, w_in, w_out, cw, lcb, wa, ba, wx, bx, lam, dnp, dn_g,
               s_lh, s_lc, s_dn, s_dc):
    b, t, _ = x.shape
    ns, nc = _seq_tiling(b, t)
    tt = ns * nc * CHUNK
    grid = (b // ns, t // (nc * CHUNK))
    x_spec = pl.BlockSpec((ns, nc * CHUNK, D_MODEL), lambda i, j: (i, j, 0))

    def full(a):
        return pl.BlockSpec(a.shape, lambda i, j: (0,) * a.ndim)

    def st_spec(a):
        return pl.BlockSpec((ns,) + a.shape[1:], lambda i, j: (i,) + (0,) * (a.ndim - 1))

    params = (norm_g.reshape(1, D_MODEL), w_in, w_out, cw, lcb, wa, ba, wx, bx, lam, dnp, dn_g)
    states = (s_lh, s_lc, s_dn, s_dc)
    return pl.pallas_call(
        functools.partial(_odd_kernel, ns=ns, nc=nc),
        grid=grid,
        in_specs=[x_spec] + [full(a) for a in params] + [st_spec(a) for a in states],
        out_specs=[x_spec] + [st_spec(a) for a in states],
        out_shape=[jax.ShapeDtypeStruct(x.shape, F32)]
        + [jax.ShapeDtypeStruct(a.shape, F32) for a in states],
        scratch_shapes=[pltpu.VMEM((tt, ODD_IN_PAD), F32), pltpu.VMEM((tt, D_MODEL), BF16),
                        pltpu.VMEM((8 + CHUNK, LRU_WIDTH + 3 * GRP), F32)],
        compiler_params=pltpu.CompilerParams(
            dimension_semantics=("parallel", "arbitrary"), vmem_limit_bytes=VMEM_LIMIT),
        name="odd_mixer",
    )(x, *params, *states)


def _rope_tables(pos):
    half = HD // 2
    freq = ROPE_BASE ** (-jnp.arange(half, dtype=F32) / half)
    ang = pos.astype(F32)[:, None] * freq[None, :]
    cos, sin = jnp.cos(ang), jnp.sin(ang)
    return jnp.concatenate([cos, cos], axis=-1), jnp.concatenate([-sin, sin], axis=-1)


def _block_diag(w):
    n, bs, _ = w.shape
    eye = jnp.eye(n, dtype=w.dtype)
    return (eye[:, None, :, None] * w[:, :, None, :]).reshape(n * bs, n * bs)


def kernel(x_prompt, x_sample, state_ret, state_hgrn, state_lru_h, state_lru_conv, state_dn,
           state_dn_conv, ffn1_norm, ffn1_w_in, ffn1_w_out, mix_norm, ffn2_norm, ffn2_w_in,
           ffn2_w_out, final_norm, even_w_in, even_w_out, ret_out_norm, hg_out_norm,
           hg_lb_logits, odd_w_in, odd_w_out, lru_conv_w, lru_conv_b, lru_w_a, lru_b_a, lru_w_x,
           lru_b_x, lru_lambda, dn_conv_w, dn_a_log, dn_dt_bias, dn_out_norm):
    depth = ffn1_norm.shape[0]
    bp, tp, _ = x_prompt.shape
    bs, ts, _ = x_sample.shape
    past_len = 2048
    tabs = (_rope_tables(jnp.arange(tp)), _rope_tables(past_len + jnp.arange(ts)))
    xs = [x_prompt, x_sample]
    nb = (bp, bs)
    outs = {k: ([], []) for k in ("ret", "hg", "lh", "lc", "dn", "dc")}

    for l in range(depth):
        j = l // 2
        last = l == depth - 1
        for g in range(2):
            xs[g] = _ffn(xs[g], ffn1_norm[l], ffn1_w_in[l], ffn1_w_out[l], final_norm, False)
        if l % 2 == 0:
            w_in = even_w_in[j].astype(BF16)
            w_out = even_w_out[j].astype(BF16)
            for g in range(2):
                if g == 0:
                    s_ret = jnp.zeros((bp, HEADS, HD, HD), F32)
                    s_hg = jnp.zeros((bp, HEADS, HD, HD), F32)
                else:
                    s_ret, s_hg = state_ret[j], state_hgrn[j]
                xs[g], n_ret, n_hg = _even_mixer(
                    xs[g], tabs[g][0], tabs[g][1], mix_norm[l], w_in, w_out, ret_out_norm[j],
                    hg_out_norm[j], hg_lb_logits, s_ret, s_hg, j)
                outs["ret"][g].append(n_ret)
                outs["hg"][g].append(n_hg)
        else:
            w_in = jnp.pad(odd_w_in[j], ((0, 0), (0, ODD_IN_PAD - odd_w_in.shape[-1]))).astype(BF16)
            w_out = odd_w_out[j].astype(BF16)
            cw = jnp.concatenate([lru_conv_w[j], dn_conv_w[j]], axis=-1)
            wa = _block_diag(lru_w_a[j]).astype(BF16)
            wx = _block_diag(lru_w_x[j]).astype(BF16)
            dnp = jnp.zeros((2, HD), F32)
            dnp = dnp.at[0, HEADS:2 * HEADS].set(dn_dt_bias[j])
            dnp = dnp.at[1, HEADS:2 * HEADS].set(dn_a_log[j])
            for g in range(2):
                if g == 0:
                    s_lh = jnp.zeros((bp, 1, LRU_WIDTH), F32)
                    s_lc = jnp.zeros((bp, CONV_W - 1, LRU_WIDTH), F32)
                    s_dn = jnp.zeros((bp, HEADS, HD, HD), F32)
                    s_dc = jnp.zeros((bp, CONV_W - 1, 3 * GRP), F32)
                else:
                    s_lh = state_lru_h[j].reshape(bs, 1, LRU_WIDTH)
                    s_lc, s_dn, s_dc = state_lru_conv[j], state_dn[j], state_dn_conv[j]
                xs[g], n_lh, n_lc, n_dn, n_dc = _odd_mixer(
                    xs[g], mix_norm[l], w_in, w_out, cw, lru_conv_b[j].reshape(1, -1), wa,
                    lru_b_a[j].reshape(1, -1), wx, lru_b_x[j].reshape(1, -1),
                    lru_lambda[j].reshape(1, -1), dnp, dn_out_norm[j].reshape(1, -1),
                    s_lh, s_lc, s_dn, s_dc)
                outs["lh"][g].append(n_lh.reshape(nb[g], LRU_WIDTH))
                outs["lc"][g].append(n_lc)
                outs["dn"][g].append(n_dn)
                outs["dc"][g].append(n_dc)
        for g in range(2):
            xs[g] = _ffn(xs[g], ffn2_norm[l], ffn2_w_in[l], ffn2_w_out[l], final_norm, last)

    res = [xs[0], xs[1]]
    for k in ("ret", "hg", "lh", "lc", "dn", "dc"):
        for g in range(2):
            res.append(jnp.stack(outs[k][g]))
    return tuple(res)
```

```python
import functools
import math

import jax
import jax.numpy as jnp
from jax import lax
from jax.experimental import pallas as pl
from jax.experimental.pallas import tpu as pltpu

F32 = jnp.float32
BF16 = jnp.bfloat16

D_MODEL = 1024
FF_DIM = 2816
EPS = 1e-6
CHUNK = 64
CONV_W = 4
HEADS = 4
HD = 128
ROPE_BASE = 10000.0
LRU_WIDTH = 512
LRU_C = 8.0
GRP = HEADS * HD
EVEN_IN = 8 * GRP
ODD_IN_PAD = 6 * GRP + HD
SUB = 16

VMEM_LIMIT = 56 * 1024 * 1024

FFN_TM = 512
FFN_TF = 256
SEQ_TILE = 512

RET_LOG_G = [math.log1p(-(2.0 ** (-5.0 - h))) for h in range(HEADS)]


def _rms(x, g):
    return x * lax.rsqrt(jnp.mean(x * x, axis=-1, keepdims=True) + EPS) * g


def _silu(x):
    return x * jax.nn.sigmoid(x)


def _dot(a, b):
    return jnp.dot(a, b, preferred_element_type=F32)


def _dot_nt(a, b):
    return lax.dot_general(a, b, (((1,), (1,)), ((), ())), preferred_element_type=F32)


def _dot_tn(a, b):
    return lax.dot_general(a, b, (((0,), (0,)), ((), ())), preferred_element_type=F32)


def _head_norm_gate(o, g, gate):
    o = o * lax.rsqrt(jnp.mean(o * o, axis=-1, keepdims=True) + EPS) * g
    return o * _silu(gate)


def _cumsum_rows(x):
    rows = lax.broadcasted_iota(jnp.int32, x.shape, 0)
    k = 1
    while k < x.shape[0]:
        x = jnp.where(rows >= k, x + pltpu.roll(x, k, 0), x)
        k *= 2
    return x


def _ffn_kernel(x_ref, g_ref, win_ref, wout_ref, fin_ref, o_ref, act_ref, *, apply_final):
    x = x_ref[...]
    h = _rms(x, g_ref[...]).astype(BF16)
    for c in range(FF_DIM // FFN_TF):
        lo = c * FFN_TF
        gate = _dot(h, win_ref[:, lo:lo + FFN_TF])
        up = _dot(h, win_ref[:, FF_DIM + lo:FF_DIM + lo + FFN_TF])
        act_ref[:, lo:lo + FFN_TF] = (_silu(gate) * up).astype(BF16)
    y = x + 0.5 * _dot(act_ref[...], wout_ref[...])
    if apply_final:
        y = _rms(y, fin_ref[...])
    o_ref[...] = y


def _resident(block_shape, index_map):
    return pl.BlockSpec(block_shape, index_map, pipeline_mode=pl.Buffered(1))


def _ffn(x, norm_g, w_in, w_out, layer, final_g, apply_final):
    shape = x.shape
    x2 = x.reshape(-1, D_MODEL)
    m = x2.shape[0]
    tm = min(FFN_TM, m)
    out = pl.pallas_call(
        functools.partial(_ffn_kernel, apply_final=apply_final),
        grid=(m // tm,),
        in_specs=[
            pl.BlockSpec((tm, D_MODEL), lambda i: (i, 0)),
            _resident((None, 1, D_MODEL), lambda i: (layer, 0, 0)),
            _resident((None, D_MODEL, 2 * FF_DIM), lambda i: (layer, 0, 0)),
            _resident((None, FF_DIM, D_MODEL), lambda i: (layer, 0, 0)),
            _resident((1, D_MODEL), lambda i: (0, 0)),
        ],
        out_specs=pl.BlockSpec((tm, D_MODEL), lambda i: (i, 0)),
        out_shape=jax.ShapeDtypeStruct((m, D_MODEL), F32),
        scratch_shapes=[pltpu.VMEM((tm, FF_DIM), BF16)],
        compiler_params=pltpu.CompilerParams(
            dimension_semantics=("parallel",), vmem_limit_bytes=VMEM_LIMIT),
        name="ffn",
    )(x2, norm_g.reshape(-1, 1, D_MODEL), w_in, w_out, final_g.reshape(1, D_MODEL))
    return out.reshape(shape)


def _even_kernel(x_ref, cos_ref, sin_ref, ng_ref, win_ref, wout_ref, rg_ref, hg_ref, lbl_ref,
                 sret_ref, shg_ref, xo_ref, sret_o, shg_o, z_ref, mix_ref, hgt_ref, *,
                 ns, nc, layer_j):
    t = pl.program_id(1)
    tt = ns * nc * CHUNK

    @pl.when(t == 0)
    def _():
        sret_o[...] = sret_ref[...]
        for s in range(ns):
            for h in range(HEADS):
                hgt_ref[s, h] = shg_ref[s, h].T

    x = x_ref[...].reshape(tt, D_MODEL)
    hn = _rms(x, ng_ref[...]).astype(BF16)
    z_ref[...] = _dot(hn, win_ref[...])

    ri = lax.broadcasted_iota(jnp.int32, (CHUNK, CHUNK), 0)
    ci = lax.broadcasted_iota(jnp.int32, (CHUNK, CHUNK), 1)
    absd = jnp.abs(ri - ci).astype(F32)
    pcol = lax.broadcasted_iota(jnp.int32, (CHUNK, 1), 0).astype(F32)
    intra = [jnp.exp(RET_LOG_G[h] * absd) for h in range(HEADS)]
    qdec = [jnp.exp(RET_LOG_G[h] * (pcol + 1.0)) * (HD ** -0.5) for h in range(HEADS)]
    kdec = [jnp.exp(RET_LOG_G[h] * (CHUNK - 1.0 - pcol)) for h in range(HEADS)]
    sdec = [math.exp(RET_LOG_G[h] * CHUNK) for h in range(HEADS)]

    lbl = lbl_ref[...]
    e = jnp.exp(lbl - jnp.max(lbl, axis=0, keepdims=True))
    sm = e / jnp.sum(e, axis=0, keepdims=True)
    lb_all = jnp.sum(sm[:layer_j + 1], axis=0, keepdims=True)

    nb = CHUNK // SUB
    rows128 = lax.broadcasted_iota(jnp.int32, (CHUNK, HD), 0)
    t_io = lax.broadcasted_iota(jnp.int32, (nb, SUB, HD), 1)
    lane_io = lax.broadcasted_iota(jnp.int32, (nb, SUB, CHUNK), 2)
    blk_io = lax.broadcasted_iota(jnp.int32, (nb, SUB, CHUNK), 0)

    def chunk_body(i, carry):
        if nc == 1:
            s, c = i, 0
        else:
            s, c = 0, i
        r0 = pl.multiple_of(i * CHUNK, CHUNK)
        c0 = pl.multiple_of(c * CHUNK, CHUNK)
        cos = cos_ref[pl.ds(c0, CHUNK), :]
        sin = sin_ref[pl.ds(c0, CHUNK), :]

        def zcol(g, h):
            return z_ref[pl.ds(r0, CHUNK), pl.ds(g * GRP + h * HD, HD)]

        ret_states = [sret_o[s, h] for h in range(HEADS)]
        hg_states = [hgt_ref[s, h] for h in range(HEADS)]
        new_ret, new_hg = [], []
        for h in range(HEADS):
            q = zcol(0, h)
            k = zcol(1, h)
            v = zcol(2, h).astype(BF16)
            q = q * cos + pltpu.roll(q, HD // 2, 1) * sin
            k = k * cos + pltpu.roll(k, HD // 2, 1) * sin
            qs = (q * (HD ** -0.5)).astype(BF16)
            st = ret_states[h]
            att = _dot_nt(qs, k.astype(BF16)) * intra[h]
            o = _dot(att.astype(BF16), v) + _dot((q * qdec[h]).astype(BF16), st.astype(BF16))
            new_ret.append(st * sdec[h] + _dot_tn((k * kdec[h]).astype(BF16), v))
            o = _head_norm_gate(o, rg_ref[:, pl.ds(h * HD, HD)], zcol(3, h))
            mix_ref[pl.ds(r0, CHUNK), pl.ds(h * HD, HD)] = o.astype(BF16)

        for h in range(HEADS):
            lb = lb_all[:, h * HD:(h + 1) * HD]
            hq = zcol(4, h)
            sig = jax.nn.sigmoid(zcol(5, h))
            v = zcol(6, h).astype(BF16)
            q = _silu(hq)
            k = (1.0 - lb) * (1.0 - sig)
            b = _cumsum_rows(jnp.log(lb + (1.0 - lb) * sig))
            stt = hg_states[h]
            b4 = b.reshape(nb, SUB, HD)
            q4 = q.reshape(nb, SUB, HD)
            k4 = k.reshape(nb, SUB, HD)
            diag = jnp.zeros((nb, SUB, CHUNK), F32)
            for sp in range(SUB):
                arg = jnp.where(t_io >= sp, b4 - b4[:, sp:sp + 1, :], -jnp.inf)
                r = jnp.sum(jnp.exp(arg) * q4 * k4[:, sp:sp + 1, :], axis=-1, keepdims=True)
                diag = jnp.where(lane_io == blk_io * SUB + sp, r, diag)
            diag = diag.reshape(CHUNK, CHUNK)
            parts = [diag[0:SUB]]
            for blk in range(1, nb):
                lo = blk * SUB
                rb = b[lo - 1:lo, :]
                qt = (q[lo:lo + SUB] * jnp.exp(b[lo:lo + SUB] - rb)).astype(BF16)
                kt = (k * jnp.exp(jnp.where(rows128 < lo, rb - b, -jnp.inf))).astype(BF16)
                parts.append(diag[lo:lo + SUB] + _dot_nt(qt, kt))
            att = jnp.concatenate(parts, axis=0)
            o = _dot(att.astype(BF16), v) + _dot_nt((q * jnp.exp(b)).astype(BF16),
                                                    stt.astype(BF16))
            bend = b[CHUNK - 1:CHUNK, :]
            kd = (k * jnp.exp(bend - b)).astype(BF16)
            new_hg.append(stt * jnp.exp(bend) + _dot_tn(v, kd))
            o = _head_norm_gate(o, hg_ref[:, pl.ds(h * HD, HD)], zcol(7, h))
            mix_ref[pl.ds(r0, CHUNK), pl.ds(GRP + h * HD, HD)] = o.astype(BF16)
        for h in range(HEADS):
            sret_o[s, h] = new_ret[h]
            hgt_ref[s, h] = new_hg[h]
        return carry

    lax.fori_loop(0, ns * nc, chunk_body, 0)

    xo_ref[...] = (x + _dot(mix_ref[...], wout_ref[...])).reshape(xo_ref.shape)

    @pl.when(t == pl.num_programs(1) - 1)
    def _():
        for s in range(ns):
            for h in range(HEADS):
                shg_o[s, h] = hgt_ref[s, h].T


def _seq_tiling(b, t):
    if t >= SEQ_TILE:
        return 1, SEQ_TILE // CHUNK
    return min(b, SEQ_TILE // t), t // CHUNK


def _even_mixer(x, cos, sin, norm_g, w_in, w_out, ret_g, hg_g, lb_logits, s_ret, s_hg, layer_j):
    b, t, _ = x.shape
    ns, nc = _seq_tiling(b, t)
    tt = ns * nc * CHUNK
    grid = (b // ns, t // (nc * CHUNK))
    st_spec = pl.BlockSpec((ns, HEADS, HD, HD), lambda i, j: (i, 0, 0, 0))
    x_spec = pl.BlockSpec((ns, nc * CHUNK, D_MODEL), lambda i, j: (i, j, 0))

    def full(a):
        return pl.BlockSpec(a.shape, lambda i, j: (0,) * a.ndim)

    tab_spec = pl.BlockSpec((nc * CHUNK, HD), lambda i, j: (j, 0))
    args = (x, cos, sin, norm_g.reshape(1, D_MODEL), w_in, w_out, ret_g.reshape(1, GRP),
            hg_g.reshape(1, GRP), lb_logits, s_ret, s_hg)
    in_specs = [x_spec, tab_spec, tab_spec] + [full(a) for a in args[3:9]] + [st_spec, st_spec]
    return pl.pallas_call(
        functools.partial(_even_kernel, ns=ns, nc=nc, layer_j=layer_j),
        grid=grid,
        in_specs=in_specs,
        out_specs=[x_spec, st_spec, st_spec],
        out_shape=[jax.ShapeDtypeStruct(x.shape, F32),
                   jax.ShapeDtypeStruct(s_ret.shape, F32),
                   jax.ShapeDtypeStruct(s_hg.shape, F32)],
        scratch_shapes=[pltpu.VMEM((tt, EVEN_IN), F32), pltpu.VMEM((tt, D_MODEL), BF16),
                        pltpu.VMEM((ns, HEADS, HD, HD), F32)],
        compiler_params=pltpu.CompilerParams(
            dimension_semantics=("parallel", "arbitrary"), vmem_limit_bytes=VMEM_LIMIT),
        name="even_mixer",
    )(*args)


def _softplus(x):
    return jnp.maximum(x, 0.0) + jnp.log1p(jnp.exp(-jnp.abs(x)))


def _gelu_tanh(x):
    return 0.5 * x * (1.0 + jnp.tanh(math.sqrt(2.0 / math.pi) * (x + 0.044715 * (x * x * x))))


def _odd_kernel(x_ref, ng_ref, win_ref, wout_ref, cw_ref, lcb_ref, wa_ref, ba_ref, wx_ref,
                bx_ref, lam_ref, dnp_ref, dg_ref, lh_ref, lc_ref, ds_ref, dc_ref,
                xo_ref, lh_o, lc_o, ds_o, dc_o, z_ref, mix_ref, xp_ref, *, ns, nc):
    t = pl.program_id(1)
    tt = ns * nc * CHUNK
    conv_ch = LRU_WIDTH + 3 * GRP
    hist = CONV_W - 1
    top = 8

    @pl.when(t == 0)
    def _():
        lh_o[...] = lh_ref[...]
        lc_o[...] = lc_ref[...]
        ds_o[...] = ds_ref[...]
        dc_o[...] = dc_ref[...]

    x = x_ref[...].reshape(tt, D_MODEL)
    hn = _rms(x, ng_ref[...]).astype(BF16)
    z_ref[...] = _dot(hn, win_ref[...])

    sp_lam = _softplus(-lam_ref[...])
    dt_bias = dnp_ref[0:1, :]
    neg_a = -jnp.exp(dnp_ref[1:2, :])
    ri = lax.broadcasted_iota(jnp.int32, (CHUNK, CHUNK), 0)
    ci = lax.broadcasted_iota(jnp.int32, (CHUNK, CHUNK), 1)
    rows_w = lax.broadcasted_iota(jnp.int32, (CHUNK, LRU_WIDTH), 0)
    eye_c = (ri == ci).astype(F32)
    sub_bits = SUB.bit_length() - 1
    same_blk = (ri >> sub_bits) == (ci >> sub_bits)
    merge_masks = []
    for lvl in range(sub_bits, CHUNK.bit_length() - 1):
        merge_masks.append(((ri >> (lvl + 1)) == (ci >> (lvl + 1)))
                           & (((ri >> lvl) & 1) == 1) & (((ci >> lvl) & 1) == 0))

    def chunk_body(i, carry):
        s = i if nc == 1 else 0
        r0 = pl.multiple_of(i * CHUNK, CHUNK)

        xp_ref[top - hist:top, 0:LRU_WIDTH] = lc_o[s]
        xp_ref[top - hist:top, LRU_WIDTH:conv_ch] = dc_o[s]
        xp_ref[top:top + CHUNK, 0:LRU_WIDTH] = z_ref[pl.ds(r0, CHUNK), 0:LRU_WIDTH]
        xp_ref[top:top + CHUNK, LRU_WIDTH:conv_ch] = z_ref[pl.ds(r0, CHUNK), 2 * GRP:5 * GRP]
        lc_o[s] = xp_ref[top + CHUNK - hist:top + CHUNK, 0:LRU_WIDTH]
        dc_o[s] = xp_ref[top + CHUNK - hist:top + CHUNK, LRU_WIDTH:conv_ch]
        y = xp_ref[top - hist:top - hist + CHUNK, :] * cw_ref[0:1, :]
        for jj in range(1, CONV_W):
            y = y + xp_ref[top - hist + jj:top - hist + jj + CHUNK, :] * cw_ref[jj:jj + 1, :]

        lx = y[:, 0:LRU_WIDTH] + lcb_ref[...]
        xb = lx.astype(BF16)
        r = jax.nn.sigmoid(_dot(xb, wa_ref[...]) + ba_ref[...])
        ig = jax.nn.sigmoid(_dot(xb, wx_ref[...]) + bx_ref[...])
        a = jnp.exp(-LRU_C * r * sp_lam)
        u = jnp.sqrt(1.0 - a * a) * (ig * lx)
        k = 1
        while k < CHUNK:
            m = rows_w >= k
            u = jnp.where(m, a * pltpu.roll(u, k, 0) + u, u)
            a = jnp.where(m, a * pltpu.roll(a, k, 0), a)
            k *= 2
        hseq = a * lh_o[s] + u
        lh_o[s] = hseq[CHUNK - 1:CHUNK]
        lg = z_ref[pl.ds(r0, CHUNK), GRP:2 * GRP]
        mix_ref[pl.ds(r0, CHUNK), 0:LRU_WIDTH] = (_gelu_tanh(lg) * hseq).astype(BF16)

        zs = z_ref[pl.ds(r0, CHUNK), 6 * GRP:6 * GRP + HD]
        beta_t = jax.nn.sigmoid(zs)
        g_t = _cumsum_rows(neg_a * _softplus(zs + dt_bias))
        g_tt = g_t.T
        heads = range(HEADS)
        states = [ds_o[s, h] for h in heads]
        kn, both_k, both_s = [], [], []
        for h in heads:
            base = LRU_WIDTH + h * HD
            q = _silu(y[:, base:base + HD])
            kk_ = _silu(y[:, base + GRP:base + GRP + HD])
            q = q * lax.rsqrt(jnp.sum(q * q, axis=-1, keepdims=True) + EPS) * (HD ** -0.5)
            kn.append(kk_ * lax.rsqrt(jnp.sum(kk_ * kk_, axis=-1, keepdims=True) + EPS))
            knb = kn[h].astype(BF16)
            qk_in = jnp.concatenate([q.astype(BF16), knb], axis=0)
            both_k.append(_dot_nt(qk_in, knb))
            both_s.append(_dot(qk_in, states[h].astype(BF16)))
        gcol, eg, e_incl, amat, rhs = [], [], [], [], []
        for h in heads:
            v = _silu(y[:, LRU_WIDTH + 2 * GRP + h * HD:LRU_WIDTH + 2 * GRP + (h + 1) * HD])
            gcol.append(g_t[:, HEADS + h:HEADS + h + 1])
            rel = gcol[h] - g_tt[HEADS + h:HEADS + h + 1, :]
            e_strict = jnp.exp(jnp.where(ri > ci, rel, -jnp.inf))
            e_incl.append(jnp.where(ri == ci, 1.0, e_strict))
            beta = beta_t[:, h:h + 1]
            eg.append(jnp.exp(gcol[h]))
            amat.append(beta * both_k[h][CHUNK:] * e_strict)
            rhs.append(beta * (v - eg[h] * both_s[h][CHUNK:]))
        dblk = [jnp.where(same_blk, amat[h], 0.0) for h in heads]
        tinv = [eye_c for _ in heads]
        for jj in range(SUB - 1):
            for h in heads:
                colv = jnp.concatenate(
                    [dblk[h][b0:b0 + SUB, b0 + jj:b0 + jj + 1] for b0 in range(0, CHUNK, SUB)],
                    axis=0)
                rowm = jnp.concatenate(
                    [jnp.broadcast_to(tinv[h][b0 + jj:b0 + jj + 1, :], (SUB, CHUNK))
                     for b0 in range(0, CHUNK, SUB)], axis=0)
                tinv[h] = tinv[h] - colv * rowm
        for lm in merge_masks:
            tb = [tinv[h].astype(BF16) for h in heads]
            left = [_dot(tb[h], jnp.where(lm, amat[h], 0.0).astype(BF16)) for h in heads]
            tinv = [tinv[h] - _dot(left[h].astype(BF16), tb[h]) for h in heads]
        w = [_dot(tinv[h].astype(BF16), rhs[h].astype(BF16)).astype(BF16) for h in heads]
        for h in heads:
            o = eg[h] * both_s[h][:CHUNK] + _dot((both_k[h][:CHUNK] * e_incl[h]).astype(BF16), w[h])
            gend = gcol[h][CHUNK - 1:CHUNK, :]
            kd = (kn[h] * jnp.exp(gend - gcol[h])).astype(BF16)
            ds_o[s, h] = states[h] * jnp.exp(gend) + _dot_tn(kd, w[h])
            o = _head_norm_gate(o, dg_ref[:, pl.ds(h * HD, HD)],
                                z_ref[pl.ds(r0, CHUNK), pl.ds(5 * GRP + h * HD, HD)])
            mix_ref[pl.ds(r0, CHUNK), pl.ds(LRU_WIDTH + h * HD, HD)] = o.astype(BF16)
        return carry

    lax.fori_loop(0, ns * nc, chunk_body, 0)

    xo_ref[...] = (x + _dot(mix_ref[...], wout_ref[...])).reshape(xo_ref.shape)


def _odd_mixer(x, norm_g, w_in, w_out, cw, lcb, wa, ba, wx, bx, lam, dnp, dn_g,
               s_lh, s_lc, s_dn, s_dc):
    b, t, _ = x.shape
    ns, nc = _seq_tiling(b, t)
    tt = ns * nc * CHUNK
    grid = (b // ns, t // (nc * CHUNK))
    x_spec = pl.BlockSpec((ns, nc * CHUNK, D_MODEL), lambda i, j: (i, j, 0))

    def full(a):
        return pl.BlockSpec(a.shape, lambda i, j: (0,) * a.ndim)

    def st_spec(a):
        return pl.BlockSpec((ns,) + a.shape[1:], lambda i, j: (i,) + (0,) * (a.ndim - 1))

    params = (norm_g.reshape(1, D_MODEL), w_in, w_out, cw, lcb, wa, ba, wx, bx, lam, dnp, dn_g)
    states = (s_lh, s_lc, s_dn, s_dc)
    return pl.pallas_call(
        functools.partial(_odd_kernel, ns=ns, nc=nc),
        grid=grid,
        in_specs=[x_spec] + [full(a) for a in params] + [st_spec(a) for a in states],
        out_specs=[x_spec] + [st_spec(a) for a in states],
        out_shape=[jax.ShapeDtypeStruct(x.shape, F32)]
        + [jax.ShapeDtypeStruct(a.shape, F32) for a in states],
        scratch_shapes=[pltpu.VMEM((tt, ODD_IN_PAD), F32), pltpu.VMEM((tt, D_MODEL), BF16),
                        pltpu.VMEM((8 + CHUNK, LRU_WIDTH + 3 * GRP), F32)],
        compiler_params=pltpu.CompilerParams(
            dimension_semantics=("parallel", "arbitrary"), vmem_limit_bytes=VMEM_LIMIT),
        name="odd_mixer",
    )(x, *params, *states)


def _rope_tables(pos):
    half = HD // 2
    freq = ROPE_BASE ** (-jnp.arange(half, dtype=F32) / half)
    ang = pos.astype(F32)[:, None] * freq[None, :]
    cos, sin = jnp.cos(ang), jnp.sin(ang)
    return jnp.concatenate([cos, cos], axis=-1), jnp.concatenate([-sin, sin], axis=-1)


def _block_diag(w):
    n, bs, _ = w.shape
    eye = jnp.eye(n, dtype=w.dtype)
    return (eye[:, None, :, None] * w[:, :, None, :]).reshape(n * bs, n * bs)


def kernel(x_prompt, x_sample, state_ret, state_hgrn, state_lru_h, state_lru_conv, state_dn,
           state_dn_conv, ffn1_norm, ffn1_w_in, ffn1_w_out, mix_norm, ffn2_norm, ffn2_w_in,
           ffn2_w_out, final_norm, even_w_in, even_w_out, ret_out_norm, hg_out_norm,
           hg_lb_logits, odd_w_in, odd_w_out, lru_conv_w, lru_conv_b, lru_w_a, lru_b_a, lru_w_x,
           lru_b_x, lru_lambda, dn_conv_w, dn_a_log, dn_dt_bias, dn_out_norm):
    depth = ffn1_norm.shape[0]
    bp, tp, _ = x_prompt.shape
    bs, ts, _ = x_sample.shape
    past_len = 2048
    tabs = (_rope_tables(jnp.arange(tp)), _rope_tables(past_len + jnp.arange(ts)))
    xs = [x_prompt, x_sample]
    nb = (bp, bs)
    w1_in, w1_out = ffn1_w_in.astype(BF16), ffn1_w_out.astype(BF16)
    w2_in, w2_out = ffn2_w_in.astype(BF16), ffn2_w_out.astype(BF16)
    outs = {k: ([], []) for k in ("ret", "hg", "lh", "lc", "dn", "dc")}

    for l in range(depth):
        j = l // 2
        last = l == depth - 1
        for g in range(2):
            xs[g] = _ffn(xs[g], ffn1_norm, w1_in, w1_out, l, final_norm, False)
        if l % 2 == 0:
            w_in = even_w_in[j].astype(BF16)
            w_out = even_w_out[j].astype(BF16)
            for g in range(2):
                if g == 0:
                    s_ret = jnp.zeros((bp, HEADS, HD, HD), F32)
                    s_hg = jnp.zeros((bp, HEADS, HD, HD), F32)
                else:
                    s_ret, s_hg = state_ret[j], state_hgrn[j]
                xs[g], n_ret, n_hg = _even_mixer(
                    xs[g], tabs[g][0], tabs[g][1], mix_norm[l], w_in, w_out, ret_out_norm[j],
                    hg_out_norm[j], hg_lb_logits, s_ret, s_hg, j)
                outs["ret"][g].append(n_ret)
                outs["hg"][g].append(n_hg)
        else:
            w_in = jnp.pad(odd_w_in[j], ((0, 0), (0, ODD_IN_PAD - odd_w_in.shape[-1]))).astype(BF16)
            w_out = odd_w_out[j].astype(BF16)
            cw = jnp.concatenate([lru_conv_w[j], dn_conv_w[j]], axis=-1)
            wa = _block_diag(lru_w_a[j]).astype(BF16)
            wx = _block_diag(lru_w_x[j]).astype(BF16)
            dnp = jnp.zeros((2, HD), F32)
            dnp = dnp.at[0, HEADS:2 * HEADS].set(dn_dt_bias[j])
            dnp = dnp.at[1, HEADS:2 * HEADS].set(dn_a_log[j])
            for g in range(2):
                if g == 0:
                    s_lh = jnp.zeros((bp, 1, LRU_WIDTH), F32)
                    s_lc = jnp.zeros((bp, CONV_W - 1, LRU_WIDTH), F32)
                    s_dn = jnp.zeros((bp, HEADS, HD, HD), F32)
                    s_dc = jnp.zeros((bp, CONV_W - 1, 3 * GRP), F32)
                else:
                    s_lh = state_lru_h[j].reshape(bs, 1, LRU_WIDTH)
                    s_lc, s_dn, s_dc = state_lru_conv[j], state_dn[j], state_dn_conv[j]
                xs[g], n_lh, n_lc, n_dn, n_dc = _odd_mixer(
                    xs[g], mix_norm[l], w_in, w_out, cw, lru_conv_b[j].reshape(1, -1), wa,
                    lru_b_a[j].reshape(1, -1), wx, lru_b_x[j].reshape(1, -1),
                    lru_lambda[j].reshape(1, -1), dnp, dn_out_norm[j].reshape(1, -1),
                    s_lh, s_lc, s_dn, s_dc)
                outs["lh"][g].append(n_lh.reshape(nb[g], LRU_WIDTH))
                outs["lc"][g].append(n_lc)
                outs["dn"][g].append(n_dn)
                outs["dc"][g].append(n_dc)
        for g in range(2):
            xs[g] = _ffn(xs[g], ffn2_norm, w2_in, w2_out, l, final_norm, last)

    res = [xs[0], xs[1]]
    for k in ("ret", "hg", "lh", "lc", "dn", "dc"):
        for g in range(2):
            res.append(jnp.stack(outs[k][g]))
    return tuple(res)
```

```python
import functools
import math

import jax
import jax.numpy as jnp
from jax import lax
from jax.experimental import pallas as pl
from jax.experimental.pallas import tpu as pltpu

F32 = jnp.float32
BF16 = jnp.bfloat16

D_MODEL = 1024
FF_DIM = 2816
EPS = 1e-6
CHUNK = 64
CONV_W = 4
HEADS = 4
HD = 128
ROPE_BASE = 10000.0
LRU_WIDTH = 512
LRU_C = 8.0
GRP = HEADS * HD
EVEN_IN = 8 * GRP
ODD_IN_PAD = 6 * GRP + HD
SUB = 8
DN_SUB = 16

VMEM_LIMIT = 56 * 1024 * 1024

FFN_TM = 512
FFN_TF = 256
SEQ_TILE = 512

RET_LOG_G = [math.log1p(-(2.0 ** (-5.0 - h))) for h in range(HEADS)]


def _rms(x, g):
    return x * lax.rsqrt(jnp.mean(x * x, axis=-1, keepdims=True) + EPS) * g


def _silu(x):
    return x * jax.nn.sigmoid(x)


def _dot(a, b):
    return jnp.dot(a, b, preferred_element_type=F32)


def _dot_nt(a, b):
    return lax.dot_general(a, b, (((1,), (1,)), ((), ())), preferred_element_type=F32)


def _dot_tn(a, b):
    return lax.dot_general(a, b, (((0,), (0,)), ((), ())), preferred_element_type=F32)


def _head_norm_gate(o, g, gate):
    o = o * lax.rsqrt(jnp.mean(o * o, axis=-1, keepdims=True) + EPS) * g
    return o * _silu(gate)


def _cumsum_rows(x):
    rows = lax.broadcasted_iota(jnp.int32, x.shape, 0)
    k = 1
    while k < x.shape[0]:
        x = jnp.where(rows >= k, x + pltpu.roll(x, k, 0), x)
        k *= 2
    return x


def _ffn_kernel(x_ref, g_ref, win_ref, wout_ref, fin_ref, o_ref, act_ref, *, apply_final):
    x = x_ref[...]
    h = _rms(x, g_ref[...]).astype(BF16)
    for c in range(FF_DIM // FFN_TF):
        lo = c * FFN_TF
        gate = _dot(h, win_ref[:, lo:lo + FFN_TF])
        up = _dot(h, win_ref[:, FF_DIM + lo:FF_DIM + lo + FFN_TF])
        act_ref[:, lo:lo + FFN_TF] = (_silu(gate) * up).astype(BF16)
    y = x + 0.5 * _dot(act_ref[...], wout_ref[...])
    if apply_final:
        y = _rms(y, fin_ref[...])
    o_ref[...] = y


def _resident(block_shape, index_map):
    return pl.BlockSpec(block_shape, index_map, pipeline_mode=pl.Buffered(1))


def _ffn(x, norm_g, w_in, w_out, layer, final_g, apply_final):
    shape = x.shape
    x2 = x.reshape(-1, D_MODEL)
    m = x2.shape[0]
    tm = min(FFN_TM, m)
    out = pl.pallas_call(
        functools.partial(_ffn_kernel, apply_final=apply_final),
        grid=(m // tm,),
        in_specs=[
            pl.BlockSpec((tm, D_MODEL), lambda i: (i, 0)),
            _resident((None, 1, D_MODEL), lambda i: (layer, 0, 0)),
            _resident((None, D_MODEL, 2 * FF_DIM), lambda i: (layer, 0, 0)),
            _resident((None, FF_DIM, D_MODEL), lambda i: (layer, 0, 0)),
            _resident((1, D_MODEL), lambda i: (0, 0)),
        ],
        out_specs=pl.BlockSpec((tm, D_MODEL), lambda i: (i, 0)),
        out_shape=jax.ShapeDtypeStruct((m, D_MODEL), F32),
        scratch_shapes=[pltpu.VMEM((tm, FF_DIM), BF16)],
        compiler_params=pltpu.CompilerParams(
            dimension_semantics=("parallel",), vmem_limit_bytes=VMEM_LIMIT),
        name="ffn",
    )(x2, norm_g.reshape(-1, 1, D_MODEL), w_in, w_out, final_g.reshape(1, D_MODEL))
    return out.reshape(shape)


def _even_kernel(x_ref, cos_ref, sin_ref, ng_ref, win_ref, wout_ref, rg_ref, hg_ref, lbl_ref,
                 sret_ref, shg_ref, xo_ref, sret_o, shg_o, z_ref, mix_ref, hgt_ref, *,
                 ns, nc, layer_j):
    t = pl.program_id(1)
    tt = ns * nc * CHUNK

    @pl.when(t == 0)
    def _():
        sret_o[...] = sret_ref[...]
        for s in range(ns):
            for h in range(HEADS):
                hgt_ref[s, h] = shg_ref[s, h].T

    x = x_ref[...].reshape(tt, D_MODEL)
    hn = _rms(x, ng_ref[...]).astype(BF16)
    z_ref[...] = _dot(hn, win_ref[...])

    ri = lax.broadcasted_iota(jnp.int32, (CHUNK, CHUNK), 0)
    ci = lax.broadcasted_iota(jnp.int32, (CHUNK, CHUNK), 1)
    absd = jnp.abs(ri - ci).astype(F32)
    pcol = lax.broadcasted_iota(jnp.int32, (CHUNK, 1), 0).astype(F32)
    intra = [jnp.exp(RET_LOG_G[h] * absd) for h in range(HEADS)]
    qdec = [jnp.exp(RET_LOG_G[h] * (pcol + 1.0)) * (HD ** -0.5) for h in range(HEADS)]
    kdec = [jnp.exp(RET_LOG_G[h] * (CHUNK - 1.0 - pcol)) for h in range(HEADS)]
    sdec = [math.exp(RET_LOG_G[h] * CHUNK) for h in range(HEADS)]

    lbl = lbl_ref[...]
    e = jnp.exp(lbl - jnp.max(lbl, axis=0, keepdims=True))
    sm = e / jnp.sum(e, axis=0, keepdims=True)
    lb_all = jnp.sum(sm[:layer_j + 1], axis=0, keepdims=True)

    nb = CHUNK // SUB
    t_io = lax.broadcasted_iota(jnp.int32, (nb, SUB, HD), 1)
    lane_io = lax.broadcasted_iota(jnp.int32, (nb, SUB, CHUNK), 2)
    blk_io = lax.broadcasted_iota(jnp.int32, (nb, SUB, CHUNK), 0)

    def chunk_body(i, carry):
        if nc == 1:
            s, c = i, 0
        else:
            s, c = 0, i
        r0 = pl.multiple_of(i * CHUNK, CHUNK)
        c0 = pl.multiple_of(c * CHUNK, CHUNK)
        cos = cos_ref[pl.ds(c0, CHUNK), :]
        sin = sin_ref[pl.ds(c0, CHUNK), :]

        def zcol(g, h):
            return z_ref[pl.ds(r0, CHUNK), pl.ds(g * GRP + h * HD, HD)]

        heads = range(HEADS)
        ret_states = [sret_o[s, h] for h in heads]
        hg_states = [hgt_ref[s, h] for h in heads]

        r_v, r_att, r_inter = [], [], []
        for h in heads:
            q = zcol(0, h)
            k = zcol(1, h)
            r_v.append(zcol(2, h).astype(BF16))
            q = q * cos + pltpu.roll(q, HD // 2, 1) * sin
            k = k * cos + pltpu.roll(k, HD // 2, 1) * sin
            qs = (q * (HD ** -0.5)).astype(BF16)
            st = ret_states[h]
            r_att.append(_dot_nt(qs, k.astype(BF16)))
            r_inter.append(_dot((q * qdec[h]).astype(BF16), st.astype(BF16)))
            sret_o[s, h] = st * sdec[h] + _dot_tn((k * kdec[h]).astype(BF16), r_v[h])

        h_q, h_k, h_b, h_v, h_off, h_inter = [], [], [], [], [], []
        for h in heads:
            lb = lb_all[:, h * HD:(h + 1) * HD]
            sig = jax.nn.sigmoid(zcol(5, h))
            v = zcol(6, h).astype(BF16)
            q = _silu(zcol(4, h))
            k = (1.0 - lb) * (1.0 - sig)
            b = _cumsum_rows(jnp.log(lb + (1.0 - lb) * sig))
            stt = hg_states[h]
            off = []
            for blk in range(1, nb):
                lo = blk * SUB
                rb = b[lo - 1:lo, :]
                qt = (q[lo:lo + SUB] * jnp.exp(b[lo:lo + SUB] - rb)).astype(BF16)
                kt = jnp.concatenate([k[:lo] * jnp.exp(rb - b[:lo]),
                                      jnp.zeros((CHUNK - lo, HD), F32)], axis=0).astype(BF16)
                off.append(_dot_nt(qt, kt))
            h_off.append(off)
            h_inter.append(_dot_nt((q * jnp.exp(b)).astype(BF16), stt.astype(BF16)))
            bend = b[CHUNK - 1:CHUNK, :]
            kd = (k * jnp.exp(bend - b)).astype(BF16)
            hgt_ref[s, h] = stt * jnp.exp(bend) + _dot_tn(v, kd)
            h_q.append(q)
            h_k.append(k)
            h_b.append(b)
            h_v.append(v)

        for h in heads:
            o = _dot((r_att[h] * intra[h]).astype(BF16), r_v[h]) + r_inter[h]
            o = _head_norm_gate(o, rg_ref[:, pl.ds(h * HD, HD)], zcol(3, h))
            mix_ref[pl.ds(r0, CHUNK), pl.ds(h * HD, HD)] = o.astype(BF16)

        h_att = []
        for h in heads:
            b4 = h_b[h].reshape(nb, SUB, HD)
            q4 = h_q[h].reshape(nb, SUB, HD)
            k4 = h_k[h].reshape(nb, SUB, HD)
            diag = jnp.zeros((nb, SUB, CHUNK), F32)
            for sp in range(SUB):
                arg = jnp.where(t_io >= sp, b4 - b4[:, sp:sp + 1, :], -jnp.inf)
                r = jnp.sum(jnp.exp(arg) * q4 * k4[:, sp:sp + 1, :], axis=-1, keepdims=True)
                diag = jnp.where(lane_io == blk_io * SUB + sp, r, diag)
            diag = diag.reshape(CHUNK, CHUNK)
            parts = [diag[0:SUB]]
            for blk in range(1, nb):
                parts.append(diag[blk * SUB:(blk + 1) * SUB] + h_off[h][blk - 1])
            h_att.append(jnp.concatenate(parts, axis=0).astype(BF16))
        for h in heads:
            o = _dot(h_att[h], h_v[h]) + h_inter[h]
            o = _head_norm_gate(o, hg_ref[:, pl.ds(h * HD, HD)], zcol(7, h))
            mix_ref[pl.ds(r0, CHUNK), pl.ds(GRP + h * HD, HD)] = o.astype(BF16)
        return carry

    lax.fori_loop(0, ns * nc, chunk_body, 0)

    xo_ref[...] = (x + _dot(mix_ref[...], wout_ref[...])).reshape(xo_ref.shape)

    @pl.when(t == pl.num_programs(1) - 1)
    def _():
        for s in range(ns):
            for h in range(HEADS):
                shg_o[s, h] = hgt_ref[s, h].T


def _seq_tiling(b, t):
    if t >= SEQ_TILE:
        return 1, SEQ_TILE // CHUNK
    return min(b, SEQ_TILE // t), t // CHUNK


def _even_mixer(x, cos, sin, norm_g, w_in, w_out, ret_g, hg_g, lb_logits, s_ret, s_hg, layer_j):
    b, t, _ = x.shape
    ns, nc = _seq_tiling(b, t)
    tt = ns * nc * CHUNK
    grid = (b // ns, t // (nc * CHUNK))
    st_spec = pl.BlockSpec((ns, HEADS, HD, HD), lambda i, j: (i, 0, 0, 0))
    x_spec = pl.BlockSpec((ns, nc * CHUNK, D_MODEL), lambda i, j: (i, j, 0))

    def full(a):
        return pl.BlockSpec(a.shape, lambda i, j: (0,) * a.ndim)

    tab_spec = pl.BlockSpec((nc * CHUNK, HD), lambda i, j: (j, 0))
    args = (x, cos, sin, norm_g.reshape(1, D_MODEL), w_in, w_out, ret_g.reshape(1, GRP),
            hg_g.reshape(1, GRP), lb_logits, s_ret, s_hg)
    in_specs = [x_spec, tab_spec, tab_spec] + [full(a) for a in args[3:9]] + [st_spec, st_spec]
    return pl.pallas_call(
        functools.partial(_even_kernel, ns=ns, nc=nc, layer_j=layer_j),
        grid=grid,
        in_specs=in_specs,
        out_specs=[x_spec, st_spec, st_spec],
        out_shape=[jax.ShapeDtypeStruct(x.shape, F32),
                   jax.ShapeDtypeStruct(s_ret.shape, F32),
                   jax.ShapeDtypeStruct(s_hg.shape, F32)],
        scratch_shapes=[pltpu.VMEM((tt, EVEN_IN), F32), pltpu.VMEM((tt, D_MODEL), BF16),
                        pltpu.VMEM((ns, HEADS, HD, HD), F32)],
        compiler_params=pltpu.CompilerParams(
            dimension_semantics=("parallel", "arbitrary"), vmem_limit_bytes=VMEM_LIMIT),
        name="even_mixer",
    )(*args)


def _softplus(x):
    return jnp.maximum(x, 0.0) + jnp.log1p(jnp.exp(-jnp.abs(x)))


def _gelu_tanh(x):
    return 0.5 * x * (1.0 + jnp.tanh(math.sqrt(2.0 / math.pi) * (x + 0.044715 * (x * x * x))))


def _odd_kernel(x_ref, ng_ref, win_ref, wout_ref, cw_ref, lcb_ref, wa_ref, ba_ref, wx_ref,
                bx_ref, lam_ref, dnp_ref, dg_ref, lh_ref, lc_ref, ds_ref, dc_ref,
                xo_ref, lh_o, lc_o, ds_o, dc_o, z_ref, mix_ref, xp_ref, *, ns, nc):
    t = pl.program_id(1)
    tt = ns * nc * CHUNK
    conv_ch = LRU_WIDTH + 3 * GRP
    hist = CONV_W - 1
    top = 8

    @pl.when(t == 0)
    def _():
        lh_o[...] = lh_ref[...]
        lc_o[...] = lc_ref[...]
        ds_o[...] = ds_ref[...]
        dc_o[...] = dc_ref[...]

    x = x_ref[...].reshape(tt, D_MODEL)
    hn = _rms(x, ng_ref[...]).astype(BF16)
    z_ref[...] = _dot(hn, win_ref[...])

    sp_lam = _softplus(-lam_ref[...])
    dt_bias = dnp_ref[0:1, :]
    neg_a = -jnp.exp(dnp_ref[1:2, :])
    ri = lax.broadcasted_iota(jnp.int32, (CHUNK, CHUNK), 0)
    ci = lax.broadcasted_iota(jnp.int32, (CHUNK, CHUNK), 1)
    rows_w = lax.broadcasted_iota(jnp.int32, (CHUNK, LRU_WIDTH), 0)
    eye_c = (ri == ci).astype(F32)
    sub_bits = DN_SUB.bit_length() - 1
    same_blk = (ri >> sub_bits) == (ci >> sub_bits)
    merge_masks = []
    for lvl in range(sub_bits, CHUNK.bit_length() - 1):
        merge_masks.append(((ri >> (lvl + 1)) == (ci >> (lvl + 1)))
                           & (((ri >> lvl) & 1) == 1) & (((ci >> lvl) & 1) == 0))

    def chunk_body(i, carry):
        s = i if nc == 1 else 0
        r0 = pl.multiple_of(i * CHUNK, CHUNK)

        xp_ref[top - hist:top, 0:LRU_WIDTH] = lc_o[s]
        xp_ref[top - hist:top, LRU_WIDTH:conv_ch] = dc_o[s]
        xp_ref[top:top + CHUNK, 0:LRU_WIDTH] = z_ref[pl.ds(r0, CHUNK), 0:LRU_WIDTH]
        xp_ref[top:top + CHUNK, LRU_WIDTH:conv_ch] = z_ref[pl.ds(r0, CHUNK), 2 * GRP:5 * GRP]
        lc_o[s] = xp_ref[top + CHUNK - hist:top + CHUNK, 0:LRU_WIDTH]
        dc_o[s] = xp_ref[top + CHUNK - hist:top + CHUNK, LRU_WIDTH:conv_ch]
        y = xp_ref[top - hist:top - hist + CHUNK, :] * cw_ref[0:1, :]
        for jj in range(1, CONV_W):
            y = y + xp_ref[top - hist + jj:top - hist + jj + CHUNK, :] * cw_ref[jj:jj + 1, :]

        def rg_lru_branch():
            lx = y[:, 0:LRU_WIDTH] + lcb_ref[...]
            xb = lx.astype(BF16)
            r = jax.nn.sigmoid(_dot(xb, wa_ref[...]) + ba_ref[...])
            ig = jax.nn.sigmoid(_dot(xb, wx_ref[...]) + bx_ref[...])
            a = jnp.exp(-LRU_C * r * sp_lam)
            u = jnp.sqrt(1.0 - a * a) * (ig * lx)
            k = 1
            while k < CHUNK:
                m = rows_w >= k
                u = jnp.where(m, a * pltpu.roll(u, k, 0) + u, u)
                a = jnp.where(m, a * pltpu.roll(a, k, 0), a)
                k *= 2
            hseq = a * lh_o[s] + u
            lh_o[s] = hseq[CHUNK - 1:CHUNK]
            lg = z_ref[pl.ds(r0, CHUNK), GRP:2 * GRP]
            mix_ref[pl.ds(r0, CHUNK), 0:LRU_WIDTH] = (_gelu_tanh(lg) * hseq).astype(BF16)

        zs = z_ref[pl.ds(r0, CHUNK), 6 * GRP:6 * GRP + HD]
        beta_t = jax.nn.sigmoid(zs)
        g_t = _cumsum_rows(neg_a * _softplus(zs + dt_bias))
        g_tt = g_t.T
        heads = range(HEADS)
        states = [ds_o[s, h] for h in heads]
        kn, both_k, both_s = [], [], []
        for h in heads:
            base = LRU_WIDTH + h * HD
            q = _silu(y[:, base:base + HD])
            kk_ = _silu(y[:, base + GRP:base + GRP + HD])
            q = q * lax.rsqrt(jnp.sum(q * q, axis=-1, keepdims=True) + EPS) * (HD ** -0.5)
            kn.append(kk_ * lax.rsqrt(jnp.sum(kk_ * kk_, axis=-1, keepdims=True) + EPS))
            knb = kn[h].astype(BF16)
            qk_in = jnp.concatenate([q.astype(BF16), knb], axis=0)
            both_k.append(_dot_nt(qk_in, knb))
            both_s.append(_dot(qk_in, states[h].astype(BF16)))
        gcol, eg, e_incl, amat, rhs = [], [], [], [], []
        for h in heads:
            v = _silu(y[:, LRU_WIDTH + 2 * GRP + h * HD:LRU_WIDTH + 2 * GRP + (h + 1) * HD])
            gcol.append(g_t[:, HEADS + h:HEADS + h + 1])
            rel = gcol[h] - g_tt[HEADS + h:HEADS + h + 1, :]
            e_strict = jnp.exp(jnp.where(ri > ci, rel, -jnp.inf))
            e_incl.append(jnp.where(ri == ci, 1.0, e_strict))
            beta = beta_t[:, h:h + 1]
            eg.append(jnp.exp(gcol[h]))
            amat.append(beta * both_k[h][CHUNK:] * e_strict)
            rhs.append(beta * (v - eg[h] * both_s[h][CHUNK:]))
        dblk = [jnp.where(same_blk, amat[h], 0.0) for h in heads]
        tinv = [eye_c for _ in heads]
        for jj in range(DN_SUB - 1):
            for h in heads:
                colv = jnp.concatenate(
                    [dblk[h][b0:b0 + DN_SUB, b0 + jj:b0 + jj + 1]
                     for b0 in range(0, CHUNK, DN_SUB)], axis=0)
                rowm = jnp.concatenate(
                    [jnp.broadcast_to(tinv[h][b0 + jj:b0 + jj + 1, :], (DN_SUB, CHUNK))
                     for b0 in range(0, CHUNK, DN_SUB)], axis=0)
                tinv[h] = tinv[h] - colv * rowm
        rg_lru_branch()
        for lm in merge_masks:
            tb = [tinv[h].astype(BF16) for h in heads]
            left = [_dot(tb[h], jnp.where(lm, amat[h], 0.0).astype(BF16)) for h in heads]
            tinv = [tinv[h] - _dot(left[h].astype(BF16), tb[h]) for h in heads]
        w = [_dot(tinv[h].astype(BF16), rhs[h].astype(BF16)).astype(BF16) for h in heads]
        for h in heads:
            o = eg[h] * both_s[h][:CHUNK] + _dot((both_k[h][:CHUNK] * e_incl[h]).astype(BF16), w[h])
            gend = gcol[h][CHUNK - 1:CHUNK, :]
            kd = (kn[h] * jnp.exp(gend - gcol[h])).astype(BF16)
            ds_o[s, h] = states[h] * jnp.exp(gend) + _dot_tn(kd, w[h])
            o = _head_norm_gate(o, dg_ref[:, pl.ds(h * HD, HD)],
                                z_ref[pl.ds(r0, CHUNK), pl.ds(5 * GRP + h * HD, HD)])
            mix_ref[pl.ds(r0, CHUNK), pl.ds(LRU_WIDTH + h * HD, HD)] = o.astype(BF16)
        return carry

    lax.fori_loop(0, ns * nc, chunk_body, 0)

    xo_ref[...] = (x + _dot(mix_ref[...], wout_ref[...])).reshape(xo_ref.shape)


def _odd_mixer(x, norm_g, w_in, w_out, cw, lcb, wa, ba, wx, bx, lam, dnp, dn_g,
               s_lh, s_lc, s_dn, s_dc):
    b, t, _ = x.shape
    ns, nc = _seq_tiling(b, t)
    tt = ns * nc * CHUNK
    grid = (b // ns, t // (nc * CHUNK))
    x_spec = pl.BlockSpec((ns, nc * CHUNK, D_MODEL), lambda i, j: (i, j, 0))

    def full(a):
        return pl.BlockSpec(a.shape, lambda i, j: (0,) * a.ndim)

    def st_spec(a):
        return pl.BlockSpec((ns,) + a.shape[1:], lambda i, j: (i,) + (0,) * (a.ndim - 1))

    params = (norm_g.reshape(1, D_MODEL), w_in, w_out, cw, lcb, wa, ba, wx, bx, lam, dnp, dn_g)
    states = (s_lh, s_lc, s_dn, s_dc)
    return pl.pallas_call(
        functools.partial(_odd_kernel, ns=ns, nc=nc),
        grid=grid,
        in_specs=[x_spec] + [full(a) for a in params] + [st_spec(a) for a in states],
        out_specs=[x_spec] + [st_spec(a) for a in states],
        out_shape=[jax.ShapeDtypeStruct(x.shape, F32)]
        + [jax.ShapeDtypeStruct(a.shape, F32) for a in states],
        scratch_shapes=[pltpu.VMEM((tt, ODD_IN_PAD), F32), pltpu.VMEM((tt, D_MODEL), BF16),
                        pltpu.VMEM((8 + CHUNK, LRU_WIDTH + 3 * GRP), F32)],
        compiler_params=pltpu.CompilerParams(
            dimension_semantics=("parallel", "arbitrary"), vmem_limit_bytes=VMEM_LIMIT),
        name="odd_mixer",
    )(x, *params, *states)


def _rope_tables(pos):
    half = HD // 2
    freq = ROPE_BASE ** (-jnp.arange(half, dtype=F32) / half)
    ang = pos.astype(F32)[:, None] * freq[None, :]
    cos, sin = jnp.cos(ang), jnp.sin(ang)
    return jnp.concatenate([cos, cos], axis=-1), jnp.concatenate([-sin, sin], axis=-1)


def _block_diag(w):
    n, bs, _ = w.shape
    eye = jnp.eye(n, dtype=w.dtype)
    return (eye[:, None, :, None] * w[:, :, None, :]).reshape(n * bs, n * bs)


def kernel(x_prompt, x_sample, state_ret, state_hgrn, state_lru_h, state_lru_conv, state_dn,
           state_dn_conv, ffn1_norm, ffn1_w_in, ffn1_w_out, mix_norm, ffn2_norm, ffn2_w_in,
           ffn2_w_out, final_norm, even_w_in, even_w_out, ret_out_norm, hg_out_norm,
           hg_lb_logits, odd_w_in, odd_w_out, lru_conv_w, lru_conv_b, lru_w_a, lru_b_a, lru_w_x,
           lru_b_x, lru_lambda, dn_conv_w, dn_a_log, dn_dt_bias, dn_out_norm):
    depth = ffn1_norm.shape[0]
    bp, tp, _ = x_prompt.shape
    bs, ts, _ = x_sample.shape
    past_len = 2048
    tabs = (_rope_tables(jnp.arange(tp)), _rope_tables(past_len + jnp.arange(ts)))
    xs = [x_prompt, x_sample]
    nb = (bp, bs)
    w1_in, w1_out = ffn1_w_in.astype(BF16), ffn1_w_out.astype(BF16)
    w2_in, w2_out = ffn2_w_in.astype(BF16), ffn2_w_out.astype(BF16)
    outs = {k: ([], []) for k in ("ret", "hg", "lh", "lc", "dn", "dc")}

    for l in range(depth):
        j = l // 2
        last = l == depth - 1
        for g in range(2):
            xs[g] = _ffn(xs[g], ffn1_norm, w1_in, w1_out, l, final_norm, False)
        if l % 2 == 0:
            w_in = even_w_in[j].astype(BF16)
            w_out = even_w_out[j].astype(BF16)
            for g in range(2):
                if g == 0:
                    s_ret = jnp.zeros((bp, HEADS, HD, HD), F32)
                    s_hg = jnp.zeros((bp, HEADS, HD, HD), F32)
                else:
                    s_ret, s_hg = state_ret[j], state_hgrn[j]
                xs[g], n_ret, n_hg = _even_mixer(
                    xs[g], tabs[g][0], tabs[g][1], mix_norm[l], w_in, w_out, ret_out_norm[j],
                    hg_out_norm[j], hg_lb_logits, s_ret, s_hg, j)
                outs["ret"][g].append(n_ret)
                outs["hg"][g].append(n_hg)
        else:
            w_in = jnp.pad(odd_w_in[j], ((0, 0), (0, ODD_IN_PAD - odd_w_in.shape[-1]))).astype(BF16)
            w_out = odd_w_out[j].astype(BF16)
            cw = jnp.concatenate([lru_conv_w[j], dn_conv_w[j]], axis=-1)
            wa = _block_diag(lru_w_a[j]).astype(BF16)
            wx = _block_diag(lru_w_x[j]).astype(BF16)
            dnp = jnp.zeros((2, HD), F32)
            dnp = dnp.at[0, HEADS:2 * HEADS].set(dn_dt_bias[j])
            dnp = dnp.at[1, HEADS:2 * HEADS].set(dn_a_log[j])
            for g in range(2):
                if g == 0:
                    s_lh = jnp.zeros((bp, 1, LRU_WIDTH), F32)
                    s_lc = jnp.zeros((bp, CONV_W - 1, LRU_WIDTH), F32)
                    s_dn = jnp.zeros((bp, HEADS, HD, HD), F32)
                    s_dc = jnp.zeros((bp, CONV_W - 1, 3 * GRP), F32)
                else:
                    s_lh = state_lru_h[j].reshape(bs, 1, LRU_WIDTH)
                    s_lc, s_dn, s_dc = state_lru_conv[j], state_dn[j], state_dn_conv[j]
                xs[g], n_lh, n_lc, n_dn, n_dc = _odd_mixer(
                    xs[g], mix_norm[l], w_in, w_out, cw, lru_conv_b[j].reshape(1, -1), wa,
                    lru_b_a[j].reshape(1, -1), wx, lru_b_x[j].reshape(1, -1),
                    lru_lambda[j].reshape(1, -1), dnp, dn_out_norm[j].reshape(1, -1),
                    s_lh, s_lc, s_dn, s_dc)
                outs["lh"][g].append(n_lh.reshape(nb[g], LRU_WIDTH))
                outs["lc"][g].append(n_lc)
                outs["dn"][g].append(n_dn)
                outs["dc"][g].append(n_dc)
        for g in range(2):
            xs[g] = _ffn(xs[g], ffn2_norm, w2_in, w2_out, l, final_norm, last)

    res = [xs[0], xs[1]]
    for k in ("ret", "hg", "lh", "lc", "dn", "dc"):
        for g in range(2):
            res.append(jnp.stack(outs[k][g]))
    return tuple(res)
```

```python
import functools
import math

import jax
import jax.numpy as jnp
from jax import lax
from jax.experimental import pallas as pl
from jax.experimental.pallas import tpu as pltpu

F32 = jnp.float32
BF16 = jnp.bfloat16

D_MODEL = 1024
FF_DIM = 2816
EPS = 1e-6
CHUNK = 64
CONV_W = 4
HEADS = 4
HD = 128
ROPE_BASE = 10000.0
LRU_WIDTH = 512
LRU_C = 8.0
GRP = HEADS * HD
EVEN_IN = 8 * GRP
ODD_IN_PAD = 6 * GRP + HD
SUB = 8
DN_SUB = 8

VMEM_LIMIT = 56 * 1024 * 1024

FFN_TM = 512
FFN_TF = 256
SEQ_TILE = 512

RET_LOG_G = [math.log1p(-(2.0 ** (-5.0 - h))) for h in range(HEADS)]


def _rms(x, g):
    return x * lax.rsqrt(jnp.mean(x * x, axis=-1, keepdims=True) + EPS) * g


def _silu(x):
    return x * jax.nn.sigmoid(x)


def _dot(a, b):
    return jnp.dot(a, b, preferred_element_type=F32)


def _dot_nt(a, b):
    return lax.dot_general(a, b, (((1,), (1,)), ((), ())), preferred_element_type=F32)


def _dot_tn(a, b):
    return lax.dot_general(a, b, (((0,), (0,)), ((), ())), preferred_element_type=F32)


def _head_norm_gate(o, g, gate):
    o = o * lax.rsqrt(jnp.mean(o * o, axis=-1, keepdims=True) + EPS) * g
    return o * _silu(gate)


def _cumsum_rows(x):
    rows = lax.broadcasted_iota(jnp.int32, x.shape, 0)
    k = 1
    while k < x.shape[0]:
        x = jnp.where(rows >= k, x + pltpu.roll(x, k, 0), x)
        k *= 2
    return x


def _ffn_kernel(x_ref, g_ref, win_ref, wout_ref, fin_ref, o_ref, act_ref, *, apply_final):
    x = x_ref[...]
    h = _rms(x, g_ref[...]).astype(BF16)
    for c in range(FF_DIM // FFN_TF):
        lo = c * FFN_TF
        gate = _dot(h, win_ref[:, lo:lo + FFN_TF])
        up = _dot(h, win_ref[:, FF_DIM + lo:FF_DIM + lo + FFN_TF])
        act_ref[:, lo:lo + FFN_TF] = (_silu(gate) * up).astype(BF16)
    y = x + 0.5 * _dot(act_ref[...], wout_ref[...])
    if apply_final:
        y = _rms(y, fin_ref[...])
    o_ref[...] = y


def _resident(block_shape, index_map):
    return pl.BlockSpec(block_shape, index_map, pipeline_mode=pl.Buffered(1))


def _ffn(x, norm_g, w_in, w_out, layer, final_g, apply_final):
    shape = x.shape
    x2 = x.reshape(-1, D_MODEL)
    m = x2.shape[0]
    tm = min(FFN_TM, m)
    out = pl.pallas_call(
        functools.partial(_ffn_kernel, apply_final=apply_final),
        grid=(m // tm,),
        in_specs=[
            pl.BlockSpec((tm, D_MODEL), lambda i: (i, 0)),
            _resident((None, 1, D_MODEL), lambda i: (layer, 0, 0)),
            _resident((None, D_MODEL, 2 * FF_DIM), lambda i: (layer, 0, 0)),
            _resident((None, FF_DIM, D_MODEL), lambda i: (layer, 0, 0)),
            _resident((1, D_MODEL), lambda i: (0, 0)),
        ],
        out_specs=pl.BlockSpec((tm, D_MODEL), lambda i: (i, 0)),
        out_shape=jax.ShapeDtypeStruct((m, D_MODEL), F32),
        scratch_shapes=[pltpu.VMEM((tm, FF_DIM), BF16)],
        compiler_params=pltpu.CompilerParams(
            dimension_semantics=("parallel",), vmem_limit_bytes=VMEM_LIMIT),
        name="ffn",
    )(x2, norm_g.reshape(-1, 1, D_MODEL), w_in, w_out, final_g.reshape(1, D_MODEL))
    return out.reshape(shape)


def _even_kernel(x_ref, cos_ref, sin_ref, ng_ref, win_ref, wout_ref, rg_ref, hg_ref, lbl_ref,
                 sret_ref, shg_ref, xo_ref, sret_o, shg_o, z_ref, mix_ref, hgt_ref, *,
                 ns, nc, layer_j):
    t = pl.program_id(1)
    tt = ns * nc * CHUNK

    @pl.when(t == 0)
    def _():
        sret_o[...] = sret_ref[...]
        for s in range(ns):
            for h in range(HEADS):
                hgt_ref[s, h] = shg_ref[s, h].T

    x = x_ref[...].reshape(tt, D_MODEL)
    hn = _rms(x, ng_ref[...]).astype(BF16)
    z_ref[...] = _dot(hn, win_ref[...])

    ri = lax.broadcasted_iota(jnp.int32, (CHUNK, CHUNK), 0)
    ci = lax.broadcasted_iota(jnp.int32, (CHUNK, CHUNK), 1)
    absd = jnp.abs(ri - ci).astype(F32)
    pcol = lax.broadcasted_iota(jnp.int32, (CHUNK, 1), 0).astype(F32)
    intra = [jnp.exp(RET_LOG_G[h] * absd) for h in range(HEADS)]
    qdec = [jnp.exp(RET_LOG_G[h] * (pcol + 1.0)) * (HD ** -0.5) for h in range(HEADS)]
    kdec = [jnp.exp(RET_LOG_G[h] * (CHUNK - 1.0 - pcol)) for h in range(HEADS)]
    sdec = [math.exp(RET_LOG_G[h] * CHUNK) for h in range(HEADS)]

    lbl = lbl_ref[...]
    e = jnp.exp(lbl - jnp.max(lbl, axis=0, keepdims=True))
    sm = e / jnp.sum(e, axis=0, keepdims=True)
    lb_all = jnp.sum(sm[:layer_j + 1], axis=0, keepdims=True)

    nb = CHUNK // SUB
    t_io = lax.broadcasted_iota(jnp.int32, (nb, SUB, HD), 1)
    lane_io = lax.broadcasted_iota(jnp.int32, (nb, SUB, CHUNK), 2)
    blk_io = lax.broadcasted_iota(jnp.int32, (nb, SUB, CHUNK), 0)

    def chunk_body(i, carry):
        if nc == 1:
            s, c = i, 0
        else:
            s, c = 0, i
        r0 = pl.multiple_of(i * CHUNK, CHUNK)
        c0 = pl.multiple_of(c * CHUNK, CHUNK)
        cos = cos_ref[pl.ds(c0, CHUNK), :]
        sin = sin_ref[pl.ds(c0, CHUNK), :]

        def zcol(g, h):
            return z_ref[pl.ds(r0, CHUNK), pl.ds(g * GRP + h * HD, HD)]

        heads = range(HEADS)
        ret_states = [sret_o[s, h] for h in heads]
        hg_states = [hgt_ref[s, h] for h in heads]

        r_v, r_att, r_inter = [], [], []
        for h in heads:
            q = zcol(0, h)
            k = zcol(1, h)
            r_v.append(zcol(2, h).astype(BF16))
            q = q * cos + pltpu.roll(q, HD // 2, 1) * sin
            k = k * cos + pltpu.roll(k, HD // 2, 1) * sin
            qs = (q * (HD ** -0.5)).astype(BF16)
            st = ret_states[h]
            r_att.append(_dot_nt(qs, k.astype(BF16)))
            r_inter.append(_dot((q * qdec[h]).astype(BF16), st.astype(BF16)))
            sret_o[s, h] = st * sdec[h] + _dot_tn((k * kdec[h]).astype(BF16), r_v[h])

        h_q, h_k, h_b, h_v, h_off, h_inter = [], [], [], [], [], []
        for h in heads:
            lb = lb_all[:, h * HD:(h + 1) * HD]
            sig = jax.nn.sigmoid(zcol(5, h))
            v = zcol(6, h).astype(BF16)
            q = _silu(zcol(4, h))
            k = (1.0 - lb) * (1.0 - sig)
            b = _cumsum_rows(jnp.log(lb + (1.0 - lb) * sig))
            stt = hg_states[h]
            off = []
            for blk in range(1, nb):
                lo = blk * SUB
                rb = b[lo - 1:lo, :]
                qt = (q[lo:lo + SUB] * jnp.exp(b[lo:lo + SUB] - rb)).astype(BF16)
                kt = jnp.concatenate([k[:lo] * jnp.exp(rb - b[:lo]),
                                      jnp.zeros((CHUNK - lo, HD), F32)], axis=0).astype(BF16)
                off.append(_dot_nt(qt, kt))
            h_off.append(off)
            h_inter.append(_dot_nt((q * jnp.exp(b)).astype(BF16), stt.astype(BF16)))
            bend = b[CHUNK - 1:CHUNK, :]
            kd = (k * jnp.exp(bend - b)).astype(BF16)
            hgt_ref[s, h] = stt * jnp.exp(bend) + _dot_tn(v, kd)
            h_q.append(q)
            h_k.append(k)
            h_b.append(b)
            h_v.append(v)

        for h in heads:
            o = _dot((r_att[h] * intra[h]).astype(BF16), r_v[h]) + r_inter[h]
            o = _head_norm_gate(o, rg_ref[:, pl.ds(h * HD, HD)], zcol(3, h))
            mix_ref[pl.ds(r0, CHUNK), pl.ds(h * HD, HD)] = o.astype(BF16)

        h_att = []
        for h in heads:
            b4 = h_b[h].reshape(nb, SUB, HD)
            q4 = h_q[h].reshape(nb, SUB, HD)
            k4 = h_k[h].reshape(nb, SUB, HD)
            diag = jnp.zeros((nb, SUB, CHUNK), F32)
            for sp in range(SUB):
                arg = jnp.where(t_io >= sp, b4 - b4[:, sp:sp + 1, :], -jnp.inf)
                r = jnp.sum(jnp.exp(arg) * q4 * k4[:, sp:sp + 1, :], axis=-1, keepdims=True)
                diag = jnp.where(lane_io == blk_io * SUB + sp, r, diag)
            diag = diag.reshape(CHUNK, CHUNK)
            parts = [diag[0:SUB]]
            for blk in range(1, nb):
                parts.append(diag[blk * SUB:(blk + 1) * SUB] + h_off[h][blk - 1])
            h_att.append(jnp.concatenate(parts, axis=0).astype(BF16))
        for h in heads:
            o = _dot(h_att[h], h_v[h]) + h_inter[h]
            o = _head_norm_gate(o, hg_ref[:, pl.ds(h * HD, HD)], zcol(7, h))
            mix_ref[pl.ds(r0, CHUNK), pl.ds(GRP + h * HD, HD)] = o.astype(BF16)
        return carry

    lax.fori_loop(0, ns * nc, chunk_body, 0)

    xo_ref[...] = (x + _dot(mix_ref[...], wout_ref[...])).reshape(xo_ref.shape)

    @pl.when(t == pl.num_programs(1) - 1)
    def _():
        for s in range(ns):
            for h in range(HEADS):
                shg_o[s, h] = hgt_ref[s, h].T


def _seq_tiling(b, t):
    if t >= SEQ_TILE:
        return 1, SEQ_TILE // CHUNK
    return min(b, SEQ_TILE // t), t // CHUNK


def _even_mixer(x, cos, sin, norm_g, w_in, w_out, ret_g, hg_g, lb_logits, s_ret, s_hg, layer_j):
    b, t, _ = x.shape
    ns, nc = _seq_tiling(b, t)
    tt = ns * nc * CHUNK
    grid = (b // ns, t // (nc * CHUNK))
    st_spec = pl.BlockSpec((ns, HEADS, HD, HD), lambda i, j: (i, 0, 0, 0))
    x_spec = pl.BlockSpec((ns, nc * CHUNK, D_MODEL), lambda i, j: (i, j, 0))

    def full(a):
        return pl.BlockSpec(a.shape, lambda i, j: (0,) * a.ndim)

    tab_spec = pl.BlockSpec((nc * CHUNK, HD), lambda i, j: (j, 0))
    args = (x, cos, sin, norm_g.reshape(1, D_MODEL), w_in, w_out, ret_g.reshape(1, GRP),
            hg_g.reshape(1, GRP), lb_logits, s_ret, s_hg)
    in_specs = [x_spec, tab_spec, tab_spec] + [full(a) for a in args[3:9]] + [st_spec, st_spec]
    return pl.pallas_call(
        functools.partial(_even_kernel, ns=ns, nc=nc, layer_j=layer_j),
        grid=grid,
        in_specs=in_specs,
        out_specs=[x_spec, st_spec, st_spec],
        out_shape=[jax.ShapeDtypeStruct(x.shape, F32),
                   jax.ShapeDtypeStruct(s_ret.shape, F32),
                   jax.ShapeDtypeStruct(s_hg.shape, F32)],
        scratch_shapes=[pltpu.VMEM((tt, EVEN_IN), F32), pltpu.VMEM((tt, D_MODEL), BF16),
                        pltpu.VMEM((ns, HEADS, HD, HD), F32)],
        compiler_params=pltpu.CompilerParams(
            dimension_semantics=("parallel", "arbitrary"), vmem_limit_bytes=VMEM_LIMIT),
        name="even_mixer",
    )(*args)


def _softplus(x):
    return jnp.maximum(x, 0.0) + jnp.log1p(jnp.exp(-jnp.abs(x)))


def _gelu_tanh(x):
    return 0.5 * x * (1.0 + jnp.tanh(math.sqrt(2.0 / math.pi) * (x + 0.044715 * (x * x * x))))


def _odd_kernel(x_ref, ng_ref, win_ref, wout_ref, cw_ref, lcb_ref, wa_ref, ba_ref, wx_ref,
                bx_ref, lam_ref, dnp_ref, dg_ref, lh_ref, lc_ref, ds_ref, dc_ref,
                xo_ref, lh_o, lc_o, ds_o, dc_o, z_ref, mix_ref, xp_ref, *, ns, nc):
    t = pl.program_id(1)
    tt = ns * nc * CHUNK
    conv_ch = LRU_WIDTH + 3 * GRP
    hist = CONV_W - 1
    top = 8

    @pl.when(t == 0)
    def _():
        lh_o[...] = lh_ref[...]
        lc_o[...] = lc_ref[...]
        ds_o[...] = ds_ref[...]
        dc_o[...] = dc_ref[...]

    x = x_ref[...].reshape(tt, D_MODEL)
    hn = _rms(x, ng_ref[...]).astype(BF16)
    z_ref[...] = _dot(hn, win_ref[...])

    sp_lam = _softplus(-lam_ref[...])
    dt_bias = dnp_ref[0:1, :]
    neg_a = -jnp.exp(dnp_ref[1:2, :])
    ri = lax.broadcasted_iota(jnp.int32, (CHUNK, CHUNK), 0)
    ci = lax.broadcasted_iota(jnp.int32, (CHUNK, CHUNK), 1)
    rows_w = lax.broadcasted_iota(jnp.int32, (CHUNK, LRU_WIDTH), 0)
    eye_c = (ri == ci).astype(F32)
    sub_bits = DN_SUB.bit_length() - 1
    same_blk = (ri >> sub_bits) == (ci >> sub_bits)
    merge_masks = []
    for lvl in range(sub_bits, CHUNK.bit_length() - 1):
        merge_masks.append(((ri >> (lvl + 1)) == (ci >> (lvl + 1)))
                           & (((ri >> lvl) & 1) == 1) & (((ci >> lvl) & 1) == 0))

    def chunk_body(i, carry):
        s = i if nc == 1 else 0
        r0 = pl.multiple_of(i * CHUNK, CHUNK)

        xp_ref[top - hist:top, 0:LRU_WIDTH] = lc_o[s]
        xp_ref[top - hist:top, LRU_WIDTH:conv_ch] = dc_o[s]
        xp_ref[top:top + CHUNK, 0:LRU_WIDTH] = z_ref[pl.ds(r0, CHUNK), 0:LRU_WIDTH]
        xp_ref[top:top + CHUNK, LRU_WIDTH:conv_ch] = z_ref[pl.ds(r0, CHUNK), 2 * GRP:5 * GRP]
        lc_o[s] = xp_ref[top + CHUNK - hist:top + CHUNK, 0:LRU_WIDTH]
        dc_o[s] = xp_ref[top + CHUNK - hist:top + CHUNK, LRU_WIDTH:conv_ch]
        y = xp_ref[top - hist:top - hist + CHUNK, :] * cw_ref[0:1, :]
        for jj in range(1, CONV_W):
            y = y + xp_ref[top - hist + jj:top - hist + jj + CHUNK, :] * cw_ref[jj:jj + 1, :]

        def rg_lru_branch():
            lx = y[:, 0:LRU_WIDTH] + lcb_ref[...]
            xb = lx.astype(BF16)
            r = jax.nn.sigmoid(_dot(xb, wa_ref[...]) + ba_ref[...])
            ig = jax.nn.sigmoid(_dot(xb, wx_ref[...]) + bx_ref[...])
            a = jnp.exp(-LRU_C * r * sp_lam)
            u = jnp.sqrt(1.0 - a * a) * (ig * lx)
            k = 1
            while k < CHUNK:
                m = rows_w >= k
                u = jnp.where(m, a * pltpu.roll(u, k, 0) + u, u)
                a = jnp.where(m, a * pltpu.roll(a, k, 0), a)
                k *= 2
            hseq = a * lh_o[s] + u
            lh_o[s] = hseq[CHUNK - 1:CHUNK]
            lg = z_ref[pl.ds(r0, CHUNK), GRP:2 * GRP]
            mix_ref[pl.ds(r0, CHUNK), 0:LRU_WIDTH] = (_gelu_tanh(lg) * hseq).astype(BF16)

        zs = z_ref[pl.ds(r0, CHUNK), 6 * GRP:6 * GRP + HD]
        beta_t = jax.nn.sigmoid(zs)
        g_t = _cumsum_rows(neg_a * _softplus(zs + dt_bias))
        g_tt = g_t.T
        heads = range(HEADS)
        states = [ds_o[s, h] for h in heads]
        kn, both_k, both_s = [], [], []
        for h in heads:
            base = LRU_WIDTH + h * HD
            q = _silu(y[:, base:base + HD])
            kk_ = _silu(y[:, base + GRP:base + GRP + HD])
            q = q * lax.rsqrt(jnp.sum(q * q, axis=-1, keepdims=True) + EPS) * (HD ** -0.5)
            kn.append(kk_ * lax.rsqrt(jnp.sum(kk_ * kk_, axis=-1, keepdims=True) + EPS))
            knb = kn[h].astype(BF16)
            qk_in = jnp.concatenate([q.astype(BF16), knb], axis=0)
            both_k.append(_dot_nt(qk_in, knb))
            both_s.append(_dot(qk_in, states[h].astype(BF16)))
        gcol, eg, e_incl, amat, rhs = [], [], [], [], []
        for h in heads:
            v = _silu(y[:, LRU_WIDTH + 2 * GRP + h * HD:LRU_WIDTH + 2 * GRP + (h + 1) * HD])
            gcol.append(g_t[:, HEADS + h:HEADS + h + 1])
            rel = gcol[h] - g_tt[HEADS + h:HEADS + h + 1, :]
            e_strict = jnp.exp(jnp.where(ri > ci, rel, -jnp.inf))
            e_incl.append(jnp.where(ri == ci, 1.0, e_strict))
            beta = beta_t[:, h:h + 1]
            eg.append(jnp.exp(gcol[h]))
            amat.append(beta * both_k[h][CHUNK:] * e_strict)
            rhs.append(beta * (v - eg[h] * both_s[h][CHUNK:]))
        dblk = [jnp.where(same_blk, amat[h], 0.0) for h in heads]
        tinv = [eye_c for _ in heads]
        for jj in range(DN_SUB - 1):
            for h in heads:
                colv = jnp.concatenate(
                    [dblk[h][b0:b0 + DN_SUB, b0 + jj:b0 + jj + 1]
                     for b0 in range(0, CHUNK, DN_SUB)], axis=0)
                rowm = jnp.concatenate(
                    [jnp.broadcast_to(tinv[h][b0 + jj:b0 + jj + 1, :], (DN_SUB, CHUNK))
                     for b0 in range(0, CHUNK, DN_SUB)], axis=0)
                tinv[h] = tinv[h] - colv * rowm
        rg_lru_branch()
        tb = [tinv[h].astype(BF16) for h in heads]
        pend = [[_dot(tb[h], jnp.where(lm, amat[h], 0.0).astype(BF16)) for lm in merge_masks]
                for h in heads]
        for _ in merge_masks:
            xb = [pend[h][0].astype(BF16) for h in heads]
            tb = [tinv[h].astype(BF16) for h in heads]
            tinv = [tinv[h] - _dot(xb[h], tb[h]) for h in heads]
            pend = [[p - _dot(xb[h], p.astype(BF16)) for p in pend[h][1:]] for h in heads]
        w = [_dot(tinv[h].astype(BF16), rhs[h].astype(BF16)).astype(BF16) for h in heads]
        for h in heads:
            o = eg[h] * both_s[h][:CHUNK] + _dot((both_k[h][:CHUNK] * e_incl[h]).astype(BF16), w[h])
            gend = gcol[h][CHUNK - 1:CHUNK, :]
            kd = (kn[h] * jnp.exp(gend - gcol[h])).astype(BF16)
            ds_o[s, h] = states[h] * jnp.exp(gend) + _dot_tn(kd, w[h])
            o = _head_norm_gate(o, dg_ref[:, pl.ds(h * HD, HD)],
                                z_ref[pl.ds(r0, CHUNK), pl.ds(5 * GRP + h * HD, HD)])
            mix_ref[pl.ds(r0, CHUNK), pl.ds(LRU_WIDTH + h * HD, HD)] = o.astype(BF16)
        return carry

    lax.fori_loop(0, ns * nc, chunk_body, 0, unroll=2)

    xo_ref[...] = (x + _dot(mix_ref[...], wout_ref[...])).reshape(xo_ref.shape)


def _odd_mixer(x, norm_g, w_in, w_out, cw, lcb, wa, ba, wx, bx, lam, dnp, dn_g,
               s_lh, s_lc, s_dn, s_dc):
    b, t, _ = x.shape
    ns, nc = _seq_tiling(b, t)
    tt = ns * nc * CHUNK
    grid = (b // ns, t // (nc * CHUNK))
    x_spec = pl.BlockSpec((ns, nc * CHUNK, D_MODEL), lambda i, j: (i, j, 0))

    def full(a):
        return pl.BlockSpec(a.shape, lambda i, j: (0,) * a.ndim)

    def st_spec(a):
        return pl.BlockSpec((ns,) + a.shape[1:], lambda i, j: (i,) + (0,) * (a.ndim - 1))

    params = (norm_g.reshape(1, D_MODEL), w_in, w_out, cw, lcb, wa, ba, wx, bx, lam, dnp, dn_g)
    states = (s_lh, s_lc, s_dn, s_dc)
    return pl.pallas_call(
        functools.partial(_odd_kernel, ns=ns, nc=nc),
        grid=grid,
        in_specs=[x_spec] + [full(a) for a in params] + [st_spec(a) for a in states],
        out_specs=[x_spec] + [st_spec(a) for a in states],
        out_shape=[jax.ShapeDtypeStruct(x.shape, F32)]
        + [jax.ShapeDtypeStruct(a.shape, F32) for a in states],
        scratch_shapes=[pltpu.VMEM((tt, ODD_IN_PAD), F32), pltpu.VMEM((tt, D_MODEL), BF16),
                        pltpu.VMEM((8 + CHUNK, LRU_WIDTH + 3 * GRP), F32)],
        compiler_params=pltpu.CompilerParams(
            dimension_semantics=("parallel", "arbitrary"), vmem_limit_bytes=VMEM_LIMIT),
        name="odd_mixer",
    )(x, *params, *states)


def _rope_tables(pos):
    half = HD // 2
    freq = ROPE_BASE ** (-jnp.arange(half, dtype=F32) / half)
    ang = pos.astype(F32)[:, None] * freq[None, :]
    cos, sin = jnp.cos(ang), jnp.sin(ang)
    return jnp.concatenate([cos, cos], axis=-1), jnp.concatenate([-sin, sin], axis=-1)


def _block_diag(w):
    n, bs, _ = w.shape
    eye = jnp.eye(n, dtype=w.dtype)
    return (eye[:, None, :, None] * w[:, :, None, :]).reshape(n * bs, n * bs)


def kernel(x_prompt, x_sample, state_ret, state_hgrn, state_lru_h, state_lru_conv, state_dn,
           state_dn_conv, ffn1_norm, ffn1_w_in, ffn1_w_out, mix_norm, ffn2_norm, ffn2_w_in,
           ffn2_w_out, final_norm, even_w_in, even_w_out, ret_out_norm, hg_out_norm,
           hg_lb_logits, odd_w_in, odd_w_out, lru_conv_w, lru_conv_b, lru_w_a, lru_b_a, lru_w_x,
           lru_b_x, lru_lambda, dn_conv_w, dn_a_log, dn_dt_bias, dn_out_norm):
    depth = ffn1_norm.shape[0]
    bp, tp, _ = x_prompt.shape
    bs, ts, _ = x_sample.shape
    past_len = 2048
    tabs = (_rope_tables(jnp.arange(tp)), _rope_tables(past_len + jnp.arange(ts)))
    xs = [x_prompt, x_sample]
    nb = (bp, bs)
    w1_in, w1_out = ffn1_w_in.astype(BF16), ffn1_w_out.astype(BF16)
    w2_in, w2_out = ffn2_w_in.astype(BF16), ffn2_w_out.astype(BF16)
    outs = {k: ([], []) for k in ("ret", "hg", "lh", "lc", "dn", "dc")}

    for l in range(depth):
        j = l // 2
        last = l == depth - 1
        for g in range(2):
            xs[g] = _ffn(xs[g], ffn1_norm, w1_in, w1_out, l, final_norm, False)
        if l % 2 == 0:
            w_in = even_w_in[j].astype(BF16)
            w_out = even_w_out[j].astype(BF16)
            for g in range(2):
                if g == 0:
                    s_ret = jnp.zeros((bp, HEADS, HD, HD), F32)
                    s_hg = jnp.zeros((bp, HEADS, HD, HD), F32)
                else:
                    s_ret, s_hg = state_ret[j], state_hgrn[j]
                xs[g], n_ret, n_hg = _even_mixer(
                    xs[g], tabs[g][0], tabs[g][1], mix_norm[l], w_in, w_out, ret_out_norm[j],
                    hg_out_norm[j], hg_lb_logits, s_ret, s_hg, j)
                outs["ret"][g].append(n_ret)
                outs["hg"][g].append(n_hg)
        else:
            w_in = jnp.pad(odd_w_in[j], ((0, 0), (0, ODD_IN_PAD - odd_w_in.shape[-1]))).astype(BF16)
            w_out = odd_w_out[j].astype(BF16)
            cw = jnp.concatenate([lru_conv_w[j], dn_conv_w[j]], axis=-1)
            wa = _block_diag(lru_w_a[j]).astype(BF16)
            wx = _block_diag(lru_w_x[j]).astype(BF16)
            dnp = jnp.zeros((2, HD), F32)
            dnp = dnp.at[0, HEADS:2 * HEADS].set(dn_dt_bias[j])
            dnp = dnp.at[1, HEADS:2 * HEADS].set(dn_a_log[j])
            for g in range(2):
                if g == 0:
                    s_lh = jnp.zeros((bp, 1, LRU_WIDTH), F32)
                    s_lc = jnp.zeros((bp, CONV_W - 1, LRU_WIDTH), F32)
                    s_dn = jnp.zeros((bp, HEADS, HD, HD), F32)
                    s_dc = jnp.zeros((bp, CONV_W - 1, 3 * GRP), F32)
                else:
                    s_lh = state_lru_h[j].reshape(bs, 1, LRU_WIDTH)
                    s_lc, s_dn, s_dc = state_lru_conv[j], state_dn[j], state_dn_conv[j]
                xs[g], n_lh, n_lc, n_dn, n_dc = _odd_mixer(
                    xs[g], mix_norm[l], w_in, w_out, cw, lru_conv_b[j].reshape(1, -1), wa,
                    lru_b_a[j].reshape(1, -1), wx, lru_b_x[j].reshape(1, -1),
                    lru_lambda[j].reshape(1, -1), dnp, dn_out_norm[j].reshape(1, -1),
                    s_lh, s_lc, s_dn, s_dc)
                outs["lh"][g].append(n_lh.reshape(nb[g], LRU_WIDTH))
                outs["lc"][g].append(n_lc)
                outs["dn"][g].append(n_dn)
                outs["dc"][g].append(n_dc)
        for g in range(2):
            xs[g] = _ffn(xs[g], ffn2_norm, w2_in, w2_out, l, final_norm, last)

    res = [xs[0], xs[1]]
    for k in ("ret", "hg", "lh", "lc", "dn", "dc"):
        for g in range(2):
            res.append(jnp.stack(outs[k][g]))
    return tuple(res)
```

```python
import functools
import math

import jax
import jax.numpy as jnp
from jax import lax
from jax.experimental import pallas as pl
from jax.experimental.pallas import tpu as pltpu

F32 = jnp.float32
BF16 = jnp.bfloat16

D_MODEL = 1024
FF_DIM = 2816
EPS = 1e-6
CHUNK = 64
CONV_W = 4
HEADS = 4
HD = 128
ROPE_BASE = 10000.0
LRU_WIDTH = 512
LRU_C = 8.0
GRP = HEADS * HD
EVEN_IN = 8 * GRP
ODD_IN_PAD = 6 * GRP + HD
SUB = 8
DN_SUB = 8

VMEM_LIMIT = 56 * 1024 * 1024

FFN_TM = 1024
FFN_TF = 256
SEQ_TILE = 512

RET_LOG_G = [math.log1p(-(2.0 ** (-5.0 - h))) for h in range(HEADS)]


def _rms(x, g):
    return x * lax.rsqrt(jnp.mean(x * x, axis=-1, keepdims=True) + EPS) * g


def _silu(x):
    return x * jax.nn.sigmoid(x)


def _dot(a, b):
    return jnp.dot(a, b, preferred_element_type=F32)


def _dot_nt(a, b):
    return lax.dot_general(a, b, (((1,), (1,)), ((), ())), preferred_element_type=F32)


def _dot_tn(a, b):
    return lax.dot_general(a, b, (((0,), (0,)), ((), ())), preferred_element_type=F32)


def _head_norm(o, g):
    return o * lax.rsqrt(jnp.mean(o * o, axis=-1, keepdims=True) + EPS) * g


def _hcols(h):
    return slice(h * HD, (h + 1) * HD)


def _chunk_cumsum(x):
    pos = lax.broadcasted_iota(jnp.int32, x.shape, 0) & (CHUNK - 1)
    k = 1
    while k < CHUNK:
        x = jnp.where(pos >= k, x + pltpu.roll(x, k, 0), x)
        k *= 2
    return x


def _resident(block_shape, index_map):
    return pl.BlockSpec(block_shape, index_map, pipeline_mode=pl.Buffered(1))


def _ffn_kernel(x_ref, g_ref, win_ref, wout_ref, fin_ref, o_ref, act_ref, *, apply_final):
    x = x_ref[...]
    h = _rms(x, g_ref[...]).astype(BF16)
    for c in range(FF_DIM // FFN_TF):
        lo = c * FFN_TF
        gate = _dot(h, win_ref[:, lo:lo + FFN_TF])
        up = _dot(h, win_ref[:, FF_DIM + lo:FF_DIM + lo + FFN_TF])
        act_ref[:, lo:lo + FFN_TF] = (_silu(gate) * up).astype(BF16)
    y = x + 0.5 * _dot(act_ref[...], wout_ref[...])
    if apply_final:
        y = _rms(y, fin_ref[...])
    o_ref[...] = y


def _ffn(x, norm_g, w_in, w_out, layer, final_g, apply_final):
    shape = x.shape
    x2 = x.reshape(-1, D_MODEL)
    m = x2.shape[0]
    tm = min(FFN_TM, m)
    out = pl.pallas_call(
        functools.partial(_ffn_kernel, apply_final=apply_final),
        grid=(m // tm,),
        in_specs=[
            pl.BlockSpec((tm, D_MODEL), lambda i: (i, 0)),
            _resident((None, 1, D_MODEL), lambda i: (layer, 0, 0)),
            _resident((None, D_MODEL, 2 * FF_DIM), lambda i: (layer, 0, 0)),
            _resident((None, FF_DIM, D_MODEL), lambda i: (layer, 0, 0)),
            _resident((1, D_MODEL), lambda i: (0, 0)),
        ],
        out_specs=pl.BlockSpec((tm, D_MODEL), lambda i: (i, 0)),
        out_shape=jax.ShapeDtypeStruct((m, D_MODEL), F32),
        scratch_shapes=[pltpu.VMEM((tm, FF_DIM), BF16)],
        compiler_params=pltpu.CompilerParams(
            dimension_semantics=("parallel",), vmem_limit_bytes=VMEM_LIMIT),
        name="ffn",
    )(x2, norm_g.reshape(-1, 1, D_MODEL), w_in, w_out, final_g.reshape(1, D_MODEL))
    return out.reshape(shape)


def _seq_tiling(b, t):
    if t >= SEQ_TILE:
        return 1, SEQ_TILE // CHUNK
    return min(b, SEQ_TILE // t), t // CHUNK


def _even_kernel(x_ref, cos_ref, sin_ref, ng_ref, win_ref, wout_ref, rg_ref, hg_ref, lbl_ref,
                 sret_ref, shg_ref, xo_ref, sret_o, shg_o,
                 qs_ref, qd_ref, kb_ref, kd_ref, rv_ref, hv_ref,
                 rgate_ref, hq_ref, hf_ref, hgate_ref, mix_ref, hgt_ref, *, ns, nc, layer_j):
    t = pl.program_id(1)
    tt = ns * nc * CHUNK
    heads = range(HEADS)

    @pl.when(t == 0)
    def _():
        sret_o[...] = sret_ref[...]
        for s in range(ns):
            for h in heads:
                hgt_ref[s, h] = shg_ref[s, h].T

    x = x_ref[...].reshape(tt, D_MODEL)
    hn = _rms(x, ng_ref[...]).astype(BF16)

    def proj(g):
        return _dot(hn, win_ref[:, g * GRP:(g + 1) * GRP])

    cos = jnp.concatenate([cos_ref[...]] * ns, axis=0)
    sin = jnp.concatenate([sin_ref[...]] * ns, axis=0)
    pos = (lax.broadcasted_iota(jnp.int32, (tt, HD), 0) & (CHUNK - 1)).astype(F32)
    scale = HD ** -0.5

    def rope(z, h):
        v = z[:, _hcols(h)]
        return v * cos + pltpu.roll(v, HD // 2, 1) * sin

    z0 = proj(0)
    z1 = proj(1)
    for h in heads:
        q = rope(z0, h)
        qs_ref[:, _hcols(h)] = (q * scale).astype(BF16)
        qd_ref[:, _hcols(h)] = (q * (jnp.exp(RET_LOG_G[h] * (pos + 1.0)) * scale)).astype(BF16)
    z2 = proj(2)
    for h in heads:
        k = rope(z1, h)
        kb_ref[:, _hcols(h)] = k.astype(BF16)
        kd_ref[:, _hcols(h)] = (k * jnp.exp(RET_LOG_G[h] * (CHUNK - 1.0 - pos))).astype(BF16)
    z3 = proj(3)
    rv_ref[...] = z2.astype(BF16)
    z4 = proj(4)
    rgate_ref[...] = _silu(z3)
    z5 = proj(5)
    hq_ref[...] = _silu(z4)
    z6 = proj(6)
    hf_ref[...] = z5
    z7 = proj(7)
    hv_ref[...] = z6.astype(BF16)
    hgate_ref[...] = _silu(z7)

    ri = lax.broadcasted_iota(jnp.int32, (CHUNK, CHUNK), 0)
    ci = lax.broadcasted_iota(jnp.int32, (CHUNK, CHUNK), 1)
    absd = jnp.abs(ri - ci).astype(F32)
    intra = [jnp.exp(RET_LOG_G[h] * absd) for h in heads]
    sdec = [math.exp(RET_LOG_G[h] * CHUNK) for h in heads]

    lbl = lbl_ref[...]
    e = jnp.exp(lbl - jnp.max(lbl, axis=0, keepdims=True))
    sm = e / jnp.sum(e, axis=0, keepdims=True)
    lb_all = jnp.sum(sm[:layer_j + 1], axis=0, keepdims=True)

    nb = CHUNK // SUB
    t_io = lax.broadcasted_iota(jnp.int32, (nb, SUB, HD), 1)
    lane_io = lax.broadcasted_iota(jnp.int32, (nb, SUB, CHUNK), 2)
    blk_io = lax.broadcasted_iota(jnp.int32, (nb, SUB, CHUNK), 0)

    def chunk_body(i, carry):
        s = i if nc == 1 else 0
        r0 = pl.multiple_of(i * CHUNK, CHUNK)

        def ld(ref, h):
            return ref[pl.ds(r0, CHUNK), pl.ds(h * HD, HD)]

        ret_states = [sret_o[s, h] for h in heads]
        hg_states = [hgt_ref[s, h] for h in heads]

        r_v, r_att, r_inter = [], [], []
        for h in heads:
            st = ret_states[h]
            r_v.append(ld(rv_ref, h))
            r_att.append(_dot_nt(ld(qs_ref, h), ld(kb_ref, h)))
            r_inter.append(_dot(ld(qd_ref, h), st.astype(BF16)))
            sret_o[s, h] = st * sdec[h] + _dot_tn(ld(kd_ref, h), r_v[h])

        h_q, h_k, h_b, h_v, h_off, h_inter = [], [], [], [], [], []
        for h in heads:
            lb = lb_all[:, _hcols(h)]
            sig = jax.nn.sigmoid(ld(hf_ref, h))
            v = ld(hv_ref, h)
            q = ld(hq_ref, h)
            k = (1.0 - lb) * (1.0 - sig)
            b = _chunk_cumsum(jnp.log(lb + (1.0 - lb) * sig))
            stt = hg_states[h]
            off = []
            for blk in range(1, nb):
                lo = blk * SUB
                rb = b[lo - 1:lo, :]
                qt = (q[lo:lo + SUB] * jnp.exp(b[lo:lo + SUB] - rb)).astype(BF16)
                kt = jnp.concatenate([k[:lo] * jnp.exp(rb - b[:lo]),
                                      jnp.zeros((CHUNK - lo, HD), F32)], axis=0).astype(BF16)
                off.append(_dot_nt(qt, kt))
            h_off.append(off)
            h_inter.append(_dot_nt((q * jnp.exp(b)).astype(BF16), stt.astype(BF16)))
            bend = b[CHUNK - 1:CHUNK, :]
            kd = (k * jnp.exp(bend - b)).astype(BF16)
            hgt_ref[s, h] = stt * jnp.exp(bend) + _dot_tn(v, kd)
            h_q.append(q)
            h_k.append(k)
            h_b.append(b)
            h_v.append(v)

        for h in heads:
            o = _dot((r_att[h] * intra[h]).astype(BF16), r_v[h]) + r_inter[h]
            o = _head_norm(o, rg_ref[:, pl.ds(h * HD, HD)]) * ld(rgate_ref, h)
            mix_ref[pl.ds(r0, CHUNK), pl.ds(h * HD, HD)] = o.astype(BF16)

        h_att = []
        for h in heads:
            b4 = h_b[h].reshape(nb, SUB, HD)
            q4 = h_q[h].reshape(nb, SUB, HD)
            k4 = h_k[h].reshape(nb, SUB, HD)
            diag = jnp.zeros((nb, SUB, CHUNK), F32)
            for sp in range(SUB):
                arg = jnp.where(t_io >= sp, b4 - b4[:, sp:sp + 1, :], -jnp.inf)
                r = jnp.sum(jnp.exp(arg) * q4 * k4[:, sp:sp + 1, :], axis=-1, keepdims=True)
                diag = jnp.where(lane_io == blk_io * SUB + sp, r, diag)
            diag = diag.reshape(CHUNK, CHUNK)
            parts = [diag[0:SUB]]
            for blk in range(1, nb):
                parts.append(diag[blk * SUB:(blk + 1) * SUB] + h_off[h][blk - 1])
            h_att.append(jnp.concatenate(parts, axis=0).astype(BF16))
        for h in heads:
            o = _dot(h_att[h], h_v[h]) + h_inter[h]
            o = _head_norm(o, hg_ref[:, pl.ds(h * HD, HD)]) * ld(hgate_ref, h)
            mix_ref[pl.ds(r0, CHUNK), pl.ds(GRP + h * HD, HD)] = o.astype(BF16)
        return carry

    lax.fori_loop(0, ns * nc, chunk_body, 0)

    xo_ref[...] = (x + _dot(mix_ref[...], wout_ref[...])).reshape(xo_ref.shape)

    @pl.when(t == pl.num_programs(1) - 1)
    def _():
        for s in range(ns):
            for h in heads:
                shg_o[s, h] = hgt_ref[s, h].T


def _even_mixer(x, cos, sin, norm_g, w_in, w_out, ret_g, hg_g, lb_logits, s_ret, s_hg, layer_j):
    b, t, _ = x.shape
    ns, nc = _seq_tiling(b, t)
    tt = ns * nc * CHUNK
    grid = (b // ns, t // (nc * CHUNK))
    st_spec = pl.BlockSpec((ns, HEADS, HD, HD), lambda i, j: (i, 0, 0, 0))
    x_spec = pl.BlockSpec((ns, nc * CHUNK, D_MODEL), lambda i, j: (i, j, 0))

    def full(a):
        return _resident(a.shape, lambda i, j: (0,) * a.ndim)

    tab_spec = pl.BlockSpec((nc * CHUNK, HD), lambda i, j: (j, 0))
    args = (x, cos, sin, norm_g.reshape(1, D_MODEL), w_in, w_out, ret_g.reshape(1, GRP),
            hg_g.reshape(1, GRP), lb_logits, s_ret, s_hg)
    in_specs = [x_spec, tab_spec, tab_spec] + [full(a) for a in args[3:9]] + [st_spec, st_spec]
    grp_bf16 = pltpu.VMEM((tt, GRP), BF16)
    grp_f32 = pltpu.VMEM((tt, GRP), F32)
    return pl.pallas_call(
        functools.partial(_even_kernel, ns=ns, nc=nc, layer_j=layer_j),
        grid=grid,
        in_specs=in_specs,
        out_specs=[x_spec, st_spec, st_spec],
        out_shape=[jax.ShapeDtypeStruct(x.shape, F32),
                   jax.ShapeDtypeStruct(s_ret.shape, F32),
                   jax.ShapeDtypeStruct(s_hg.shape, F32)],
        scratch_shapes=[grp_bf16] * 6 + [grp_f32] * 4
        + [pltpu.VMEM((tt, D_MODEL), BF16), pltpu.VMEM((ns, HEADS, HD, HD), F32)],
        compiler_params=pltpu.CompilerParams(
            dimension_semantics=("parallel", "arbitrary"), vmem_limit_bytes=VMEM_LIMIT),
        name="even_mixer",
    )(*args)


def _softplus(x):
    return jnp.maximum(x, 0.0) + jnp.log1p(jnp.exp(-jnp.abs(x)))


def _gelu_tanh(x):
    return 0.5 * x * (1.0 + jnp.tanh(math.sqrt(2.0 / math.pi) * (x + 0.044715 * (x * x * x))))


def _odd_kernel(x_ref, ng_ref, win_ref, wout_ref, cw_ref, lcb_ref, wa_ref, ba_ref, wx_ref,
                bx_ref, lam_ref, dnp_ref, dg_ref, lh_ref, lc_ref, ds_ref, dc_ref,
                xo_ref, lh_o, lc_o, ds_o, dc_o,
                z_ref, gl_ref, sg_ref, bt_ref, gt_ref, gtt_ref, mix_ref, xp_ref, *, ns, nc):
    t = pl.program_id(1)
    tt = ns * nc * CHUNK
    conv_ch = LRU_WIDTH + 3 * GRP
    hist = CONV_W - 1
    top = 8
    heads = range(HEADS)

    @pl.when(t == 0)
    def _():
        lh_o[...] = lh_ref[...]
        lc_o[...] = lc_ref[...]
        ds_o[...] = ds_ref[...]
        dc_o[...] = dc_ref[...]

    x = x_ref[...].reshape(tt, D_MODEL)
    hn = _rms(x, ng_ref[...]).astype(BF16)

    def proj(lo, width=GRP):
        return _dot(hn, win_ref[:, lo:lo + width])

    zs = proj(6 * GRP, HD)
    z5 = proj(5 * GRP)
    bt_ref[...] = jax.nn.sigmoid(zs)
    g_all = _chunk_cumsum(-jnp.exp(dnp_ref[1:2, :]) * _softplus(zs + dnp_ref[0:1, :]))
    gt_ref[...] = g_all
    for c in range(ns * nc):
        gtt_ref[c] = g_all[c * CHUNK:(c + 1) * CHUNK, :].T
    z1 = proj(GRP)
    sg_ref[...] = _silu(z5)
    z_ref[:, 0:GRP] = proj(0)
    gl_ref[...] = _gelu_tanh(z1)
    for g in range(3):
        z_ref[:, (g + 1) * GRP:(g + 2) * GRP] = proj((g + 2) * GRP)

    sp_lam = _softplus(-lam_ref[...])
    ri = lax.broadcasted_iota(jnp.int32, (CHUNK, CHUNK), 0)
    ci = lax.broadcasted_iota(jnp.int32, (CHUNK, CHUNK), 1)
    rows_w = lax.broadcasted_iota(jnp.int32, (CHUNK, LRU_WIDTH), 0)
    eye_c = (ri == ci).astype(F32)
    sub_bits = DN_SUB.bit_length() - 1
    same_blk = (ri >> sub_bits) == (ci >> sub_bits)
    merge_masks = []
    for lvl in range(sub_bits, CHUNK.bit_length() - 1):
        merge_masks.append(((ri >> (lvl + 1)) == (ci >> (lvl + 1)))
                           & (((ri >> lvl) & 1) == 1) & (((ci >> lvl) & 1) == 0))

    def chunk_body(i, carry):
        s = i if nc == 1 else 0
        r0 = pl.multiple_of(i * CHUNK, CHUNK)
        rows = pl.ds(r0, CHUNK)

        xp_ref[top - hist:top, 0:LRU_WIDTH] = lc_o[s]
        xp_ref[top - hist:top, LRU_WIDTH:conv_ch] = dc_o[s]
        xp_ref[top:top + CHUNK, :] = z_ref[rows, :]
        lc_o[s] = xp_ref[top + CHUNK - hist:top + CHUNK, 0:LRU_WIDTH]
        dc_o[s] = xp_ref[top + CHUNK - hist:top + CHUNK, LRU_WIDTH:conv_ch]
        y = xp_ref[top - hist:top - hist + CHUNK, :] * cw_ref[0:1, :]
        for jj in range(1, CONV_W):
            y = y + xp_ref[top - hist + jj:top - hist + jj + CHUNK, :] * cw_ref[jj:jj + 1, :]

        def rg_lru_branch():
            lx = y[:, 0:LRU_WIDTH] + lcb_ref[...]
            xb = lx.astype(BF16)
            r = jax.nn.sigmoid(_dot(xb, wa_ref[...]) + ba_ref[...])
            ig = jax.nn.sigmoid(_dot(xb, wx_ref[...]) + bx_ref[...])
            a = jnp.exp(-LRU_C * r * sp_lam)
            u = jnp.sqrt(1.0 - a * a) * (ig * lx)
            k = 1
            while k < CHUNK:
                m = rows_w >= k
                u = jnp.where(m, a * pltpu.roll(u, k, 0) + u, u)
                a = jnp.where(m, a * pltpu.roll(a, k, 0), a)
                k *= 2
            hseq = a * lh_o[s] + u
            lh_o[s] = hseq[CHUNK - 1:CHUNK]
            mix_ref[rows, 0:LRU_WIDTH] = (gl_ref[rows, :] * hseq).astype(BF16)

        beta_t = bt_ref[rows, :]
        g_t = gt_ref[rows, :]
        g_tt = gtt_ref[i]
        states = [ds_o[s, h] for h in heads]
        kn, both_k, both_s = [], [], []
        for h in heads:
            base = LRU_WIDTH + h * HD
            q = _silu(y[:, base:base + HD])
            kk_ = _silu(y[:, base + GRP:base + GRP + HD])
            q = q * lax.rsqrt(jnp.sum(q * q, axis=-1, keepdims=True) + EPS) * (HD ** -0.5)
            kn.append(kk_ * lax.rsqrt(jnp.sum(kk_ * kk_, axis=-1, keepdims=True) + EPS))
            knb = kn[h].astype(BF16)
            qk_in = jnp.concatenate([q.astype(BF16), knb], axis=0)
            both_k.append(_dot_nt(qk_in, knb))
            both_s.append(_dot(qk_in, states[h].astype(BF16)))
        gcol, eg, e_incl, amat, rhs = [], [], [], [], []
        for h in heads:
            v = _silu(y[:, LRU_WIDTH + 2 * GRP + h * HD:LRU_WIDTH + 2 * GRP + (h + 1) * HD])
            gcol.append(g_t[:, HEADS + h:HEADS + h + 1])
            rel = gcol[h] - g_tt[HEADS + h:HEADS + h + 1, :]
            e_strict = jnp.exp(jnp.where(ri > ci, rel, -jnp.inf))
            e_incl.append(jnp.where(ri == ci, 1.0, e_strict))
            beta = beta_t[:, h:h + 1]
            eg.append(jnp.exp(gcol[h]))
            amat.append(beta * both_k[h][CHUNK:] * e_strict)
            rhs.append(beta * (v - eg[h] * both_s[h][CHUNK:]))
        dblk = [jnp.where(same_blk, amat[h], 0.0) for h in heads]
        tinv = [eye_c for _ in heads]
        for jj in range(DN_SUB - 1):
            for h in heads:
                colv = jnp.concatenate(
                    [dblk[h][b0:b0 + DN_SUB, b0 + jj:b0 + jj + 1]
                     for b0 in range(0, CHUNK, DN_SUB)], axis=0)
                rowm = jnp.concatenate(
                    [jnp.broadcast_to(tinv[h][b0 + jj:b0 + jj + 1, :], (DN_SUB, CHUNK))
                     for b0 in range(0, CHUNK, DN_SUB)], axis=0)
                tinv[h] = tinv[h] - colv * rowm
        rg_lru_branch()
        tb = [tinv[h].astype(BF16) for h in heads]
        pend = [[_dot(tb[h], jnp.where(lm, amat[h], 0.0).astype(BF16)) for lm in merge_masks]
                for h in heads]
        for _ in merge_masks:
            xb_ = [pend[h][0].astype(BF16) for h in heads]
            tb = [tinv[h].astype(BF16) for h in heads]
            tinv = [tinv[h] - _dot(xb_[h], tb[h]) for h in heads]
            pend = [[p - _dot(xb_[h], p.astype(BF16)) for p in pend[h][1:]] for h in heads]
        w = [_dot(tinv[h].astype(BF16), rhs[h].astype(BF16)).astype(BF16) for h in heads]
        for h in heads:
            o = eg[h] * both_s[h][:CHUNK] + _dot((both_k[h][:CHUNK] * e_incl[h]).astype(BF16), w[h])
            gend = gcol[h][CHUNK - 1:CHUNK, :]
            kd = (kn[h] * jnp.exp(gend - gcol[h])).astype(BF16)
            ds_o[s, h] = states[h] * jnp.exp(gend) + _dot_tn(kd, w[h])
            o = _head_norm(o, dg_ref[:, pl.ds(h * HD, HD)]) * sg_ref[rows, pl.ds(h * HD, HD)]
            mix_ref[rows, pl.ds(LRU_WIDTH + h * HD, HD)] = o.astype(BF16)
        return carry

    lax.fori_loop(0, ns * nc, chunk_body, 0, unroll=2)

    xo_ref[...] = (x + _dot(mix_ref[...], wout_ref[...])).reshape(xo_ref.shape)


def _odd_mixer(x, norm_g, w_in, w_out, cw, lcb, wa, ba, wx, bx, lam, dnp, dn_g,
               s_lh, s_lc, s_dn, s_dc):
    b, t, _ = x.shape
    ns, nc = _seq_tiling(b, t)
    tt = ns * nc * CHUNK
    grid = (b // ns, t // (nc * CHUNK))
    x_spec = pl.BlockSpec((ns, nc * CHUNK, D_MODEL), lambda i, j: (i, j, 0))

    def full(a):
        return _resident(a.shape, lambda i, j: (0,) * a.ndim)

    def st_spec(a):
        return pl.BlockSpec((ns,) + a.shape[1:], lambda i, j: (i,) + (0,) * (a.ndim - 1))

    params = (norm_g.reshape(1, D_MODEL), w_in, w_out, cw, lcb, wa, ba, wx, bx, lam, dnp, dn_g)
    states = (s_lh, s_lc, s_dn, s_dc)
    grp_f32 = pltpu.VMEM((tt, GRP), F32)
    head_f32 = pltpu.VMEM((tt, HD), F32)
    return pl.pallas_call(
        functools.partial(_odd_kernel, ns=ns, nc=nc),
        grid=grid,
        in_specs=[x_spec] + [full(a) for a in params] + [st_spec(a) for a in states],
        out_specs=[x_spec] + [st_spec(a) for a in states],
        out_shape=[jax.ShapeDtypeStruct(x.shape, F32)]
        + [jax.ShapeDtypeStruct(a.shape, F32) for a in states],
        scratch_shapes=[pltpu.VMEM((tt, LRU_WIDTH + 3 * GRP), F32), grp_f32, grp_f32,
                        head_f32, head_f32, pltpu.VMEM((ns * nc, HD, CHUNK), F32),
                        pltpu.VMEM((tt, D_MODEL), BF16),
                        pltpu.VMEM((8 + CHUNK, LRU_WIDTH + 3 * GRP), F32)],
        compiler_params=pltpu.CompilerParams(
            dimension_semantics=("parallel", "arbitrary"), vmem_limit_bytes=VMEM_LIMIT),
        name="odd_mixer",
    )(x, *params, *states)


def _rope_tables(pos):
    half = HD // 2
    freq = ROPE_BASE ** (-jnp.arange(half, dtype=F32) / half)
    ang = pos.astype(F32)[:, None] * freq[None, :]
    cos, sin = jnp.cos(ang), jnp.sin(ang)
    return jnp.concatenate([cos, cos], axis=-1), jnp.concatenate([-sin, sin], axis=-1)


def _block_diag(w):
    n, bs, _ = w.shape
    eye = jnp.eye(n, dtype=w.dtype)
    return (eye[:, None, :, None] * w[:, :, None, :]).reshape(n * bs, n * bs)


def kernel(x_prompt, x_sample, state_ret, state_hgrn, state_lru_h, state_lru_conv, state_dn,
           state_dn_conv, ffn1_norm, ffn1_w_in, ffn1_w_out, mix_norm, ffn2_norm, ffn2_w_in,
           ffn2_w_out, final_norm, even_w_in, even_w_out, ret_out_norm, hg_out_norm,
           hg_lb_logits, odd_w_in, odd_w_out, lru_conv_w, lru_conv_b, lru_w_a, lru_b_a, lru_w_x,
           lru_b_x, lru_lambda, dn_conv_w, dn_a_log, dn_dt_bias, dn_out_norm):
    depth = ffn1_norm.shape[0]
    bp, tp, _ = x_prompt.shape
    bs, ts, _ = x_sample.shape
    past_len = 2048
    tabs = (_rope_tables(jnp.arange(tp)), _rope_tables(past_len + jnp.arange(ts)))
    xs = [x_prompt, x_sample]
    nb = (bp, bs)
    w1_in, w1_out = ffn1_w_in.astype(BF16), ffn1_w_out.astype(BF16)
    w2_in, w2_out = ffn2_w_in.astype(BF16), ffn2_w_out.astype(BF16)
    outs = {k: ([], []) for k in ("ret", "hg", "lh", "lc", "dn", "dc")}

    for l in range(depth):
        j = l // 2
        last = l == depth - 1
        for g in range(2):
            xs[g] = _ffn(xs[g], ffn1_norm, w1_in, w1_out, l, final_norm, False)
        if l % 2 == 0:
            w_in = even_w_in[j].astype(BF16)
            w_out = even_w_out[j].astype(BF16)
            for g in range(2):
                if g == 0:
                    s_ret = jnp.zeros((bp, HEADS, HD, HD), F32)
                    s_hg = jnp.zeros((bp, HEADS, HD, HD), F32)
                else:
                    s_ret, s_hg = state_ret[j], state_hgrn[j]
                xs[g], n_ret, n_hg = _even_mixer(
                    xs[g], tabs[g][0], tabs[g][1], mix_norm[l], w_in, w_out, ret_out_norm[j],
                    hg_out_norm[j], hg_lb_logits, s_ret, s_hg, j)
                outs["ret"][g].append(n_ret)
                outs["hg"][g].append(n_hg)
        else:
            w_in = jnp.pad(odd_w_in[j], ((0, 0), (0, ODD_IN_PAD - odd_w_in.shape[-1]))).astype(BF16)
            w_out = odd_w_out[j].astype(BF16)
            cw = jnp.concatenate([lru_conv_w[j], dn_conv_w[j]], axis=-1)
            wa = _block_diag(lru_w_a[j]).astype(BF16)
            wx = _block_diag(lru_w_x[j]).astype(BF16)
            dnp = jnp.zeros((2, HD), F32)
            dnp = dnp.at[0, HEADS:2 * HEADS].set(dn_dt_bias[j])
            dnp = dnp.at[1, HEADS:2 * HEADS].set(dn_a_log[j])
            for g in range(2):
                if g == 0:
                    s_lh = jnp.zeros((bp, 1, LRU_WIDTH), F32)
                    s_lc = jnp.zeros((bp, CONV_W - 1, LRU_WIDTH), F32)
                    s_dn = jnp.zeros((bp, HEADS, HD, HD), F32)
                    s_dc = jnp.zeros((bp, CONV_W - 1, 3 * GRP), F32)
                else:
                    s_lh = state_lru_h[j].reshape(bs, 1, LRU_WIDTH)
                    s_lc, s_dn, s_dc = state_lru_conv[j], state_dn[j], state_dn_conv[j]
                xs[g], n_lh, n_lc, n_dn, n_dc = _odd_mixer(
                    xs[g], mix_norm[l], w_in, w_out, cw, lru_conv_b[j].reshape(1, -1), wa,
                    lru_b_a[j].reshape(1, -1), wx, lru_b_x[j].reshape(1, -1),
                    lru_lambda[j].reshape(1, -1), dnp, dn_out_norm[j].reshape(1, -1),
                    s_lh, s_lc, s_dn, s_dc)
                outs["lh"][g].append(n_lh.reshape(nb[g], LRU_WIDTH))
                outs["lc"][g].append(n_lc)
                outs["dn"][g].append(n_dn)
                outs["dc"][g].append(n_dc)
        for g in range(2):
            xs[g] = _ffn(xs[g], ffn2_norm, w2_in, w2_out, l, final_norm, last)

    res = [xs[0], xs[1]]
    for k in ("ret", "hg", "lh", "lc", "dn", "dc"):
        for g in range(2):
            res.append(jnp.stack(outs[k][g]))
    return tuple(res)
```

```python
import functools
import math

import jax
import jax.numpy as jnp
from jax import lax
from jax.experimental import pallas as pl
from jax.experimental.pallas import tpu as pltpu

F32 = jnp.float32
BF16 = jnp.bfloat16

D_MODEL = 1024
FF_DIM = 2816
EPS = 1e-6
CHUNK = 64
CONV_W = 4
HEADS = 4
HD = 128
ROPE_BASE = 10000.0
LRU_WIDTH = 512
LRU_C = 8.0
GRP = HEADS * HD
EVEN_IN = 8 * GRP
ODD_IN_PAD = 6 * GRP + HD
SUB = 8
DN_SUB = 8

VMEM_LIMIT = 56 * 1024 * 1024

FFN_TM = 512
FFN_TF = 256
FFN_CAST_ROWS = 16
SEQ_TILE = 512

RET_LOG_G = [math.log1p(-(2.0 ** (-5.0 - h))) for h in range(HEADS)]


def _rms(x, g):
    return x * lax.rsqrt(jnp.mean(x * x, axis=-1, keepdims=True) + EPS) * g


def _silu(x):
    return x * jax.nn.sigmoid(x)


def _dot(a, b):
    return jnp.dot(a, b, preferred_element_type=F32)


def _dot_nt(a, b):
    return lax.dot_general(a, b, (((1,), (1,)), ((), ())), preferred_element_type=F32)


def _dot_tn(a, b):
    return lax.dot_general(a, b, (((0,), (0,)), ((), ())), preferred_element_type=F32)


def _head_norm(o, g):
    return o * lax.rsqrt(jnp.mean(o * o, axis=-1, keepdims=True) + EPS) * g


def _hcols(h):
    return slice(h * HD, (h + 1) * HD)


def _chunk_cumsum(x):
    pos = lax.broadcasted_iota(jnp.int32, x.shape, 0) & (CHUNK - 1)
    k = 1
    while k < CHUNK:
        x = jnp.where(pos >= k, x + pltpu.roll(x, k, 0), x)
        k *= 2
    return x


def _resident(block_shape, index_map):
    return pl.BlockSpec(block_shape, index_map, pipeline_mode=pl.Buffered(1))


def _ffn_kernel(*refs, n_first, apply_final, cast_next):
    xa_ref, xb_ref, g_ref, win_ref, wout_ref, fin_ref = refs[:6]
    refs = refs[6:]
    if cast_next:
        nwi_ref, nwo_ref, oa_ref, ob_ref, nwi_o, nwo_o, act_ref = refs
        nwi_o[...] = nwi_ref[...].astype(BF16)
        nwo_o[...] = nwo_ref[...].astype(BF16)
    else:
        oa_ref, ob_ref, act_ref = refs
    i = pl.program_id(0)

    def tile(x_ref, o_ref):
        x = x_ref[...]
        h = _rms(x, g_ref[...]).astype(BF16)
        for c in range(FF_DIM // FFN_TF):
            lo = c * FFN_TF
            gate = _dot(h, win_ref[:, lo:lo + FFN_TF])
            up = _dot(h, win_ref[:, FF_DIM + lo:FF_DIM + lo + FFN_TF])
            act_ref[:, lo:lo + FFN_TF] = (_silu(gate) * up).astype(BF16)
        y = x + 0.5 * _dot(act_ref[...], wout_ref[...])
        if apply_final:
            y = _rms(y, fin_ref[...])
        o_ref[...] = y

    @pl.when(i < n_first)
    def _():
        tile(xa_ref, oa_ref)

    @pl.when(i >= n_first)
    def _():
        tile(xb_ref, ob_ref)


def _ffn(xa, xb, norm_g, w_in, w_out, final_g, apply_final, next_w=None):
    sa, sb = xa.shape, xb.shape
    xa2, xb2 = xa.reshape(-1, D_MODEL), xb.reshape(-1, D_MODEL)
    tm = min(FFN_TM, xa2.shape[0], xb2.shape[0])
    na, nb = xa2.shape[0] // tm, xb2.shape[0] // tm
    x_blk = (tm, D_MODEL)
    in_specs = [
        pl.BlockSpec(x_blk, lambda i: (jnp.minimum(i, na - 1), 0)),
        pl.BlockSpec(x_blk, lambda i: (jnp.maximum(i - na, 0), 0)),
        _resident((1, D_MODEL), lambda i: (0, 0)),
        _resident((D_MODEL, 2 * FF_DIM), lambda i: (0, 0)),
        _resident((FF_DIM, D_MODEL), lambda i: (0, 0)),
        _resident((1, D_MODEL), lambda i: (0, 0)),
    ]
    out_specs = [
        pl.BlockSpec(x_blk, lambda i: (jnp.minimum(i, na - 1), 0)),
        pl.BlockSpec(x_blk, lambda i: (jnp.maximum(i - na, 0), 0)),
    ]
    out_shape = [jax.ShapeDtypeStruct(xa2.shape, F32), jax.ShapeDtypeStruct(xb2.shape, F32)]
    args = [xa2, xb2, norm_g.reshape(1, D_MODEL), w_in, w_out, final_g.reshape(1, D_MODEL)]
    if next_w is not None:
        nwi, nwo, layer = next_w
        ncol = na // FFN_CAST_ROWS
        blk_i = (None, D_MODEL // FFN_CAST_ROWS, 2 * FF_DIM // ncol)
        blk_o = (None, FF_DIM // FFN_CAST_ROWS, D_MODEL // ncol)

        def cast_blk(i):
            i = jnp.minimum(i, na - 1)
            return i // ncol, i % ncol

        in_specs += [pl.BlockSpec(blk_i, lambda i: (layer,) + cast_blk(i)),
                     pl.BlockSpec(blk_o, lambda i: (layer,) + cast_blk(i))]
        out_specs += [pl.BlockSpec(blk_i[1:], cast_blk), pl.BlockSpec(blk_o[1:], cast_blk)]
        out_shape += [jax.ShapeDtypeStruct((D_MODEL, 2 * FF_DIM), BF16),
                      jax.ShapeDtypeStruct((FF_DIM, D_MODEL), BF16)]
        args += [nwi, nwo]
    outs = pl.pallas_call(
        functools.partial(_ffn_kernel, n_first=na, apply_final=apply_final,
                          cast_next=next_w is not None),
        grid=(na + nb,),
        in_specs=in_specs,
        out_specs=out_specs,
        out_shape=out_shape,
        scratch_shapes=[pltpu.VMEM((tm, FF_DIM), BF16)],
        compiler_params=pltpu.CompilerParams(
            dimension_semantics=("arbitrary",), vmem_limit_bytes=VMEM_LIMIT),
        name="ffn",
    )(*args)
    return outs[0].reshape(sa), outs[1].reshape(sb), tuple(outs[2:])


def _seq_tiling(b, t):
    if t >= SEQ_TILE:
        return 1, SEQ_TILE // CHUNK
    return min(b, SEQ_TILE // t), t // CHUNK


def _even_kernel(x_ref, cos_ref, sin_ref, ng_ref, win_ref, wout_ref, rg_ref, hg_ref, lbl_ref,
                 sret_ref, shg_ref, xo_ref, sret_o, shg_o,
                 qs_ref, qd_ref, kb_ref, kd_ref, rv_ref, hv_ref,
                 rgate_ref, hq_ref, hf_ref, hgate_ref, mix_ref, hgt_ref, *, ns, nc, layer_j):
    t = pl.program_id(1)
    tt = ns * nc * CHUNK
    heads = range(HEADS)

    @pl.when(t == 0)
    def _():
        sret_o[...] = sret_ref[...]
        for s in range(ns):
            for h in heads:
                hgt_ref[s, h] = shg_ref[s, h].T

    x = x_ref[...].reshape(tt, D_MODEL)
    hn = _rms(x, ng_ref[...]).astype(BF16)

    def proj(g):
        return _dot(hn, win_ref[:, g * GRP:(g + 1) * GRP])

    cos = jnp.concatenate([cos_ref[...]] * ns, axis=0)
    sin = jnp.concatenate([sin_ref[...]] * ns, axis=0)
    pos = (lax.broadcasted_iota(jnp.int32, (tt, HD), 0) & (CHUNK - 1)).astype(F32)
    scale = HD ** -0.5

    def rope(z, h):
        v = z[:, _hcols(h)]
        return v * cos + pltpu.roll(v, HD // 2, 1) * sin

    z0 = proj(0)
    z1 = proj(1)
    for h in heads:
        q = rope(z0, h)
        qs_ref[:, _hcols(h)] = (q * scale).astype(BF16)
        qd_ref[:, _hcols(h)] = (q * (jnp.exp(RET_LOG_G[h] * (pos + 1.0)) * scale)).astype(BF16)
    z2 = proj(2)
    for h in heads:
        k = rope(z1, h)
        kb_ref[:, _hcols(h)] = k.astype(BF16)
        kd_ref[:, _hcols(h)] = (k * jnp.exp(RET_LOG_G[h] * (CHUNK - 1.0 - pos))).astype(BF16)
    z3 = proj(3)
    rv_ref[...] = z2.astype(BF16)
    z4 = proj(4)
    rgate_ref[...] = _silu(z3)
    z5 = proj(5)
    hq_ref[...] = _silu(z4)
    z6 = proj(6)
    hf_ref[...] = z5
    z7 = proj(7)
    hv_ref[...] = z6.astype(BF16)
    hgate_ref[...] = _silu(z7)

    ri = lax.broadcasted_iota(jnp.int32, (CHUNK, CHUNK), 0)
    ci = lax.broadcasted_iota(jnp.int32, (CHUNK, CHUNK), 1)
    absd = jnp.abs(ri - ci).astype(F32)
    intra = [jnp.exp(RET_LOG_G[h] * absd) for h in heads]
    sdec = [math.exp(RET_LOG_G[h] * CHUNK) for h in heads]

    lbl = lbl_ref[...]
    e = jnp.exp(lbl - jnp.max(lbl, axis=0, keepdims=True))
    sm = e / jnp.sum(e, axis=0, keepdims=True)
    lb_all = jnp.sum(sm[:layer_j + 1], axis=0, keepdims=True)

    nb = CHUNK // SUB
    t_io = lax.broadcasted_iota(jnp.int32, (nb, SUB, HD), 1)
    lane_io = lax.broadcasted_iota(jnp.int32, (nb, SUB, CHUNK), 2)
    blk_io = lax.broadcasted_iota(jnp.int32, (nb, SUB, CHUNK), 0)

    def chunk_body(i, carry):
        s = i if nc == 1 else 0
        r0 = pl.multiple_of(i * CHUNK, CHUNK)

        def ld(ref, h):
            return ref[pl.ds(r0, CHUNK), pl.ds(h * HD, HD)]

        ret_states = [sret_o[s, h] for h in heads]
        hg_states = [hgt_ref[s, h] for h in heads]

        r_v, r_att, r_inter = [], [], []
        for h in heads:
            st = ret_states[h]
            r_v.append(ld(rv_ref, h))
            r_att.append(_dot_nt(ld(qs_ref, h), ld(kb_ref, h)))
            r_inter.append(_dot(ld(qd_ref, h), st.astype(BF16)))
            sret_o[s, h] = st * sdec[h] + _dot_tn(ld(kd_ref, h), r_v[h])

        h_q, h_k, h_b, h_v, h_off, h_inter = [], [], [], [], [], []
        for h in heads:
            lb = lb_all[:, _hcols(h)]
            sig = jax.nn.sigmoid(ld(hf_ref, h))
            v = ld(hv_ref, h)
            q = ld(hq_ref, h)
            k = (1.0 - lb) * (1.0 - sig)
            b = _chunk_cumsum(jnp.log(lb + (1.0 - lb) * sig))
            stt = hg_states[h]
            off = []
            for blk in range(1, nb):
                lo = blk * SUB
                rb = b[lo - 1:lo, :]
                qt = (q[lo:lo + SUB] * jnp.exp(b[lo:lo + SUB] - rb)).astype(BF16)
                kt = jnp.concatenate([k[:lo] * jnp.exp(rb - b[:lo]),
                                      jnp.zeros((CHUNK - lo, HD), F32)], axis=0).astype(BF16)
                off.append(_dot_nt(qt, kt))
            h_off.append(off)
            h_inter.append(_dot_nt((q * jnp.exp(b)).astype(BF16), stt.astype(BF16)))
            bend = b[CHUNK - 1:CHUNK, :]
            kd = (k * jnp.exp(bend - b)).astype(BF16)
            hgt_ref[s, h] = stt * jnp.exp(bend) + _dot_tn(v, kd)
            h_q.append(q)
            h_k.append(k)
            h_b.append(b)
            h_v.append(v)

        for h in heads:
            o = _dot((r_att[h] * intra[h]).astype(BF16), r_v[h]) + r_inter[h]
            o = _head_norm(o, rg_ref[:, pl.ds(h * HD, HD)]) * ld(rgate_ref, h)
            mix_ref[pl.ds(r0, CHUNK), pl.ds(h * HD, HD)] = o.astype(BF16)

        h_att = []
        for h in heads:
            b4 = h_b[h].reshape(nb, SUB, HD)
            q4 = h_q[h].reshape(nb, SUB, HD)
            k4 = h_k[h].reshape(nb, SUB, HD)
            diag = jnp.zeros((nb, SUB, CHUNK), F32)
            for sp in range(SUB):
                arg = jnp.where(t_io >= sp, b4 - b4[:, sp:sp + 1, :], -jnp.inf)
                r = jnp.sum(jnp.exp(arg) * q4 * k4[:, sp:sp + 1, :], axis=-1, keepdims=True)
                diag = jnp.where(lane_io == blk_io * SUB + sp, r, diag)
            diag = diag.reshape(CHUNK, CHUNK)
            parts = [diag[0:SUB]]
            for blk in range(1, nb):
                parts.append(diag[blk * SUB:(blk + 1) * SUB] + h_off[h][blk - 1])
            h_att.append(jnp.concatenate(parts, axis=0).astype(BF16))
        for h in heads:
            o = _dot(h_att[h], h_v[h]) + h_inter[h]
            o = _head_norm(o, hg_ref[:, pl.ds(h * HD, HD)]) * ld(hgate_ref, h)
            mix_ref[pl.ds(r0, CHUNK), pl.ds(GRP + h * HD, HD)] = o.astype(BF16)
        return carry

    lax.fori_loop(0, ns * nc, chunk_body, 0)

    xo_ref[...] = (x + _dot(mix_ref[...], wout_ref[...])).reshape(xo_ref.shape)

    @pl.when(t == pl.num_programs(1) - 1)
    def _():
        for s in range(ns):
            for h in heads:
                shg_o[s, h] = hgt_ref[s, h].T


def _even_mixer(x, cos, sin, norm_g, w_in, w_out, ret_g, hg_g, lb_logits, s_ret, s_hg, layer_j):
    b, t, _ = x.shape
    ns, nc = _seq_tiling(b, t)
    tt = ns * nc * CHUNK
    grid = (b // ns, t // (nc * CHUNK))
    st_spec = pl.BlockSpec((ns, HEADS, HD, HD), lambda i, j: (i, 0, 0, 0))
    x_spec = pl.BlockSpec((ns, nc * CHUNK, D_MODEL), lambda i, j: (i, j, 0))

    def full(a):
        return _resident(a.shape, lambda i, j: (0,) * a.ndim)

    tab_spec = pl.BlockSpec((nc * CHUNK, HD), lambda i, j: (j, 0))
    args = (x, cos, sin, norm_g.reshape(1, D_MODEL), w_in, w_out, ret_g.reshape(1, GRP),
            hg_g.reshape(1, GRP), lb_logits, s_ret, s_hg)
    in_specs = [x_spec, tab_spec, tab_spec] + [full(a) for a in args[3:9]] + [st_spec, st_spec]
    grp_bf16 = pltpu.VMEM((tt, GRP), BF16)
    grp_f32 = pltpu.VMEM((tt, GRP), F32)
    return pl.pallas_call(
        functools.partial(_even_kernel, ns=ns, nc=nc, layer_j=layer_j),
        grid=grid,
        in_specs=in_specs,
        out_specs=[x_spec, st_spec, st_spec],
        out_shape=[jax.ShapeDtypeStruct(x.shape, F32),
                   jax.ShapeDtypeStruct(s_ret.shape, F32),
                   jax.ShapeDtypeStruct(s_hg.shape, F32)],
        scratch_shapes=[grp_bf16] * 6 + [grp_f32] * 4
        + [pltpu.VMEM((tt, D_MODEL), BF16), pltpu.VMEM((ns, HEADS, HD, HD), F32)],
        compiler_params=pltpu.CompilerParams(
            dimension_semantics=("parallel", "arbitrary"), vmem_limit_bytes=VMEM_LIMIT),
        name="even_mixer",
    )(*args)


def _softplus(x):
    return jnp.maximum(x, 0.0) + jnp.log1p(jnp.exp(-jnp.abs(x)))


def _gelu_tanh(x):
    return 0.5 * x * (1.0 + jnp.tanh(math.sqrt(2.0 / math.pi) * (x + 0.044715 * (x * x * x))))


def _odd_kernel(x_ref, ng_ref, win_ref, wout_ref, cw_ref, lcb_ref, wa_ref, ba_ref, wx_ref,
                bx_ref, lam_ref, dnp_ref, dg_ref, lh_ref, lc_ref, ds_ref, dc_ref,
                xo_ref, lh_o, lc_o, ds_o, dc_o,
                z_ref, gl_ref, sg_ref, bt_ref, gt_ref, gtt_ref, mix_ref, xp_ref, *, ns, nc):
    t = pl.program_id(1)
    tt = ns * nc * CHUNK
    conv_ch = LRU_WIDTH + 3 * GRP
    hist = CONV_W - 1
    top = 8
    heads = range(HEADS)

    @pl.when(t == 0)
    def _():
        lh_o[...] = lh_ref[...]
        lc_o[...] = lc_ref[...]
        ds_o[...] = ds_ref[...]
        dc_o[...] = dc_ref[...]

    x = x_ref[...].reshape(tt, D_MODEL)
    hn = _rms(x, ng_ref[...]).astype(BF16)

    def proj(lo, width=GRP):
        return _dot(hn, win_ref[:, lo:lo + width])

    zs = proj(6 * GRP, HD)
    z5 = proj(5 * GRP)
    bt_ref[...] = jax.nn.sigmoid(zs)
    g_all = _chunk_cumsum(-jnp.exp(dnp_ref[1:2, :]) * _softplus(zs + dnp_ref[0:1, :]))
    gt_ref[...] = g_all
    for c in range(ns * nc):
        gtt_ref[c] = g_all[c * CHUNK:(c + 1) * CHUNK, :].T
    z1 = proj(GRP)
    sg_ref[...] = _silu(z5)
    z_ref[:, 0:GRP] = proj(0)
    gl_ref[...] = _gelu_tanh(z1)
    for g in range(3):
        z_ref[:, (g + 1) * GRP:(g + 2) * GRP] = proj((g + 2) * GRP)

    sp_lam = _softplus(-lam_ref[...])
    ri = lax.broadcasted_iota(jnp.int32, (CHUNK, CHUNK), 0)
    ci = lax.broadcasted_iota(jnp.int32, (CHUNK, CHUNK), 1)
    rows_w = lax.broadcasted_iota(jnp.int32, (CHUNK, LRU_WIDTH), 0)
    eye_c = (ri == ci).astype(F32)
    sub_bits = DN_SUB.bit_length() - 1
    same_blk = (ri >> sub_bits) == (ci >> sub_bits)
    merge_masks = []
    for lvl in range(sub_bits, CHUNK.bit_length() - 1):
        merge_masks.append(((ri >> (lvl + 1)) == (ci >> (lvl + 1)))
                           & (((ri >> lvl) & 1) == 1) & (((ci >> lvl) & 1) == 0))

    def chunk_body(i, carry):
        s = i if nc == 1 else 0
        r0 = pl.multiple_of(i * CHUNK, CHUNK)
        rows = pl.ds(r0, CHUNK)

        xp_ref[top - hist:top, 0:LRU_WIDTH] = lc_o[s]
        xp_ref[top - hist:top, LRU_WIDTH:conv_ch] = dc_o[s]
        xp_ref[top:top + CHUNK, :] = z_ref[rows, :]
        lc_o[s] = xp_ref[top + CHUNK - hist:top + CHUNK, 0:LRU_WIDTH]
        dc_o[s] = xp_ref[top + CHUNK - hist:top + CHUNK, LRU_WIDTH:conv_ch]
        y = xp_ref[top - hist:top - hist + CHUNK, :] * cw_ref[0:1, :]
        for jj in range(1, CONV_W):
            y = y + xp_ref[top - hist + jj:top - hist + jj + CHUNK, :] * cw_ref[jj:jj + 1, :]

        def rg_lru_branch():
            lx = y[:, 0:LRU_WIDTH] + lcb_ref[...]
            xb = lx.astype(BF16)
            r = jax.nn.sigmoid(_dot(xb, wa_ref[...]) + ba_ref[...])
            ig = jax.nn.sigmoid(_dot(xb, wx_ref[...]) + bx_ref[...])
            a = jnp.exp(-LRU_C * r * sp_lam)
            u = jnp.sqrt(1.0 - a * a) * (ig * lx)
            k = 1
            while k < CHUNK:
                m = rows_w >= k
                u = jnp.where(m, a * pltpu.roll(u, k, 0) + u, u)
                a = jnp.where(m, a * pltpu.roll(a, k, 0), a)
                k *= 2
            hseq = a * lh_o[s] + u
            lh_o[s] = hseq[CHUNK - 1:CHUNK]
            mix_ref[rows, 0:LRU_WIDTH] = (gl_ref[rows, :] * hseq).astype(BF16)

        beta_t = bt_ref[rows, :]
        g_t = gt_ref[rows, :]
        g_tt = gtt_ref[i]
        states = [ds_o[s, h] for h in heads]
        kn, both_k, both_s = [], [], []
        for h in heads:
            base = LRU_WIDTH + h * HD
            q = _silu(y[:, base:base + HD])
            kk_ = _silu(y[:, base + GRP:base + GRP + HD])
            q = q * lax.rsqrt(jnp.sum(q * q, axis=-1, keepdims=True) + EPS) * (HD ** -0.5)
            kn.append(kk_ * lax.rsqrt(jnp.sum(kk_ * kk_, axis=-1, keepdims=True) + EPS))
            knb = kn[h].astype(BF16)
            qk_in = jnp.concatenate([q.astype(BF16), knb], axis=0)
            both_k.append(_dot_nt(qk_in, knb))
            both_s.append(_dot(qk_in, states[h].astype(BF16)))
        gcol, eg, e_incl, amat, rhs = [], [], [], [], []
        for h in heads:
            v = _silu(y[:, LRU_WIDTH + 2 * GRP + h * HD:LRU_WIDTH + 2 * GRP + (h + 1) * HD])
            gcol.append(g_t[:, HEADS + h:HEADS + h + 1])
            rel = gcol[h] - g_tt[HEADS + h:HEADS + h + 1, :]
            e_strict = jnp.exp(jnp.where(ri > ci, rel, -jnp.inf))
            e_incl.append(jnp.where(ri == ci, 1.0, e_strict))
            beta = beta_t[:, h:h + 1]
            eg.append(jnp.exp(gcol[h]))
            amat.append(beta * both_k[h][CHUNK:] * e_strict)
            rhs.append(beta * (v - eg[h] * both_s[h][CHUNK:]))
        dblk = [jnp.where(same_blk, amat[h], 0.0) for h in heads]
        tinv = [eye_c for _ in heads]
        for jj in range(DN_SUB - 1):
            for h in heads:
                colv = jnp.concatenate(
                    [dblk[h][b0:b0 + DN_SUB, b0 + jj:b0 + jj + 1]
                     for b0 in range(0, CHUNK, DN_SUB)], axis=0)
                rowm = jnp.concatenate(
                    [jnp.broadcast_to(tinv[h][b0 + jj:b0 + jj + 1, :], (DN_SUB, CHUNK))
                     for b0 in range(0, CHUNK, DN_SUB)], axis=0)
                tinv[h] = tinv[h] - colv * rowm
        rg_lru_branch()
        tb = [tinv[h].astype(BF16) for h in heads]
        pend = [[_dot(tb[h], jnp.where(lm, amat[h], 0.0).astype(BF16)) for lm in merge_masks]
                for h in heads]
        for _ in merge_masks:
            xb_ = [pend[h][0].astype(BF16) for h in heads]
            tb = [tinv[h].astype(BF16) for h in heads]
            tinv = [tinv[h] - _dot(xb_[h], tb[h]) for h in heads]
            pend = [[p - _dot(xb_[h], p.astype(BF16)) for p in pend[h][1:]] for h in heads]
        w = [_dot(tinv[h].astype(BF16), rhs[h].astype(BF16)).astype(BF16) for h in heads]
        for h in heads:
            o = eg[h] * both_s[h][:CHUNK] + _dot((both_k[h][:CHUNK] * e_incl[h]).astype(BF16), w[h])
            gend = gcol[h][CHUNK - 1:CHUNK, :]
            kd = (kn[h] * jnp.exp(gend - gcol[h])).astype(BF16)
            ds_o[s, h] = states[h] * jnp.exp(gend) + _dot_tn(kd, w[h])
            o = _head_norm(o, dg_ref[:, pl.ds(h * HD, HD)]) * sg_ref[rows, pl.ds(h * HD, HD)]
            mix_ref[rows, pl.ds(LRU_WIDTH + h * HD, HD)] = o.astype(BF16)
        return carry

    lax.fori_loop(0, ns * nc, chunk_body, 0, unroll=2)

    xo_ref[...] = (x + _dot(mix_ref[...], wout_ref[...])).reshape(xo_ref.shape)


def _odd_mixer(x, norm_g, w_in, w_out, cw, lcb, wa, ba, wx, bx, lam, dnp, dn_g,
               s_lh, s_lc, s_dn, s_dc):
    b, t, _ = x.shape
    ns, nc = _seq_tiling(b, t)
    tt = ns * nc * CHUNK
    grid = (b // ns, t // (nc * CHUNK))
    x_spec = pl.BlockSpec((ns, nc * CHUNK, D_MODEL), lambda i, j: (i, j, 0))

    def full(a):
        return _resident(a.shape, lambda i, j: (0,) * a.ndim)

    def st_spec(a):
        return pl.BlockSpec((ns,) + a.shape[1:], lambda i, j: (i,) + (0,) * (a.ndim - 1))

    params = (norm_g.reshape(1, D_MODEL), w_in, w_out, cw, lcb, wa, ba, wx, bx, lam, dnp, dn_g)
    states = (s_lh, s_lc, s_dn, s_dc)
    grp_f32 = pltpu.VMEM((tt, GRP), F32)
    head_f32 = pltpu.VMEM((tt, HD), F32)
    return pl.pallas_call(
        functools.partial(_odd_kernel, ns=ns, nc=nc),
        grid=grid,
        in_specs=[x_spec] + [full(a) for a in params] + [st_spec(a) for a in states],
        out_specs=[x_spec] + [st_spec(a) for a in states],
        out_shape=[jax.ShapeDtypeStruct(x.shape, F32)]
        + [jax.ShapeDtypeStruct(a.shape, F32) for a in states],
        scratch_shapes=[pltpu.VMEM((tt, LRU_WIDTH + 3 * GRP), F32), grp_f32, grp_f32,
                        head_f32, head_f32, pltpu.VMEM((ns * nc, HD, CHUNK), F32),
                        pltpu.VMEM((tt, D_MODEL), BF16),
                        pltpu.VMEM((8 + CHUNK, LRU_WIDTH + 3 * GRP), F32)],
        compiler_params=pltpu.CompilerParams(
            dimension_semantics=("parallel", "arbitrary"), vmem_limit_bytes=VMEM_LIMIT),
        name="odd_mixer",
    )(x, *params, *states)


def _rope_tables(pos):
    half = HD // 2
    freq = ROPE_BASE ** (-jnp.arange(half, dtype=F32) / half)
    ang = pos.astype(F32)[:, None] * freq[None, :]
    cos, sin = jnp.cos(ang), jnp.sin(ang)
    return jnp.concatenate([cos, cos], axis=-1), jnp.concatenate([-sin, sin], axis=-1)


def _block_diag(w):
    n, bs, _ = w.shape
    eye = jnp.eye(n, dtype=w.dtype)
    return (eye[:, None, :, None] * w[:, :, None, :]).reshape(n * bs, n * bs)


def kernel(x_prompt, x_sample, state_ret, state_hgrn, state_lru_h, state_lru_conv, state_dn,
           state_dn_conv, ffn1_norm, ffn1_w_in, ffn1_w_out, mix_norm, ffn2_norm, ffn2_w_in,
           ffn2_w_out, final_norm, even_w_in, even_w_out, ret_out_norm, hg_out_norm,
           hg_lb_logits, odd_w_in, odd_w_out, lru_conv_w, lru_conv_b, lru_w_a, lru_b_a, lru_w_x,
           lru_b_x, lru_lambda, dn_conv_w, dn_a_log, dn_dt_bias, dn_out_norm):
    depth = ffn1_norm.shape[0]
    bp, tp, _ = x_prompt.shape
    bs, ts, _ = x_sample.shape
    past_len = 2048
    tabs = (_rope_tables(jnp.arange(tp)), _rope_tables(past_len + jnp.arange(ts)))
    xs = [x_prompt, x_sample]
    nb = (bp, bs)
    outs = {k: ([], []) for k in ("ret", "hg", "lh", "lc", "dn", "dc")}

    ffn_sets = []
    for l in range(depth):
        ffn_sets += [(ffn1_norm[l], ffn1_w_in, ffn1_w_out, l), (ffn2_norm[l], ffn2_w_in, ffn2_w_out, l)]
    ffn_w = [(ffn1_w_in[0].astype(BF16), ffn1_w_out[0].astype(BF16))]

    def run_ffn(apply_final):
        k = len(ffn_w) - 1
        nxt = ffn_sets[k + 1][1:] if k + 1 < len(ffn_sets) else None
        xs[0], xs[1], w_next = _ffn(xs[0], xs[1], ffn_sets[k][0], ffn_w[k][0], ffn_w[k][1],
                                    final_norm, apply_final, nxt)
        ffn_w.append(w_next)

    for l in range(depth):
        j = l // 2
        last = l == depth - 1
        run_ffn(False)
        if l % 2 == 0:
            w_in = even_w_in[j].astype(BF16)
            w_out = even_w_out[j].astype(BF16)
            for g in range(2):
                if g == 0:
                    s_ret = jnp.zeros((bp, HEADS, HD, HD), F32)
                    s_hg = jnp.zeros((bp, HEADS, HD, HD), F32)
                else:
                    s_ret, s_hg = state_ret[j], state_hgrn[j]
                xs[g], n_ret, n_hg = _even_mixer(
                    xs[g], tabs[g][0], tabs[g][1], mix_norm[l], w_in, w_out, ret_out_norm[j],
                    hg_out_norm[j], hg_lb_logits, s_ret, s_hg, j)
                outs["ret"][g].append(n_ret)
                outs["hg"][g].append(n_hg)
        else:
            w_in = jnp.pad(odd_w_in[j], ((0, 0), (0, ODD_IN_PAD - odd_w_in.shape[-1]))).astype(BF16)
            w_out = odd_w_out[j].astype(BF16)
            cw = jnp.concatenate([lru_conv_w[j], dn_conv_w[j]], axis=-1)
            wa = _block_diag(lru_w_a[j]).astype(BF16)
            wx = _block_diag(lru_w_x[j]).astype(BF16)
            dnp = jnp.zeros((2, HD), F32)
            dnp = dnp.at[0, HEADS:2 * HEADS].set(dn_dt_bias[j])
            dnp = dnp.at[1, HEADS:2 * HEADS].set(dn_a_log[j])
            for g in range(2):
                if g == 0:
                    s_lh = jnp.zeros((bp, 1, LRU_WIDTH), F32)
                    s_lc = jnp.zeros((bp, CONV_W - 1, LRU_WIDTH), F32)
                    s_dn = jnp.zeros((bp, HEADS, HD, HD), F32)
                    s_dc = jnp.zeros((bp, CONV_W - 1, 3 * GRP), F32)
                else:
                    s_lh = state_lru_h[j].reshape(bs, 1, LRU_WIDTH)
                    s_lc, s_dn, s_dc = state_lru_conv[j], state_dn[j], state_dn_conv[j]
                xs[g], n_lh, n_lc, n_dn, n_dc = _odd_mixer(
                    xs[g], mix_norm[l], w_in, w_out, cw, lru_conv_b[j].reshape(1, -1), wa,
                    lru_b_a[j].reshape(1, -1), wx, lru_b_x[j].reshape(1, -1),
                    lru_lambda[j].reshape(1, -1), dnp, dn_out_norm[j].reshape(1, -1),
                    s_lh, s_lc, s_dn, s_dc)
                outs["lh"][g].append(n_lh.reshape(nb[g], LRU_WIDTH))
                outs["lc"][g].append(n_lc)
                outs["dn"][g].append(n_dn)
                outs["dc"][g].append(n_dc)
        run_ffn(last)

    res = [xs[0], xs[1]]
    for k in ("ret", "hg", "lh", "lc", "dn", "dc"):
        for g in range(2):
            res.append(jnp.stack(outs[k][g]))
    return tuple(res)
```

```python
import functools
import math

import jax
import jax.numpy as jnp
from jax import lax
from jax.experimental import pallas as pl
from jax.experimental.pallas import tpu as pltpu

F32 = jnp.float32
BF16 = jnp.bfloat16

D_MODEL = 1024
FF_DIM = 2816
EPS = 1e-6
CHUNK = 64
CONV_W = 4
HEADS = 4
HD = 128
ROPE_BASE = 10000.0
LRU_WIDTH = 512
LRU_C = 8.0
GRP = HEADS * HD
EVEN_IN = 8 * GRP
ODD_IN_PAD = 6 * GRP + HD
SUB = 8
DN_SUB = 8

VMEM_LIMIT = 56 * 1024 * 1024

FFN_TM = 512
FFN_TF = 256
FFN_CAST_ROWS = 16
SEQ_TILE = 512
MIXER_LANES = 2
ODD_LANES = 1
ODD_UNROLL = 2

RET_LOG_G = [math.log1p(-(2.0 ** (-5.0 - h))) for h in range(HEADS)]


def _rms(x, g):
    return x * lax.rsqrt(jnp.mean(x * x, axis=-1, keepdims=True) + EPS) * g


def _silu(x):
    return x * jax.nn.sigmoid(x)


def _dot(a, b):
    return jnp.dot(a, b, preferred_element_type=F32)


def _dot_nt(a, b):
    return lax.dot_general(a, b, (((1,), (1,)), ((), ())), preferred_element_type=F32)


def _dot_tn(a, b):
    return lax.dot_general(a, b, (((0,), (0,)), ((), ())), preferred_element_type=F32)


def _head_norm(o, g):
    return o * lax.rsqrt(jnp.mean(o * o, axis=-1, keepdims=True) + EPS) * g


def _hcols(h):
    return slice(h * HD, (h + 1) * HD)


def _chunk_cumsum(x):
    pos = lax.broadcasted_iota(jnp.int32, x.shape, 0) & (CHUNK - 1)
    k = 1
    while k < CHUNK:
        x = jnp.where(pos >= k, x + pltpu.roll(x, k, 0), x)
        k *= 2
    return x


def _resident(block_shape, index_map):
    return pl.BlockSpec(block_shape, index_map, pipeline_mode=pl.Buffered(1))


def _ffn_kernel(*refs, n_first, apply_final, n_casts):
    xa_ref, xb_ref, g_ref, win_ref, wout_ref, fin_ref = refs[:6]
    cast_in = refs[6:6 + n_casts]
    oa_ref, ob_ref = refs[6 + n_casts:8 + n_casts]
    cast_out = refs[8 + n_casts:8 + 2 * n_casts]
    act_ref = refs[8 + 2 * n_casts]
    for src, dst in zip(cast_in, cast_out):
        w = src.shape[-1]
        dst[:, 0:w] = src[...].astype(BF16)
        if dst.shape[-1] > w:
            dst[:, w:] = jnp.zeros((dst.shape[0], dst.shape[-1] - w), BF16)
    i = pl.program_id(0)

    def tile(x_ref, o_ref):
        x = x_ref[...]
        h = _rms(x, g_ref[...]).astype(BF16)
        for c in range(FF_DIM // FFN_TF):
            lo = c * FFN_TF
            gate = _dot(h, win_ref[:, lo:lo + FFN_TF])
            up = _dot(h, win_ref[:, FF_DIM + lo:FF_DIM + lo + FFN_TF])
            act_ref[:, lo:lo + FFN_TF] = (_silu(gate) * up).astype(BF16)
        y = x + 0.5 * _dot(act_ref[...], wout_ref[...])
        if apply_final:
            y = _rms(y, fin_ref[...])
        o_ref[...] = y

    @pl.when(i < n_first)
    def _():
        tile(xa_ref, oa_ref)

    @pl.when(i >= n_first)
    def _():
        tile(xb_ref, ob_ref)


def _ffn(xa, xb, norm_g, w_in, w_out, final_g, apply_final, casts=()):
    sa, sb = xa.shape, xb.shape
    xa2, xb2 = xa.reshape(-1, D_MODEL), xb.reshape(-1, D_MODEL)
    tm = min(FFN_TM, xa2.shape[0], xb2.shape[0])
    na, nb = xa2.shape[0] // tm, xb2.shape[0] // tm
    x_blk = (tm, D_MODEL)
    in_specs = [
        pl.BlockSpec(x_blk, lambda i: (jnp.minimum(i, na - 1), 0)),
        pl.BlockSpec(x_blk, lambda i: (jnp.maximum(i - na, 0), 0)),
        _resident((1, D_MODEL), lambda i: (0, 0)),
        _resident((D_MODEL, 2 * FF_DIM), lambda i: (0, 0)),
        _resident((FF_DIM, D_MODEL), lambda i: (0, 0)),
        _resident((1, D_MODEL), lambda i: (0, 0)),
    ]
    out_specs = [
        pl.BlockSpec(x_blk, lambda i: (jnp.minimum(i, na - 1), 0)),
        pl.BlockSpec(x_blk, lambda i: (jnp.maximum(i - na, 0), 0)),
    ]
    out_shape = [jax.ShapeDtypeStruct(xa2.shape, F32), jax.ShapeDtypeStruct(xb2.shape, F32)]
    args = [xa2, xb2, norm_g.reshape(1, D_MODEL), w_in, w_out, final_g.reshape(1, D_MODEL)]
    cast_in_specs, cast_out_specs = [], []
    for arr, layer, row_blocks, c_pad in casts:
        _, r, c = arr.shape
        ncol = na // (row_blocks or na)
        rows = r // (na // ncol)

        def blk(i, layer=layer, ncol=ncol):
            i = jnp.minimum(i, na - 1)
            return layer, i // ncol, i % ncol

        cast_in_specs.append(pl.BlockSpec((None, rows, c // ncol), blk))
        cast_out_specs.append(pl.BlockSpec((rows, c_pad // ncol), lambda i, blk=blk: blk(i)[1:]))
        out_shape.append(jax.ShapeDtypeStruct((r, c_pad), BF16))
        args.append(arr)
    outs = pl.pallas_call(
        functools.partial(_ffn_kernel, n_first=na, apply_final=apply_final,
                          n_casts=len(casts)),
        grid=(na + nb,),
        in_specs=in_specs + cast_in_specs,
        out_specs=out_specs + cast_out_specs,
        out_shape=out_shape,
        scratch_shapes=[pltpu.VMEM((tm, FF_DIM), BF16)],
        compiler_params=pltpu.CompilerParams(
            dimension_semantics=("arbitrary",), vmem_limit_bytes=VMEM_LIMIT),
        name="ffn",
    )(*args)
    return outs[0].reshape(sa), outs[1].reshape(sb), list(outs[2:])


def _seq_tiling(b, t, lanes=1):
    if t >= SEQ_TILE:
        lanes = lanes if b % lanes == 0 else 1
        return lanes, SEQ_TILE // CHUNK // lanes
    return min(b, SEQ_TILE // t), t // CHUNK


def _even_kernel(x_ref, cos_ref, sin_ref, ng_ref, win_ref, wout_ref, rg_ref, hg_ref, lbl_ref,
                 sret_ref, shg_ref, xo_ref, sret_o, shg_o,
                 qs_ref, qd_ref, kb_ref, kd_ref, rv_ref, hv_ref,
                 rgate_ref, hq_ref, hf_ref, hgate_ref, mix_ref, hgt_ref, *, ns, nc, layer_j):
    t = pl.program_id(1)
    tt = ns * nc * CHUNK
    heads = range(HEADS)

    @pl.when(t == 0)
    def _():
        sret_o[...] = sret_ref[...]
        for s in range(ns):
            for h in heads:
                hgt_ref[s, h] = shg_ref[s, h].T

    x = x_ref[...].reshape(tt, D_MODEL)
    hn = _rms(x, ng_ref[...]).astype(BF16)

    def proj(g):
        return _dot(hn, win_ref[:, g * GRP:(g + 1) * GRP])

    cos = jnp.concatenate([cos_ref[...]] * ns, axis=0)
    sin = jnp.concatenate([sin_ref[...]] * ns, axis=0)
    pos = (lax.broadcasted_iota(jnp.int32, (tt, HD), 0) & (CHUNK - 1)).astype(F32)
    scale = HD ** -0.5

    def rope(z, h):
        v = z[:, _hcols(h)]
        return v * cos + pltpu.roll(v, HD // 2, 1) * sin

    z0 = proj(0)
    z1 = proj(1)
    for h in heads:
        q = rope(z0, h)
        qs_ref[:, _hcols(h)] = (q * scale).astype(BF16)
        qd_ref[:, _hcols(h)] = (q * (jnp.exp(RET_LOG_G[h] * (pos + 1.0)) * scale)).astype(BF16)
    z2 = proj(2)
    for h in heads:
        k = rope(z1, h)
        kb_ref[:, _hcols(h)] = k.astype(BF16)
        kd_ref[:, _hcols(h)] = (k * jnp.exp(RET_LOG_G[h] * (CHUNK - 1.0 - pos))).astype(BF16)
    z3 = proj(3)
    rv_ref[...] = z2.astype(BF16)
    z4 = proj(4)
    rgate_ref[...] = _silu(z3)
    z5 = proj(5)
    hq_ref[...] = _silu(z4)
    z6 = proj(6)
    hf_ref[...] = z5
    z7 = proj(7)
    hv_ref[...] = z6.astype(BF16)
    hgate_ref[...] = _silu(z7)

    ri = lax.broadcasted_iota(jnp.int32, (CHUNK, CHUNK), 0)
    ci = lax.broadcasted_iota(jnp.int32, (CHUNK, CHUNK), 1)
    absd = jnp.abs(ri - ci).astype(F32)
    intra = [jnp.exp(RET_LOG_G[h] * absd) for h in heads]
    sdec = [math.exp(RET_LOG_G[h] * CHUNK) for h in heads]

    lbl = lbl_ref[...]
    e = jnp.exp(lbl - jnp.max(lbl, axis=0, keepdims=True))
    sm = e / jnp.sum(e, axis=0, keepdims=True)
    lb_all = jnp.sum(sm[:layer_j + 1], axis=0, keepdims=True)

    nb = CHUNK // SUB
    t_io = lax.broadcasted_iota(jnp.int32, (nb, SUB, HD), 1)
    lane_io = lax.broadcasted_iota(jnp.int32, (nb, SUB, CHUNK), 2)
    blk_io = lax.broadcasted_iota(jnp.int32, (nb, SUB, CHUNK), 0)

    def chunk_body(i, carry):
        streams = []
        for u in range(lanes):
            s, c = (i * lanes + u, 0) if nc == 1 else (u, i)
            r0 = pl.multiple_of((s * nc + c) * CHUNK, CHUNK)
            streams += [(s, r0, h) for h in heads]

        def ld(ref, r0, h):
            return ref[pl.ds(r0, CHUNK), pl.ds(h * HD, HD)]

        ret_states = [sret_o[s, h] for s, _, h in streams]
        hg_states = [hgt_ref[s, h] for s, _, h in streams]

        r_v, r_att, r_inter = [], [], []
        for n, (s, r0, h) in enumerate(streams):
            st = ret_states[n]
            r_v.append(ld(rv_ref, r0, h))
            r_att.append(_dot_nt(ld(qs_ref, r0, h), ld(kb_ref, r0, h)))
            r_inter.append(_dot(ld(qd_ref, r0, h), st.astype(BF16)))
            sret_o[s, h] = st * sdec[h] + _dot_tn(ld(kd_ref, r0, h), r_v[n])

        h_q, h_k, h_b, h_v, h_off, h_inter = [], [], [], [], [], []
        for n, (s, r0, h) in enumerate(streams):
            lb = lb_all[:, _hcols(h)]
            sig = jax.nn.sigmoid(ld(hf_ref, r0, h))
            v = ld(hv_ref, r0, h)
            q = ld(hq_ref, r0, h)
            k = (1.0 - lb) * (1.0 - sig)
            b = _chunk_cumsum(jnp.log(lb + (1.0 - lb) * sig))
            stt = hg_states[n]
            off = []
            for blk in range(1, nb):
                lo = blk * SUB
                rb = b[lo - 1:lo, :]
                qt = (q[lo:lo + SUB] * jnp.exp(b[lo:lo + SUB] - rb)).astype(BF16)
                kt = jnp.concatenate([k[:lo] * jnp.exp(rb - b[:lo]),
                                      jnp.zeros((CHUNK - lo, HD), F32)], axis=0).astype(BF16)
                off.append(_dot_nt(qt, kt))
            h_off.append(off)
            h_inter.append(_dot_nt((q * jnp.exp(b)).astype(BF16), stt.astype(BF16)))
            bend = b[CHUNK - 1:CHUNK, :]
            kd = (k * jnp.exp(bend - b)).astype(BF16)
            hgt_ref[s, h] = stt * jnp.exp(bend) + _dot_tn(v, kd)
            h_q.append(q)
            h_k.append(k)
            h_b.append(b)
            h_v.append(v)

        for n, (s, r0, h) in enumerate(streams):
            o = _dot((r_att[n] * intra[h]).astype(BF16), r_v[n]) + r_inter[n]
            o = _head_norm(o, rg_ref[:, pl.ds(h * HD, HD)]) * ld(rgate_ref, r0, h)
            mix_ref[pl.ds(r0, CHUNK), pl.ds(h * HD, HD)] = o.astype(BF16)

        h_att = []
        for n in range(len(streams)):
            b4 = h_b[n].reshape(nb, SUB, HD)
            q4 = h_q[n].reshape(nb, SUB, HD)
            k4 = h_k[n].reshape(nb, SUB, HD)
            diag = jnp.zeros((nb, SUB, CHUNK), F32)
            for sp in range(SUB):
                arg = jnp.where(t_io >= sp, b4 - b4[:, sp:sp + 1, :], -jnp.inf)
                r = jnp.sum(jnp.exp(arg) * q4 * k4[:, sp:sp + 1, :], axis=-1, keepdims=True)
                diag = jnp.where(lane_io == blk_io * SUB + sp, r, diag)
            diag = diag.reshape(CHUNK, CHUNK)
            parts = [diag[0:SUB]]
            for blk in range(1, nb):
                parts.append(diag[blk * SUB:(blk + 1) * SUB] + h_off[n][blk - 1])
            h_att.append(jnp.concatenate(parts, axis=0).astype(BF16))
        for n, (s, r0, h) in enumerate(streams):
            o = _dot(h_att[n], h_v[n]) + h_inter[n]
            o = _head_norm(o, hg_ref[:, pl.ds(h * HD, HD)]) * ld(hgate_ref, r0, h)
            mix_ref[pl.ds(r0, CHUNK), pl.ds(GRP + h * HD, HD)] = o.astype(BF16)
        return carry

    lanes = MIXER_LANES if ns % MIXER_LANES == 0 else 1
    lax.fori_loop(0, ns * nc // lanes, chunk_body, 0)

    xo_ref[...] = (x + _dot(mix_ref[...], wout_ref[...])).reshape(xo_ref.shape)

    @pl.when(t == pl.num_programs(1) - 1)
    def _():
        for s in range(ns):
            for h in heads:
                shg_o[s, h] = hgt_ref[s, h].T


def _even_mixer(x, cos, sin, norm_g, w_in, w_out, ret_g, hg_g, lb_logits, s_ret, s_hg, layer_j):
    b, t, _ = x.shape
    ns, nc = _seq_tiling(b, t, MIXER_LANES)
    tt = ns * nc * CHUNK
    grid = (b // ns, t // (nc * CHUNK))
    st_spec = pl.BlockSpec((ns, HEADS, HD, HD), lambda i, j: (i, 0, 0, 0))
    x_spec = pl.BlockSpec((ns, nc * CHUNK, D_MODEL), lambda i, j: (i, j, 0))

    def full(a):
        return _resident(a.shape, lambda i, j: (0,) * a.ndim)

    tab_spec = pl.BlockSpec((nc * CHUNK, HD), lambda i, j: (j, 0))
    args = (x, cos, sin, norm_g.reshape(1, D_MODEL), w_in, w_out, ret_g.reshape(1, GRP),
            hg_g.reshape(1, GRP), lb_logits, s_ret, s_hg)
    in_specs = [x_spec, tab_spec, tab_spec] + [full(a) for a in args[3:9]] + [st_spec, st_spec]
    grp_bf16 = pltpu.VMEM((tt, GRP), BF16)
    grp_f32 = pltpu.VMEM((tt, GRP), F32)
    return pl.pallas_call(
        functools.partial(_even_kernel, ns=ns, nc=nc, layer_j=layer_j),
        grid=grid,
        in_specs=in_specs,
        out_specs=[x_spec, st_spec, st_spec],
        out_shape=[jax.ShapeDtypeStruct(x.shape, F32),
                   jax.ShapeDtypeStruct(s_ret.shape, F32),
                   jax.ShapeDtypeStruct(s_hg.shape, F32)],
        scratch_shapes=[grp_bf16] * 6 + [grp_f32] * 4
        + [pltpu.VMEM((tt, D_MODEL), BF16), pltpu.VMEM((ns, HEADS, HD, HD), F32)],
        compiler_params=pltpu.CompilerParams(
            dimension_semantics=("parallel", "arbitrary"), vmem_limit_bytes=VMEM_LIMIT),
        name="even_mixer",
    )(*args)


def _softplus(x):
    return jnp.maximum(x, 0.0) + jnp.log1p(jnp.exp(-jnp.abs(x)))


def _gelu_tanh(x):
    return 0.5 * x * (1.0 + jnp.tanh(math.sqrt(2.0 / math.pi) * (x + 0.044715 * (x * x * x))))


def _odd_kernel(x_ref, ng_ref, win_ref, wout_ref, cw_ref, lcb_ref, wa_ref, ba_ref, wx_ref,
                bx_ref, lam_ref, dnp_ref, dg_ref, lh_ref, lc_ref, ds_ref, dc_ref,
                xo_ref, lh_o, lc_o, ds_o, dc_o,
                z_ref, gl_ref, sg_ref, bt_ref, gt_ref, gtt_ref, mix_ref, xp_ref, *, ns, nc):
    t = pl.program_id(1)
    tt = ns * nc * CHUNK
    conv_ch = LRU_WIDTH + 3 * GRP
    hist = CONV_W - 1
    top = 8
    heads = range(HEADS)

    @pl.when(t == 0)
    def _():
        lh_o[...] = lh_ref[...]
        lc_o[...] = lc_ref[...]
        ds_o[...] = ds_ref[...]
        dc_o[...] = dc_ref[...]

    x = x_ref[...].reshape(tt, D_MODEL)
    hn = _rms(x, ng_ref[...]).astype(BF16)

    def proj(lo, width=GRP):
        return _dot(hn, win_ref[:, lo:lo + width])

    zs = proj(6 * GRP, HD)
    z5 = proj(5 * GRP)
    bt_ref[...] = jax.nn.sigmoid(zs)
    g_all = _chunk_cumsum(-jnp.exp(dnp_ref[1:2, :]) * _softplus(zs + dnp_ref[0:1, :]))
    gt_ref[...] = g_all
    for c in range(ns * nc):
        gtt_ref[c] = g_all[c * CHUNK:(c + 1) * CHUNK, :].T
    z1 = proj(GRP)
    sg_ref[...] = _silu(z5)
    z_ref[:, 0:GRP] = proj(0)
    gl_ref[...] = _gelu_tanh(z1)
    for g in range(3):
        z_ref[:, (g + 1) * GRP:(g + 2) * GRP] = proj((g + 2) * GRP)

    sp_lam = _softplus(-lam_ref[...])
    ri = lax.broadcasted_iota(jnp.int32, (CHUNK, CHUNK), 0)
    ci = lax.broadcasted_iota(jnp.int32, (CHUNK, CHUNK), 1)
    rows_w = lax.broadcasted_iota(jnp.int32, (CHUNK, LRU_WIDTH), 0)
    eye_c = (ri == ci).astype(F32)
    sub_bits = DN_SUB.bit_length() - 1
    same_blk = (ri >> sub_bits) == (ci >> sub_bits)
    merge_masks = []
    for lvl in range(sub_bits, CHUNK.bit_length() - 1):
        merge_masks.append(((ri >> (lvl + 1)) == (ci >> (lvl + 1)))
                           & (((ri >> lvl) & 1) == 1) & (((ci >> lvl) & 1) == 0))

    def chunk_body(i, carry):
        lane_seq, lane_rows, lane_chunk, ys = [], [], [], []
        for u in range(lanes):
            s, c = (i * lanes + u, 0) if nc == 1 else (u, i)
            r0 = pl.multiple_of((s * nc + c) * CHUNK, CHUNK)
            rows = pl.ds(r0, CHUNK)
            lane_seq.append(s)
            lane_rows.append(rows)
            lane_chunk.append(s * nc + c)

            xp_ref[u, top - hist:top, 0:LRU_WIDTH] = lc_o[s]
            xp_ref[u, top - hist:top, LRU_WIDTH:conv_ch] = dc_o[s]
            xp_ref[u, top:top + CHUNK, :] = z_ref[rows, :]
            lc_o[s] = xp_ref[u, top + CHUNK - hist:top + CHUNK, 0:LRU_WIDTH]
            dc_o[s] = xp_ref[u, top + CHUNK - hist:top + CHUNK, LRU_WIDTH:conv_ch]
            y = xp_ref[u, top - hist:top - hist + CHUNK, :] * cw_ref[0:1, :]
            for jj in range(1, CONV_W):
                y = y + (xp_ref[u, top - hist + jj:top - hist + jj + CHUNK, :]
                         * cw_ref[jj:jj + 1, :])
            ys.append(y)

        def rg_lru_branch(u):
            y, s, rows = ys[u], lane_seq[u], lane_rows[u]
            lx = y[:, 0:LRU_WIDTH] + lcb_ref[...]
            xb = lx.astype(BF16)
            r = jax.nn.sigmoid(_dot(xb, wa_ref[...]) + ba_ref[...])
            ig = jax.nn.sigmoid(_dot(xb, wx_ref[...]) + bx_ref[...])
            a = jnp.exp(-LRU_C * r * sp_lam)
            u = jnp.sqrt(1.0 - a * a) * (ig * lx)
            k = 1
            while k < CHUNK:
                m = rows_w >= k
                u = jnp.where(m, a * pltpu.roll(u, k, 0) + u, u)
                a = jnp.where(m, a * pltpu.roll(a, k, 0), a)
                k *= 2
            hseq = a * lh_o[s] + u
            lh_o[s] = hseq[CHUNK - 1:CHUNK]
            mix_ref[rows, 0:LRU_WIDTH] = (gl_ref[rows, :] * hseq).astype(BF16)

        streams = [(u, h) for u in range(lanes) for h in heads]
        ns_ = range(len(streams))
        beta_t = [bt_ref[rows, :] for rows in lane_rows]
        g_t = [gt_ref[rows, :] for rows in lane_rows]
        g_tt = [gtt_ref[c] for c in lane_chunk]
        states = [ds_o[lane_seq[u], h] for u, h in streams]
        kn, both_k, both_s = [], [], []
        for n, (u, h) in enumerate(streams):
            y = ys[u]
            base = LRU_WIDTH + h * HD
            q = _silu(y[:, base:base + HD])
            kk_ = _silu(y[:, base + GRP:base + GRP + HD])
            q = q * lax.rsqrt(jnp.sum(q * q, axis=-1, keepdims=True) + EPS) * (HD ** -0.5)
            kn.append(kk_ * lax.rsqrt(jnp.sum(kk_ * kk_, axis=-1, keepdims=True) + EPS))
            knb = kn[n].astype(BF16)
            qk_in = jnp.concatenate([q.astype(BF16), knb], axis=0)
            both_k.append(_dot_nt(qk_in, knb))
            both_s.append(_dot(qk_in, states[n].astype(BF16)))
        gcol, eg, e_incl, amat, rhs = [], [], [], [], []
        for n, (u, h) in enumerate(streams):
            y = ys[u]
            v = _silu(y[:, LRU_WIDTH + 2 * GRP + h * HD:LRU_WIDTH + 2 * GRP + (h + 1) * HD])
            gcol.append(g_t[u][:, HEADS + h:HEADS + h + 1])
            rel = gcol[n] - g_tt[u][HEADS + h:HEADS + h + 1, :]
            e_strict = jnp.exp(jnp.where(ri > ci, rel, -jnp.inf))
            e_incl.append(jnp.where(ri == ci, 1.0, e_strict))
            beta = beta_t[u][:, h:h + 1]
            eg.append(jnp.exp(gcol[n]))
            amat.append(beta * both_k[n][CHUNK:] * e_strict)
            rhs.append(beta * (v - eg[n] * both_s[n][CHUNK:]))
        dblk = [jnp.where(same_blk, amat[n], 0.0) for n in ns_]
        tinv = [eye_c for _ in ns_]
        for jj in range(DN_SUB - 1):
            for n in ns_:
                colv = jnp.concatenate(
                    [dblk[n][b0:b0 + DN_SUB, b0 + jj:b0 + jj + 1]
                     for b0 in range(0, CHUNK, DN_SUB)], axis=0)
                rowm = jnp.concatenate(
                    [jnp.broadcast_to(tinv[n][b0 + jj:b0 + jj + 1, :], (DN_SUB, CHUNK))
                     for b0 in range(0, CHUNK, DN_SUB)], axis=0)
                tinv[n] = tinv[n] - colv * rowm
        for u in range(lanes):
            rg_lru_branch(u)
        tb = [tinv[n].astype(BF16) for n in ns_]
        pend = [[_dot(tb[n], jnp.where(lm, amat[n], 0.0).astype(BF16)) for lm in merge_masks]
                for n in ns_]
        for _ in merge_masks:
            xb_ = [pend[n][0].astype(BF16) for n in ns_]
            tb = [tinv[n].astype(BF16) for n in ns_]
            tinv = [tinv[n] - _dot(xb_[n], tb[n]) for n in ns_]
            pend = [[p - _dot(xb_[n], p.astype(BF16)) for p in pend[n][1:]] for n in ns_]
        w = [_dot(tinv[n].astype(BF16), rhs[n].astype(BF16)).astype(BF16) for n in ns_]
        for n, (u, h) in enumerate(streams):
            s, rows = lane_seq[u], lane_rows[u]
            o = eg[n] * both_s[n][:CHUNK] + _dot((both_k[n][:CHUNK] * e_incl[n]).astype(BF16), w[n])
            gend = gcol[n][CHUNK - 1:CHUNK, :]
            kd = (kn[n] * jnp.exp(gend - gcol[n])).astype(BF16)
            ds_o[s, h] = states[n] * jnp.exp(gend) + _dot_tn(kd, w[n])
            o = _head_norm(o, dg_ref[:, pl.ds(h * HD, HD)]) * sg_ref[rows, pl.ds(h * HD, HD)]
            mix_ref[rows, pl.ds(LRU_WIDTH + h * HD, HD)] = o.astype(BF16)
        return carry

    lanes = ODD_LANES if ns % ODD_LANES == 0 else 1
    lax.fori_loop(0, ns * nc // lanes, chunk_body, 0, unroll=ODD_UNROLL)

    xo_ref[...] = (x + _dot(mix_ref[...], wout_ref[...])).reshape(xo_ref.shape)


def _odd_mixer(x, norm_g, w_in, w_out, cw, lcb, wa, ba, wx, bx, lam, dnp, dn_g,
               s_lh, s_lc, s_dn, s_dc):
    b, t, _ = x.shape
    ns, nc = _seq_tiling(b, t, ODD_LANES)
    tt = ns * nc * CHUNK
    grid = (b // ns, t // (nc * CHUNK))
    x_spec = pl.BlockSpec((ns, nc * CHUNK, D_MODEL), lambda i, j: (i, j, 0))

    def full(a):
        return _resident(a.shape, lambda i, j: (0,) * a.ndim)

    def st_spec(a):
        return pl.BlockSpec((ns,) + a.shape[1:], lambda i, j: (i,) + (0,) * (a.ndim - 1))

    params = (norm_g.reshape(1, D_MODEL), w_in, w_out, cw, lcb, wa, ba, wx, bx, lam, dnp, dn_g)
    states = (s_lh, s_lc, s_dn, s_dc)
    grp_f32 = pltpu.VMEM((tt, GRP), F32)
    head_f32 = pltpu.VMEM((tt, HD), F32)
    return pl.pallas_call(
        functools.partial(_odd_kernel, ns=ns, nc=nc),
        grid=grid,
        in_specs=[x_spec] + [full(a) for a in params] + [st_spec(a) for a in states],
        out_specs=[x_spec] + [st_spec(a) for a in states],
        out_shape=[jax.ShapeDtypeStruct(x.shape, F32)]
        + [jax.ShapeDtypeStruct(a.shape, F32) for a in states],
        scratch_shapes=[pltpu.VMEM((tt, LRU_WIDTH + 3 * GRP), F32), grp_f32, grp_f32,
                        head_f32, head_f32, pltpu.VMEM((ns * nc, HD, CHUNK), F32),
                        pltpu.VMEM((tt, D_MODEL), BF16),
                        pltpu.VMEM((ODD_LANES, 8 + CHUNK, LRU_WIDTH + 3 * GRP), F32)],
        compiler_params=pltpu.CompilerParams(
            dimension_semantics=("parallel", "arbitrary"), vmem_limit_bytes=VMEM_LIMIT),
        name="odd_mixer",
    )(x, *params, *states)


def _rope_tables(pos):
    half = HD // 2
    freq = ROPE_BASE ** (-jnp.arange(half, dtype=F32) / half)
    ang = pos.astype(F32)[:, None] * freq[None, :]
    cos, sin = jnp.cos(ang), jnp.sin(ang)
    return jnp.concatenate([cos, cos], axis=-1), jnp.concatenate([-sin, sin], axis=-1)


def _block_diag(w):
    n, bs, _ = w.shape
    eye = jnp.eye(n, dtype=w.dtype)
    return (eye[:, None, :, None] * w[:, :, None, :]).reshape(n * bs, n * bs)


def kernel(x_prompt, x_sample, state_ret, state_hgrn, state_lru_h, state_lru_conv, state_dn,
           state_dn_conv, ffn1_norm, ffn1_w_in, ffn1_w_out, mix_norm, ffn2_norm, ffn2_w_in,
           ffn2_w_out, final_norm, even_w_in, even_w_out, ret_out_norm, hg_out_norm,
           hg_lb_logits, odd_w_in, odd_w_out, lru_conv_w, lru_conv_b, lru_w_a, lru_b_a, lru_w_x,
           lru_b_x, lru_lambda, dn_conv_w, dn_a_log, dn_dt_bias, dn_out_norm):
    depth = ffn1_norm.shape[0]
    bp, tp, _ = x_prompt.shape
    bs, ts, _ = x_sample.shape
    past_len = 2048
    tabs = (_rope_tables(jnp.arange(tp)), _rope_tables(past_len + jnp.arange(ts)))
    xs = [x_prompt, x_sample]
    nb = (bp, bs)
    outs = {k: ([], []) for k in ("ret", "hg", "lh", "lc", "dn", "dc")}

    ffn_sets = []
    for l in range(depth):
        ffn_sets += [(ffn1_norm[l], ffn1_w_in, ffn1_w_out, l),
                     (ffn2_norm[l], ffn2_w_in, ffn2_w_out, l)]
    ffn_w = [(ffn1_w_in[0].astype(BF16), ffn1_w_out[0].astype(BF16))]

    def run_ffn(apply_final, mixer_casts=()):
        k = len(ffn_w) - 1
        casts = []
        if k + 1 < len(ffn_sets):
            _, nwi, nwo, nl = ffn_sets[k + 1]
            casts = [(nwi, nl, FFN_CAST_ROWS, nwi.shape[-1]),
                     (nwo, nl, FFN_CAST_ROWS, nwo.shape[-1])]
        casts += list(mixer_casts)
        xs[0], xs[1], cast = _ffn(xs[0], xs[1], ffn_sets[k][0], ffn_w[k][0], ffn_w[k][1],
                                  final_norm, apply_final, casts)
        ffn_w.append(tuple(cast[:2]) if k + 1 < len(ffn_sets) else ())
        return cast[len(cast) - len(mixer_casts):]

    for l in range(depth):
        j = l // 2
        last = l == depth - 1
        if l % 2 == 0:
            w_in, w_out = run_ffn(False, [(even_w_in, j, None, EVEN_IN),
                                          (even_w_out, j, None, D_MODEL)])
            for g in range(2):
                if g == 0:
                    s_ret = jnp.zeros((bp, HEADS, HD, HD), F32)
                    s_hg = jnp.zeros((bp, HEADS, HD, HD), F32)
                else:
                    s_ret, s_hg = state_ret[j], state_hgrn[j]
                xs[g], n_ret, n_hg = _even_mixer(
                    xs[g], tabs[g][0], tabs[g][1], mix_norm[l], w_in, w_out, ret_out_norm[j],
                    hg_out_norm[j], hg_lb_logits, s_ret, s_hg, j)
                outs["ret"][g].append(n_ret)
                outs["hg"][g].append(n_hg)
        else:
            w_in, w_out = run_ffn(False, [(odd_w_in, j, None, ODD_IN_PAD),
                                          (odd_w_out, j, None, D_MODEL)])
            cw = jnp.concatenate([lru_conv_w[j], dn_conv_w[j]], axis=-1)
            wa = _block_diag(lru_w_a[j]).astype(BF16)
            wx = _block_diag(lru_w_x[j]).astype(BF16)
            dnp = jnp.zeros((2, HD), F32)
            dnp = dnp.at[0, HEADS:2 * HEADS].set(dn_dt_bias[j])
            dnp = dnp.at[1, HEADS:2 * HEADS].set(dn_a_log[j])
            for g in range(2):
                if g == 0:
                    s_lh = jnp.zeros((bp, 1, LRU_WIDTH), F32)
                    s_lc = jnp.zeros((bp, CONV_W - 1, LRU_WIDTH), F32)
                    s_dn = jnp.zeros((bp, HEADS, HD, HD), F32)
                    s_dc = jnp.zeros((bp, CONV_W - 1, 3 * GRP), F32)
                else:
                    s_lh = state_lru_h[j].reshape(bs, 1, LRU_WIDTH)
                    s_lc, s_dn, s_dc = state_lru_conv[j], state_dn[j], state_dn_conv[j]
                xs[g], n_lh, n_lc, n_dn, n_dc = _odd_mixer(
                    xs[g], mix_norm[l], w_in, w_out, cw, lru_conv_b[j].reshape(1, -1), wa,
                    lru_b_a[j].reshape(1, -1), wx, lru_b_x[j].reshape(1, -1),
                    lru_lambda[j].reshape(1, -1), dnp, dn_out_norm[j].reshape(1, -1),
                    s_lh, s_lc, s_dn, s_dc)
                outs["lh"][g].append(n_lh.reshape(nb[g], LRU_WIDTH))
                outs["lc"][g].append(n_lc)
                outs["dn"][g].append(n_dn)
                outs["dc"][g].append(n_dc)
        run_ffn(last)

    res = [xs[0], xs[1]]
    for k in ("ret", "hg", "lh", "lc", "dn", "dc"):
        for g in range(2):
            res.append(jnp.stack(outs[k][g]))
    return tuple(res)
```

```python
import functools
import math

import jax
import jax.numpy as jnp
from jax import lax
from jax.experimental import pallas as pl
from jax.experimental.pallas import tpu as pltpu

F32 = jnp.float32
BF16 = jnp.bfloat16

D_MODEL = 1024
FF_DIM = 2816
EPS = 1e-6
CHUNK = 64
CONV_W = 4
HEADS = 4
HD = 128
ROPE_BASE = 10000.0
LRU_WIDTH = 512
LRU_C = 8.0
GRP = HEADS * HD
EVEN_IN = 8 * GRP
SUB = 8
DN_SUB = 8

VMEM_LIMIT = 56 * 1024 * 1024

FFN_TM = 512
FFN_TF = 256
FFN_CAST_ROWS = 16
SEQ_TILE = 512
MIXER_LANES = 2
ODD_LANES = 1
ODD_UNROLL = 8
EVEN_UNROLL = 4

RET_LOG_G = [math.log1p(-(2.0 ** (-5.0 - h))) for h in range(HEADS)]


def _rms(x, g):
    return x * lax.rsqrt(jnp.mean(x * x, axis=-1, keepdims=True) + EPS) * g


def _silu(x):
    return x * jax.nn.sigmoid(x)


def _dot(a, b):
    return jnp.dot(a, b, preferred_element_type=F32)


def _dot_nt(a, b):
    return lax.dot_general(a, b, (((1,), (1,)), ((), ())), preferred_element_type=F32)


def _dot_tn(a, b):
    return lax.dot_general(a, b, (((0,), (0,)), ((), ())), preferred_element_type=F32)


def _head_norm(o, g):
    return o * lax.rsqrt(jnp.mean(o * o, axis=-1, keepdims=True) + EPS) * g


def _hcols(h):
    return slice(h * HD, (h + 1) * HD)


def _chunk_cumsum(x):
    pos = lax.broadcasted_iota(jnp.int32, x.shape, 0) & (CHUNK - 1)
    k = 1
    while k < CHUNK:
        x = jnp.where(pos >= k, x + pltpu.roll(x, k, 0), x)
        k *= 2
    return x


def _resident(block_shape, index_map):
    return pl.BlockSpec(block_shape, index_map, pipeline_mode=pl.Buffered(1))


def _ffn_kernel(*refs, n_first, apply_final, n_casts):
    xa_ref, xb_ref, g_ref, win_ref, wout_ref, fin_ref = refs[:6]
    cast_in = refs[6:6 + n_casts]
    oa_ref, ob_ref = refs[6 + n_casts:8 + n_casts]
    cast_out = refs[8 + n_casts:8 + 2 * n_casts]
    act_ref = refs[8 + 2 * n_casts]
    for src, dst in zip(cast_in, cast_out):
        dst[...] = src[...].astype(BF16)
    i = pl.program_id(0)

    def tile(x_ref, o_ref):
        x = x_ref[...]
        h = _rms(x, g_ref[...]).astype(BF16)

        for c in range(FF_DIM // FFN_TF):
            lo = c * FFN_TF
            gate = _dot(h, win_ref[:, lo:lo + FFN_TF])
            up = _dot(h, win_ref[:, FF_DIM + lo:FF_DIM + lo + FFN_TF])
            act_ref[:, lo:lo + FFN_TF] = (_silu(gate) * up).astype(BF16)
        y = x + 0.5 * _dot(act_ref[...], wout_ref[...])
        if apply_final:
            y = _rms(y, fin_ref[...])
        o_ref[...] = y

    @pl.when(i < n_first)
    def _():
        tile(xa_ref, oa_ref)

    @pl.when(i >= n_first)
    def _():
        tile(xb_ref, ob_ref)


def _ffn(xa, xb, norm_g, w_in, w_out, final_g, apply_final, casts=()):
    sa, sb = xa.shape, xb.shape
    xa2, xb2 = xa.reshape(-1, D_MODEL), xb.reshape(-1, D_MODEL)
    tm = min(FFN_TM, xa2.shape[0], xb2.shape[0])
    na, nb = xa2.shape[0] // tm, xb2.shape[0] // tm
    x_blk = (tm, D_MODEL)
    in_specs = [
        pl.BlockSpec(x_blk, lambda i: (jnp.minimum(i, na - 1), 0)),
        pl.BlockSpec(x_blk, lambda i: (jnp.maximum(i - na, 0), 0)),
        _resident((1, D_MODEL), lambda i: (0, 0)),
        _resident((D_MODEL, 2 * FF_DIM), lambda i: (0, 0)),
        _resident((FF_DIM, D_MODEL), lambda i: (0, 0)),
        _resident((1, D_MODEL), lambda i: (0, 0)),
    ]
    out_specs = [
        pl.BlockSpec(x_blk, lambda i: (jnp.minimum(i, na - 1), 0)),
        pl.BlockSpec(x_blk, lambda i: (jnp.maximum(i - na, 0), 0)),
    ]
    out_shape = [jax.ShapeDtypeStruct(xa2.shape, F32), jax.ShapeDtypeStruct(xb2.shape, F32)]
    args = [xa2, xb2, norm_g.reshape(1, D_MODEL), w_in, w_out, final_g.reshape(1, D_MODEL)]
    cast_in_specs, cast_out_specs = [], []
    for arr, layer, row_blocks, c in casts:
        r = arr.shape[1]
        ncol = na // (row_blocks or na)
        rows = r // (na // ncol)

        def blk(i, layer=layer, ncol=ncol):
            i = jnp.minimum(i, na - 1)
            return layer, i // ncol, i % ncol

        cast_in_specs.append(pl.BlockSpec((None, rows, c // ncol), blk))
        cast_out_specs.append(pl.BlockSpec((rows, c // ncol), lambda i, blk=blk: blk(i)[1:]))
        out_shape.append(jax.ShapeDtypeStruct((r, c), BF16))
        args.append(arr)
    outs = pl.pallas_call(
        functools.partial(_ffn_kernel, n_first=na, apply_final=apply_final,
                          n_casts=len(casts)),
        grid=(na + nb,),
        in_specs=in_specs + cast_in_specs,
        out_specs=out_specs + cast_out_specs,
        out_shape=out_shape,
        scratch_shapes=[pltpu.VMEM((tm, FF_DIM), BF16)],
        compiler_params=pltpu.CompilerParams(
            dimension_semantics=("arbitrary",), vmem_limit_bytes=VMEM_LIMIT),
        name="ffn",
    )(*args)
    return outs[0].reshape(sa), outs[1].reshape(sb), list(outs[2:])


def _seq_tiling(b, t, lanes=1):
    if t >= SEQ_TILE:
        lanes = lanes if b % lanes == 0 else 1
        return lanes, SEQ_TILE // CHUNK // lanes
    return min(b, SEQ_TILE // t), t // CHUNK


def _even_kernel(x_ref, cos_ref, sin_ref, ng_ref, win_ref, wout_ref, rg_ref, hg_ref, lbl_ref,
                 sret_ref, shg_ref, xo_ref, sret_o, shg_o,
                 qs_ref, qd_ref, kb_ref, kd_ref, rv_ref, hv_ref,
                 rgate_ref, hq_ref, hf_ref, hgate_ref, mix_ref, hgt_ref, *, ns, nc, layer_j):
    t = pl.program_id(1)
    tt = ns * nc * CHUNK
    heads = range(HEADS)

    @pl.when(t == 0)
    def _():
        sret_o[...] = sret_ref[...]
        for s in range(ns):
            for h in heads:
                hgt_ref[s, h] = shg_ref[s, h].T

    x = x_ref[...].reshape(tt, D_MODEL)
    hn = _rms(x, ng_ref[...]).astype(BF16)

    def proj(g):
        return _dot(hn, win_ref[:, g * GRP:(g + 1) * GRP])

    cos = jnp.concatenate([cos_ref[...]] * ns, axis=0)
    sin = jnp.concatenate([sin_ref[...]] * ns, axis=0)
    pos = (lax.broadcasted_iota(jnp.int32, (tt, HD), 0) & (CHUNK - 1)).astype(F32)
    scale = HD ** -0.5

    def rope(z, h):
        v = z[:, _hcols(h)]
        return v * cos + pltpu.roll(v, HD // 2, 1) * sin

    z0 = proj(0)
    z1 = proj(1)
    for h in heads:
        q = rope(z0, h)
        qs_ref[:, _hcols(h)] = (q * scale).astype(BF16)
        qd_ref[:, _hcols(h)] = (q * (jnp.exp(RET_LOG_G[h] * (pos + 1.0)) * scale)).astype(BF16)
    z2 = proj(2)
    for h in heads:
        k = rope(z1, h)
        kb_ref[:, _hcols(h)] = k.astype(BF16)
        kd_ref[:, _hcols(h)] = (k * jnp.exp(RET_LOG_G[h] * (CHUNK - 1.0 - pos))).astype(BF16)
    z3 = proj(3)
    rv_ref[...] = z2.astype(BF16)
    z4 = proj(4)
    rgate_ref[...] = _silu(z3)
    z5 = proj(5)
    hq_ref[...] = _silu(z4)
    z6 = proj(6)
    hf_ref[...] = z5
    z7 = proj(7)
    hv_ref[...] = z6.astype(BF16)
    hgate_ref[...] = _silu(z7)

    ri = lax.broadcasted_iota(jnp.int32, (CHUNK, CHUNK), 0)
    ci = lax.broadcasted_iota(jnp.int32, (CHUNK, CHUNK), 1)
    absd = jnp.abs(ri - ci).astype(F32)
    intra = [jnp.exp(RET_LOG_G[h] * absd) for h in heads]
    sdec = [math.exp(RET_LOG_G[h] * CHUNK) for h in heads]

    lbl = lbl_ref[...]
    e = jnp.exp(lbl - jnp.max(lbl, axis=0, keepdims=True))
    sm = e / jnp.sum(e, axis=0, keepdims=True)
    lb_all = jnp.sum(sm[:layer_j + 1], axis=0, keepdims=True)

    nb = CHUNK // SUB
    t_io = lax.broadcasted_iota(jnp.int32, (nb, SUB, HD), 1)
    lane_io = lax.broadcasted_iota(jnp.int32, (nb, SUB, CHUNK), 2)
    blk_io = lax.broadcasted_iota(jnp.int32, (nb, SUB, CHUNK), 0)

    def chunk_body(i, carry):
        streams = []
        for u in range(lanes):
            s, c = (i * lanes + u, 0) if nc == 1 else (u, i)
            r0 = pl.multiple_of((s * nc + c) * CHUNK, CHUNK)
            streams += [(s, r0, h) for h in heads]

        def ld(ref, r0, h):
            return ref[pl.ds(r0, CHUNK), pl.ds(h * HD, HD)]

        ret_states = [sret_o[s, h] for s, _, h in streams]
        hg_states = [hgt_ref[s, h] for s, _, h in streams]

        r_v, r_att, r_inter = [], [], []
        for n, (s, r0, h) in enumerate(streams):
            st = ret_states[n]
            r_v.append(ld(rv_ref, r0, h))
            r_att.append(_dot_nt(ld(qs_ref, r0, h), ld(kb_ref, r0, h)))
            r_inter.append(_dot(ld(qd_ref, r0, h), st.astype(BF16)))
            sret_o[s, h] = st * sdec[h] + _dot_tn(ld(kd_ref, r0, h), r_v[n])

        h_q, h_k, h_b, h_v, h_off, h_inter = [], [], [], [], [], []
        for n, (s, r0, h) in enumerate(streams):
            lb = lb_all[:, _hcols(h)]
            sig = jax.nn.sigmoid(ld(hf_ref, r0, h))
            v = ld(hv_ref, r0, h)
            q = ld(hq_ref, r0, h)
            k = (1.0 - lb) * (1.0 - sig)
            b = _chunk_cumsum(jnp.log(lb + (1.0 - lb) * sig))
            stt = hg_states[n]
            off = []
            for blk in range(1, nb):
                lo = blk * SUB
                rb = b[lo - 1:lo, :]
                qt = (q[lo:lo + SUB] * jnp.exp(b[lo:lo + SUB] - rb)).astype(BF16)
                kt = jnp.concatenate([k[:lo] * jnp.exp(rb - b[:lo]),
                                      jnp.zeros((CHUNK - lo, HD), F32)], axis=0).astype(BF16)
                off.append(_dot_nt(qt, kt))
            h_off.append(off)
            h_inter.append(_dot_nt((q * jnp.exp(b)).astype(BF16), stt.astype(BF16)))
            bend = b[CHUNK - 1:CHUNK, :]
            kd = (k * jnp.exp(bend - b)).astype(BF16)
            hgt_ref[s, h] = stt * jnp.exp(bend) + _dot_tn(v, kd)
            h_q.append(q)
            h_k.append(k)
            h_b.append(b)
            h_v.append(v)

        for n, (s, r0, h) in enumerate(streams):
            o = _dot((r_att[n] * intra[h]).astype(BF16), r_v[n]) + r_inter[n]
            o = _head_norm(o, rg_ref[:, pl.ds(h * HD, HD)]) * ld(rgate_ref, r0, h)
            mix_ref[pl.ds(r0, CHUNK), pl.ds(h * HD, HD)] = o.astype(BF16)

        h_att = []
        for n in range(len(streams)):
            b4 = h_b[n].reshape(nb, SUB, HD)
            q4 = h_q[n].reshape(nb, SUB, HD)
            k4 = h_k[n].reshape(nb, SUB, HD)
            diag = jnp.zeros((nb, SUB, CHUNK), F32)
            for sp in range(SUB):
                arg = jnp.where(t_io >= sp, b4 - b4[:, sp:sp + 1, :], -jnp.inf)
                r = jnp.sum(jnp.exp(arg) * q4 * k4[:, sp:sp + 1, :], axis=-1, keepdims=True)
                diag = jnp.where(lane_io == blk_io * SUB + sp, r, diag)
            diag = diag.reshape(CHUNK, CHUNK)
            parts = [diag[0:SUB]]
            for blk in range(1, nb):
                parts.append(diag[blk * SUB:(blk + 1) * SUB] + h_off[n][blk - 1])
            h_att.append(jnp.concatenate(parts, axis=0).astype(BF16))
        for n, (s, r0, h) in enumerate(streams):
            o = _dot(h_att[n], h_v[n]) + h_inter[n]
            o = _head_norm(o, hg_ref[:, pl.ds(h * HD, HD)]) * ld(hgate_ref, r0, h)
            mix_ref[pl.ds(r0, CHUNK), pl.ds(GRP + h * HD, HD)] = o.astype(BF16)
        return carry

    lanes = MIXER_LANES if ns % MIXER_LANES == 0 else 1
    lax.fori_loop(0, ns * nc // lanes, chunk_body, 0, unroll=EVEN_UNROLL)

    xo_ref[...] = (x + _dot(mix_ref[...], wout_ref[...])).reshape(xo_ref.shape)

    @pl.when(t == pl.num_programs(1) - 1)
    def _():
        for s in range(ns):
            for h in heads:
                shg_o[s, h] = hgt_ref[s, h].T


def _even_mixer(x, cos, sin, norm_g, w_in, w_out, ret_g, hg_g, lb_logits, s_ret, s_hg, layer_j):
    b, t, _ = x.shape
    ns, nc = _seq_tiling(b, t, MIXER_LANES)
    tt = ns * nc * CHUNK
    grid = (b // ns, t // (nc * CHUNK))
    st_spec = pl.BlockSpec((ns, HEADS, HD, HD), lambda i, j: (i, 0, 0, 0))
    x_spec = pl.BlockSpec((ns, nc * CHUNK, D_MODEL), lambda i, j: (i, j, 0))

    def full(a):
        return _resident(a.shape, lambda i, j: (0,) * a.ndim)

    tab_spec = pl.BlockSpec((nc * CHUNK, HD), lambda i, j: (j, 0))
    args = (x, cos, sin, norm_g.reshape(1, D_MODEL), w_in, w_out, ret_g.reshape(1, GRP),
            hg_g.reshape(1, GRP), lb_logits, s_ret, s_hg)
    in_specs = [x_spec, tab_spec, tab_spec] + [full(a) for a in args[3:9]] + [st_spec, st_spec]
    grp_bf16 = pltpu.VMEM((tt, GRP), BF16)
    grp_f32 = pltpu.VMEM((tt, GRP), F32)
    return pl.pallas_call(
        functools.partial(_even_kernel, ns=ns, nc=nc, layer_j=layer_j),
        grid=grid,
        in_specs=in_specs,
        out_specs=[x_spec, st_spec, st_spec],
        out_shape=[jax.ShapeDtypeStruct(x.shape, F32),
                   jax.ShapeDtypeStruct(s_ret.shape, F32),
                   jax.ShapeDtypeStruct(s_hg.shape, F32)],
        scratch_shapes=[grp_bf16] * 6 + [grp_f32] * 4
        + [pltpu.VMEM((tt, D_MODEL), BF16), pltpu.VMEM((ns, HEADS, HD, HD), F32)],
        compiler_params=pltpu.CompilerParams(
            dimension_semantics=("parallel", "arbitrary"), vmem_limit_bytes=VMEM_LIMIT),
        name="even_mixer",
    )(*args)


def _softplus(x):
    return jnp.maximum(x, 0.0) + jnp.log1p(jnp.exp(-jnp.abs(x)))


def _gelu_tanh(x):
    return 0.5 * x * (1.0 + jnp.tanh(math.sqrt(2.0 / math.pi) * (x + 0.044715 * (x * x * x))))


def _odd_kernel(x_ref, ng_ref, win_ref, wt_ref, wout_ref, cw_ref, lcb_ref, wa_ref, ba_ref,
                wx_ref, bx_ref, lam_ref, dnp_ref, dg_ref, lh_ref, lc_ref, ds_ref, dc_ref,
                xo_ref, lh_o, lc_o, ds_o, dc_o,
                z_ref, gl_ref, sg_ref, bt_ref, gt_ref, gtt_ref, mix_ref, xp_ref, *, ns, nc):
    t = pl.program_id(1)
    tt = ns * nc * CHUNK
    conv_ch = LRU_WIDTH + 3 * GRP
    hist = CONV_W - 1
    top = 8
    heads = range(HEADS)

    @pl.when(t == 0)
    def _():
        lh_o[...] = lh_ref[...]
        lc_o[...] = lc_ref[...]
        ds_o[...] = ds_ref[...]
        dc_o[...] = dc_ref[...]

    x = x_ref[...].reshape(tt, D_MODEL)
    hn = _rms(x, ng_ref[...]).astype(BF16)

    def proj(lo):
        return _dot(hn, win_ref[:, lo:lo + GRP])

    zs = _dot(hn, wt_ref[...])
    z5 = proj(5 * GRP)
    bt_ref[...] = jax.nn.sigmoid(zs)
    g_all = _chunk_cumsum(-jnp.exp(dnp_ref[1:2, :]) * _softplus(zs + dnp_ref[0:1, :]))
    gt_ref[...] = g_all
    for c in range(ns * nc):
        gtt_ref[c] = g_all[c * CHUNK:(c + 1) * CHUNK, :].T
    z1 = proj(GRP)
    sg_ref[...] = _silu(z5)
    z_ref[:, 0:GRP] = proj(0)
    gl_ref[...] = _gelu_tanh(z1)
    for g in range(3):
        z_ref[:, (g + 1) * GRP:(g + 2) * GRP] = proj((g + 2) * GRP)

    sp_lam = _softplus(-lam_ref[...])
    ri = lax.broadcasted_iota(jnp.int32, (CHUNK, CHUNK), 0)
    ci = lax.broadcasted_iota(jnp.int32, (CHUNK, CHUNK), 1)
    rows_w = lax.broadcasted_iota(jnp.int32, (CHUNK, LRU_WIDTH), 0)
    eye_c = (ri == ci).astype(F32)
    sub_bits = DN_SUB.bit_length() - 1
    same_blk = (ri >> sub_bits) == (ci >> sub_bits)
    merge_masks = []
    for lvl in range(sub_bits, CHUNK.bit_length() - 1):
        merge_masks.append(((ri >> (lvl + 1)) == (ci >> (lvl + 1)))
                           & (((ri >> lvl) & 1) == 1) & (((ci >> lvl) & 1) == 0))

    def chunk_body(i, carry):
        lane_seq, lane_rows, lane_chunk, ys = [], [], [], []
        for u in range(lanes):
            s, c = (i * lanes + u, 0) if nc == 1 else (u, i)
            r0 = pl.multiple_of((s * nc + c) * CHUNK, CHUNK)
            rows = pl.ds(r0, CHUNK)
            lane_seq.append(s)
            lane_rows.append(rows)
            lane_chunk.append(s * nc + c)

            xp_ref[u, top - hist:top, 0:LRU_WIDTH] = lc_o[s]
            xp_ref[u, top - hist:top, LRU_WIDTH:conv_ch] = dc_o[s]
            xp_ref[u, top:top + CHUNK, :] = z_ref[rows, :]
            lc_o[s] = xp_ref[u, top + CHUNK - hist:top + CHUNK, 0:LRU_WIDTH]
            dc_o[s] = xp_ref[u, top + CHUNK - hist:top + CHUNK, LRU_WIDTH:conv_ch]
            y = xp_ref[u, top - hist:top - hist + CHUNK, :] * cw_ref[0:1, :]
            for jj in range(1, CONV_W):
                y = y + (xp_ref[u, top - hist + jj:top - hist + jj + CHUNK, :]
                         * cw_ref[jj:jj + 1, :])
            ys.append(y)

        def rg_lru_branch(lane):
            y, s, rows = ys[lane], lane_seq[lane], lane_rows[lane]
            lx = y[:, 0:LRU_WIDTH] + lcb_ref[...]
            xb = lx.astype(BF16)
            r = jax.nn.sigmoid(_dot(xb, wa_ref[...]) + ba_ref[...])
            ig = jax.nn.sigmoid(_dot(xb, wx_ref[...]) + bx_ref[...])
            a = jnp.exp(-LRU_C * r * sp_lam)
            u = jnp.sqrt(1.0 - a * a) * (ig * lx)
            k = 1
            while k < CHUNK:
                m = rows_w >= k
                u = jnp.where(m, a * pltpu.roll(u, k, 0) + u, u)
                a = jnp.where(m, a * pltpu.roll(a, k, 0), a)
                k *= 2
            hseq = a * lh_o[s] + u
            lh_o[s] = hseq[CHUNK - 1:CHUNK]
            mix_ref[rows, 0:LRU_WIDTH] = (gl_ref[rows, :] * hseq).astype(BF16)

        streams = [(u, h) for u in range(lanes) for h in heads]
        ns_ = range(len(streams))
        beta_t = [bt_ref[rows, :] for rows in lane_rows]
        g_t = [gt_ref[rows, :] for rows in lane_rows]
        g_tt = [gtt_ref[c] for c in lane_chunk]
        states = [ds_o[lane_seq[u], h] for u, h in streams]
        kn, both_k, both_s = [], [], []
        for n, (u, h) in enumerate(streams):
            y = ys[u]
            base = LRU_WIDTH + h * HD
            q = _silu(y[:, base:base + HD])
            kk_ = _silu(y[:, base + GRP:base + GRP + HD])
            q = q * lax.rsqrt(jnp.sum(q * q, axis=-1, keepdims=True) + EPS) * (HD ** -0.5)
            kn.append(kk_ * lax.rsqrt(jnp.sum(kk_ * kk_, axis=-1, keepdims=True) + EPS))
            knb = kn[n].astype(BF16)
            qk_in = jnp.concatenate([q.astype(BF16), knb], axis=0)
            both_k.append(_dot_nt(qk_in, knb))
            both_s.append(_dot(qk_in, states[n].astype(BF16)))
        gcol, eg, e_incl, amat, rhs = [], [], [], [], []
        for n, (u, h) in enumerate(streams):
            y = ys[u]
            v = _silu(y[:, LRU_WIDTH + 2 * GRP + h * HD:LRU_WIDTH + 2 * GRP + (h + 1) * HD])
            gcol.append(g_t[u][:, HEADS + h:HEADS + h + 1])
            rel = gcol[n] - g_tt[u][HEADS + h:HEADS + h + 1, :]
            e_strict = jnp.exp(jnp.where(ri > ci, rel, -jnp.inf))
            e_incl.append(jnp.where(ri == ci, 1.0, e_strict))
            beta = beta_t[u][:, h:h + 1]
            eg.append(jnp.exp(gcol[n]))
            amat.append(beta * both_k[n][CHUNK:] * e_strict)
            rhs.append(beta * (v - eg[n] * both_s[n][CHUNK:]))
        dblk = [jnp.where(same_blk, amat[n], 0.0) for n in ns_]
        tinv = [eye_c for _ in ns_]
        for jj in range(DN_SUB - 1):
            for n in ns_:
                colv = jnp.concatenate(
                    [dblk[n][b0:b0 + DN_SUB, b0 + jj:b0 + jj + 1]
                     for b0 in range(0, CHUNK, DN_SUB)], axis=0)
                rowm = jnp.concatenate(
                    [jnp.broadcast_to(tinv[n][b0 + jj:b0 + jj + 1, :], (DN_SUB, CHUNK))
                     for b0 in range(0, CHUNK, DN_SUB)], axis=0)
                tinv[n] = tinv[n] - colv * rowm
        for lane in range(lanes):
            rg_lru_branch(lane)
        tb = [tinv[n].astype(BF16) for n in ns_]
        pend = [[_dot(tb[n], jnp.where(lm, amat[n], 0.0).astype(BF16)) for lm in merge_masks]
                for n in ns_]
        for _ in merge_masks:
            xb_ = [pend[n][0].astype(BF16) for n in ns_]
            tb = [tinv[n].astype(BF16) for n in ns_]
            tinv = [tinv[n] - _dot(xb_[n], tb[n]) for n in ns_]
            pend = [[p - _dot(xb_[n], p.astype(BF16)) for p in pend[n][1:]] for n in ns_]
        w = [_dot(tinv[n].astype(BF16), rhs[n].astype(BF16)).astype(BF16) for n in ns_]
        for n, (u, h) in enumerate(streams):
            s, rows = lane_seq[u], lane_rows[u]
            o = eg[n] * both_s[n][:CHUNK] + _dot((both_k[n][:CHUNK] * e_incl[n]).astype(BF16), w[n])
            gend = gcol[n][CHUNK - 1:CHUNK, :]
            kd = (kn[n] * jnp.exp(gend - gcol[n])).astype(BF16)
            ds_o[s, h] = states[n] * jnp.exp(gend) + _dot_tn(kd, w[n])
            o = _head_norm(o, dg_ref[:, pl.ds(h * HD, HD)]) * sg_ref[rows, pl.ds(h * HD, HD)]
            mix_ref[rows, pl.ds(LRU_WIDTH + h * HD, HD)] = o.astype(BF16)
        return carry

    lanes = ODD_LANES if ns % ODD_LANES == 0 else 1
    lax.fori_loop(0, ns * nc // lanes, chunk_body, 0, unroll=ODD_UNROLL)

    xo_ref[...] = (x + _dot(mix_ref[...], wout_ref[...])).reshape(xo_ref.shape)


def _odd_mixer(x, norm_g, w_in, w_tail, w_out, cw, lcb, wa, ba, wx, bx, lam, dnp, dn_g,
               s_lh, s_lc, s_dn, s_dc):
    b, t, _ = x.shape
    ns, nc = _seq_tiling(b, t, ODD_LANES)
    tt = ns * nc * CHUNK
    grid = (b // ns, t // (nc * CHUNK))
    x_spec = pl.BlockSpec((ns, nc * CHUNK, D_MODEL), lambda i, j: (i, j, 0))

    def full(a):
        return _resident(a.shape, lambda i, j: (0,) * a.ndim)

    def st_spec(a):
        return pl.BlockSpec((ns,) + a.shape[1:], lambda i, j: (i,) + (0,) * (a.ndim - 1))

    params = (norm_g.reshape(1, D_MODEL), w_in, w_tail, w_out, cw, lcb, wa, ba, wx, bx, lam, dnp,
              dn_g)
    states = (s_lh, s_lc, s_dn, s_dc)
    grp_f32 = pltpu.VMEM((tt, GRP), F32)
    head_f32 = pltpu.VMEM((tt, HD), F32)
    return pl.pallas_call(
        functools.partial(_odd_kernel, ns=ns, nc=nc),
        grid=grid,
        in_specs=[x_spec] + [full(a) for a in params] + [st_spec(a) for a in states],
        out_specs=[x_spec] + [st_spec(a) for a in states],
        out_shape=[jax.ShapeDtypeStruct(x.shape, F32)]
        + [jax.ShapeDtypeStruct(a.shape, F32) for a in states],
        scratch_shapes=[pltpu.VMEM((tt, LRU_WIDTH + 3 * GRP), F32), grp_f32, grp_f32,
                        head_f32, head_f32, pltpu.VMEM((ns * nc, HD, CHUNK), F32),
                        pltpu.VMEM((tt, D_MODEL), BF16),
                        pltpu.VMEM((ODD_LANES, 8 + CHUNK, LRU_WIDTH + 3 * GRP), F32)],
        compiler_params=pltpu.CompilerParams(
            dimension_semantics=("parallel", "arbitrary"), vmem_limit_bytes=VMEM_LIMIT),
        name="odd_mixer",
    )(x, *params, *states)


def _rope_tables(pos):
    half = HD // 2
    freq = ROPE_BASE ** (-jnp.arange(half, dtype=F32) / half)
    ang = pos.astype(F32)[:, None] * freq[None, :]
    cos, sin = jnp.cos(ang), jnp.sin(ang)
    return jnp.concatenate([cos, cos], axis=-1), jnp.concatenate([-sin, sin], axis=-1)


def _block_diag(w):
    n, bs, _ = w.shape
    eye = jnp.eye(n, dtype=w.dtype)
    return (eye[:, None, :, None] * w[:, :, None, :]).reshape(n * bs, n * bs)


def kernel(x_prompt, x_sample, state_ret, state_hgrn, state_lru_h, state_lru_conv, state_dn,
           state_dn_conv, ffn1_norm, ffn1_w_in, ffn1_w_out, mix_norm, ffn2_norm, ffn2_w_in,
           ffn2_w_out, final_norm, even_w_in, even_w_out, ret_out_norm, hg_out_norm,
           hg_lb_logits, odd_w_in, odd_w_out, lru_conv_w, lru_conv_b, lru_w_a, lru_b_a, lru_w_x,
           lru_b_x, lru_lambda, dn_conv_w, dn_a_log, dn_dt_bias, dn_out_norm):
    depth = ffn1_norm.shape[0]
    bp, tp, _ = x_prompt.shape
    bs, ts, _ = x_sample.shape
    past_len = 2048
    tabs = (_rope_tables(jnp.arange(tp)), _rope_tables(past_len + jnp.arange(ts)))
    xs = [x_prompt, x_sample]
    nb = (bp, bs)
    outs = {k: ([], []) for k in ("ret", "hg", "lh", "lc", "dn", "dc")}

    ffn_sets = []
    for l in range(depth):
        ffn_sets += [(ffn1_norm[l], ffn1_w_in, ffn1_w_out, l),
                     (ffn2_norm[l], ffn2_w_in, ffn2_w_out, l)]
    ffn_w = [(ffn1_w_in[0].astype(BF16), ffn1_w_out[0].astype(BF16))]

    def run_ffn(apply_final, mixer_casts=()):
        k = len(ffn_w) - 1
        casts = []
        if k + 1 < len(ffn_sets):
            _, nwi, nwo, nl = ffn_sets[k + 1]
            casts = [(nwi, nl, FFN_CAST_ROWS, nwi.shape[-1]),
                     (nwo, nl, FFN_CAST_ROWS, nwo.shape[-1])]
        casts += list(mixer_casts)
        xs[0], xs[1], cast = _ffn(xs[0], xs[1], ffn_sets[k][0], ffn_w[k][0], ffn_w[k][1],
                                  final_norm, apply_final, casts)
        ffn_w.append(tuple(cast[:2]) if k + 1 < len(ffn_sets) else ())
        return cast[len(cast) - len(mixer_casts):]

    for l in range(depth):
        j = l // 2
        last = l == depth - 1
        if l % 2 == 0:
            w_in, w_out = run_ffn(False, [(even_w_in, j, None, EVEN_IN),
                                          (even_w_out, j, None, D_MODEL)])
            for g in range(2):
                if g == 0:
                    s_ret = jnp.zeros((bp, HEADS, HD, HD), F32)
                    s_hg = jnp.zeros((bp, HEADS, HD, HD), F32)
                else:
                    s_ret, s_hg = state_ret[j], state_hgrn[j]
                xs[g], n_ret, n_hg = _even_mixer(
                    xs[g], tabs[g][0], tabs[g][1], mix_norm[l], w_in, w_out, ret_out_norm[j],
                    hg_out_norm[j], hg_lb_logits, s_ret, s_hg, j)
                outs["ret"][g].append(n_ret)
                outs["hg"][g].append(n_hg)
        else:
            w_in, w_out = run_ffn(False, [(odd_w_in, j, None, 6 * GRP),
                                          (odd_w_out, j, None, D_MODEL)])
            n_tail = odd_w_in.shape[-1] - 6 * GRP
            w_tail = jnp.pad(odd_w_in[j][:, 6 * GRP:], ((0, 0), (0, HD - n_tail))).astype(BF16)
            cw = jnp.concatenate([lru_conv_w[j], dn_conv_w[j]], axis=-1)
            wa = _block_diag(lru_w_a[j]).astype(BF16)
            wx = _block_diag(lru_w_x[j]).astype(BF16)
            dnp = jnp.zeros((2, HD), F32)
            dnp = dnp.at[0, HEADS:2 * HEADS].set(dn_dt_bias[j])
            dnp = dnp.at[1, HEADS:2 * HEADS].set(dn_a_log[j])
            for g in range(2):
                if g == 0:
                    s_lh = jnp.zeros((bp, 1, LRU_WIDTH), F32)
                    s_lc = jnp.zeros((bp, CONV_W - 1, LRU_WIDTH), F32)
                    s_dn = jnp.zeros((bp, HEADS, HD, HD), F32)
                    s_dc = jnp.zeros((bp, CONV_W - 1, 3 * GRP), F32)
                else:
                    s_lh = state_lru_h[j].reshape(bs, 1, LRU_WIDTH)
                    s_lc, s_dn, s_dc = state_lru_conv[j], state_dn[j], state_dn_conv[j]
                xs[g], n_lh, n_lc, n_dn, n_dc = _odd_mixer(
                    xs[g], mix_norm[l], w_in, w_tail, w_out, cw, lru_conv_b[j].reshape(1, -1), wa,
                    lru_b_a[j].reshape(1, -1), wx, lru_b_x[j].reshape(1, -1),
                    lru_lambda[j].reshape(1, -1), dnp, dn_out_norm[j].reshape(1, -1),
                    s_lh, s_lc, s_dn, s_dc)
                outs["lh"][g].append(n_lh.reshape(nb[g], LRU_WIDTH))
                outs["lc"][g].append(n_lc)
                outs["dn"][g].append(n_dn)
                outs["dc"][g].append(n_dc)
        run_ffn(last)

    res = [xs[0], xs[1]]
    for k in ("ret", "hg", "lh", "lc", "dn", "dc"):
        for g in range(2):
            res.append(jnp.stack(outs[k][g]))
    return tuple(res)
```

```python
import functools
import math

import jax
import jax.numpy as jnp
from jax import lax
from jax.experimental import pallas as pl
from jax.experimental.pallas import tpu as pltpu

F32 = jnp.float32
BF16 = jnp.bfloat16

D_MODEL = 1024
FF_DIM = 2816
EPS = 1e-6
CHUNK = 64
CONV_W = 4
HEADS = 4
HD = 128
ROPE_BASE = 10000.0
LRU_WIDTH = 512
LRU_C = 8.0
GRP = HEADS * HD
EVEN_IN = 8 * GRP
SUB = 8
DN_SUB = 8

VMEM_LIMIT = 56 * 1024 * 1024

FFN_TM = 512
FFN_TF = 256
FFN_CAST_ROWS = 16
SEQ_TILE = 512
MIXER_LANES = 1
ODD_LANES = 2

RET_LOG_G = [math.log1p(-(2.0 ** (-5.0 - h))) for h in range(HEADS)]


def _rms(x, g):
    return x * lax.rsqrt(jnp.mean(x * x, axis=-1, keepdims=True) + EPS) * g


def _silu(x):
    return x * jax.nn.sigmoid(x)


def _dot(a, b):
    return jnp.dot(a, b, preferred_element_type=F32)


def _dot_nt(a, b):
    return lax.dot_general(a, b, (((1,), (1,)), ((), ())), preferred_element_type=F32)


def _dot_tn(a, b):
    return lax.dot_general(a, b, (((0,), (0,)), ((), ())), preferred_element_type=F32)


def _head_norm(o, g):
    return o * lax.rsqrt(jnp.mean(o * o, axis=-1, keepdims=True) + EPS) * g


def _hcols(h):
    return slice(h * HD, (h + 1) * HD)


def _chunk_cumsum(x):
    pos = lax.broadcasted_iota(jnp.int32, x.shape, 0) & (CHUNK - 1)
    k = 1
    while k < CHUNK:
        x = jnp.where(pos >= k, x + pltpu.roll(x, k, 0), x)
        k *= 2
    return x


def _resident(block_shape, index_map):
    return pl.BlockSpec(block_shape, index_map, pipeline_mode=pl.Buffered(1))


def _ffn_kernel(*refs, n_first, apply_final, n_casts):
    xa_ref, xb_ref, g_ref, win_ref, wout_ref, fin_ref = refs[:6]
    cast_in = refs[6:6 + n_casts]
    oa_ref, ob_ref = refs[6 + n_casts:8 + n_casts]
    cast_out = refs[8 + n_casts:8 + 2 * n_casts]
    act_ref = refs[8 + 2 * n_casts]
    for src, dst in zip(cast_in, cast_out):
        dst[...] = src[...].astype(BF16)
    i = pl.program_id(0)

    def tile(x_ref, o_ref):
        x = x_ref[...]
        h = _rms(x, g_ref[...]).astype(BF16)

        for c in range(FF_DIM // FFN_TF):
            lo = c * FFN_TF
            gate = _dot(h, win_ref[:, lo:lo + FFN_TF])
            up = _dot(h, win_ref[:, FF_DIM + lo:FF_DIM + lo + FFN_TF])
            act_ref[:, lo:lo + FFN_TF] = (_silu(gate) * up).astype(BF16)
        y = x + 0.5 * _dot(act_ref[...], wout_ref[...])
        if apply_final:
            y = _rms(y, fin_ref[...])
        o_ref[...] = y

    @pl.when(i < n_first)
    def _():
        tile(xa_ref, oa_ref)

    @pl.when(i >= n_first)
    def _():
        tile(xb_ref, ob_ref)


def _ffn(xa, xb, norm_g, w_in, w_out, final_g, apply_final, casts=()):
    sa, sb = xa.shape, xb.shape
    xa2, xb2 = xa.reshape(-1, D_MODEL), xb.reshape(-1, D_MODEL)
    tm = min(FFN_TM, xa2.shape[0], xb2.shape[0])
    na, nb = xa2.shape[0] // tm, xb2.shape[0] // tm
    x_blk = (tm, D_MODEL)
    in_specs = [
        pl.BlockSpec(x_blk, lambda i: (jnp.minimum(i, na - 1), 0)),
        pl.BlockSpec(x_blk, lambda i: (jnp.maximum(i - na, 0), 0)),
        _resident((1, D_MODEL), lambda i: (0, 0)),
        _resident((D_MODEL, 2 * FF_DIM), lambda i: (0, 0)),
        _resident((FF_DIM, D_MODEL), lambda i: (0, 0)),
        _resident((1, D_MODEL), lambda i: (0, 0)),
    ]
    out_specs = [
        pl.BlockSpec(x_blk, lambda i: (jnp.minimum(i, na - 1), 0)),
        pl.BlockSpec(x_blk, lambda i: (jnp.maximum(i - na, 0), 0)),
    ]
    out_shape = [jax.ShapeDtypeStruct(xa2.shape, F32), jax.ShapeDtypeStruct(xb2.shape, F32)]
    args = [xa2, xb2, norm_g.reshape(1, D_MODEL), w_in, w_out, final_g.reshape(1, D_MODEL)]
    cast_in_specs, cast_out_specs = [], []
    for arr, layer, row_blocks, c in casts:
        r = arr.shape[1]
        ncol = na // (row_blocks or na)
        rows = r // (na // ncol)

        def blk(i, layer=layer, ncol=ncol):
            i = jnp.minimum(i, na - 1)
            return layer, i // ncol, i % ncol

        cast_in_specs.append(pl.BlockSpec((None, rows, c // ncol), blk))
        cast_out_specs.append(pl.BlockSpec((rows, c // ncol), lambda i, blk=blk: blk(i)[1:]))
        out_shape.append(jax.ShapeDtypeStruct((r, c), BF16))
        args.append(arr)
    outs = pl.pallas_call(
        functools.partial(_ffn_kernel, n_first=na, apply_final=apply_final,
                          n_casts=len(casts)),
        grid=(na + nb,),
        in_specs=in_specs + cast_in_specs,
        out_specs=out_specs + cast_out_specs,
        out_shape=out_shape,
        scratch_shapes=[pltpu.VMEM((tm, FF_DIM), BF16)],
        compiler_params=pltpu.CompilerParams(
            dimension_semantics=("arbitrary",), vmem_limit_bytes=VMEM_LIMIT),
        name="ffn",
    )(*args)
    return outs[0].reshape(sa), outs[1].reshape(sb), list(outs[2:])


def _seq_tiling(b, t, lanes=1):
    if t >= SEQ_TILE:
        lanes = lanes if b % lanes == 0 else 1
        return lanes, SEQ_TILE // CHUNK // lanes
    return min(b, SEQ_TILE // t), t // CHUNK


def _even_kernel(x_ref, cos_ref, sin_ref, ng_ref, win_ref, wout_ref, rg_ref, hg_ref, lbl_ref,
                 sret_ref, shg_ref, xo_ref, sret_o, shg_o,
                 qs_ref, qd_ref, kb_ref, kd_ref, rv_ref, hv_ref,
                 rgate_ref, hq_ref, hf_ref, hgate_ref, mix_ref, hgt_ref, *, ns, nc, layer_j):
    t = pl.program_id(1)
    tt = ns * nc * CHUNK
    heads = range(HEADS)

    @pl.when(t == 0)
    def _():
        sret_o[...] = sret_ref[...]
        for s in range(ns):
            for h in heads:
                hgt_ref[s, h] = shg_ref[s, h].T

    x = x_ref[...].reshape(tt, D_MODEL)
    hn = _rms(x, ng_ref[...]).astype(BF16)

    def proj(g):
        return _dot(hn, win_ref[:, g * GRP:(g + 1) * GRP])

    cos = jnp.concatenate([cos_ref[...]] * ns, axis=0)
    sin = jnp.concatenate([sin_ref[...]] * ns, axis=0)
    pos = (lax.broadcasted_iota(jnp.int32, (tt, HD), 0) & (CHUNK - 1)).astype(F32)
    scale = HD ** -0.5

    def rope(z, h):
        v = z[:, _hcols(h)]
        return v * cos + pltpu.roll(v, HD // 2, 1) * sin

    z0 = proj(0)
    z1 = proj(1)
    for h in heads:
        q = rope(z0, h)
        qs_ref[:, _hcols(h)] = (q * scale).astype(BF16)
        qd_ref[:, _hcols(h)] = (q * (jnp.exp(RET_LOG_G[h] * (pos + 1.0)) * scale)).astype(BF16)
    z2 = proj(2)
    for h in heads:
        k = rope(z1, h)
        kb_ref[:, _hcols(h)] = k.astype(BF16)
        kd_ref[:, _hcols(h)] = (k * jnp.exp(RET_LOG_G[h] * (CHUNK - 1.0 - pos))).astype(BF16)
    z3 = proj(3)
    rv_ref[...] = z2.astype(BF16)
    z4 = proj(4)
    rgate_ref[...] = _silu(z3)
    z5 = proj(5)
    hq_ref[...] = _silu(z4)
    z6 = proj(6)
    hf_ref[...] = z5
    z7 = proj(7)
    hv_ref[...] = z6.astype(BF16)
    hgate_ref[...] = _silu(z7)

    ri = lax.broadcasted_iota(jnp.int32, (CHUNK, CHUNK), 0)
    ci = lax.broadcasted_iota(jnp.int32, (CHUNK, CHUNK), 1)
    absd = jnp.abs(ri - ci).astype(F32)
    intra = [jnp.exp(RET_LOG_G[h] * absd) for h in heads]
    sdec = [math.exp(RET_LOG_G[h] * CHUNK) for h in heads]

    lbl = lbl_ref[...]
    e = jnp.exp(lbl - jnp.max(lbl, axis=0, keepdims=True))
    sm = e / jnp.sum(e, axis=0, keepdims=True)
    lb_all = jnp.sum(sm[:layer_j + 1], axis=0, keepdims=True)

    nb = CHUNK // SUB
    t_io = lax.broadcasted_iota(jnp.int32, (nb, SUB, HD), 1)
    lane_io = lax.broadcasted_iota(jnp.int32, (nb, SUB, CHUNK), 2)
    blk_io = lax.broadcasted_iota(jnp.int32, (nb, SUB, CHUNK), 0)

    def chunk_body(i, carry):
        streams = []
        for u in range(lanes):
            s, c = (i * lanes + u, 0) if nc == 1 else (u, i)
            r0 = (s * nc + c) * CHUNK
            streams += [(s, r0, h) for h in heads]

        def ld(ref, r0, h):
            return ref[pl.ds(r0, CHUNK), pl.ds(h * HD, HD)]

        ret_states = [sret_o[s, h] for s, _, h in streams]
        hg_states = [hgt_ref[s, h] for s, _, h in streams]

        r_v, r_att, r_inter = [], [], []
        for n, (s, r0, h) in enumerate(streams):
            st = ret_states[n]
            r_v.append(ld(rv_ref, r0, h))
            r_att.append(_dot_nt(ld(qs_ref, r0, h), ld(kb_ref, r0, h)))
            r_inter.append(_dot(ld(qd_ref, r0, h), st.astype(BF16)))
            sret_o[s, h] = st * sdec[h] + _dot_tn(ld(kd_ref, r0, h), r_v[n])

        h_q, h_k, h_b, h_v, h_off, h_inter = [], [], [], [], [], []
        for n, (s, r0, h) in enumerate(streams):
            lb = lb_all[:, _hcols(h)]
            sig = jax.nn.sigmoid(ld(hf_ref, r0, h))
            v = ld(hv_ref, r0, h)
            q = ld(hq_ref, r0, h)
            k = (1.0 - lb) * (1.0 - sig)
            b = _chunk_cumsum(jnp.log(lb + (1.0 - lb) * sig))
            stt = hg_states[n]
            off = []
            for blk in range(1, nb):
                lo = blk * SUB
                rb = b[lo - 1:lo, :]
                qt = (q[lo:lo + SUB] * jnp.exp(b[lo:lo + SUB] - rb)).astype(BF16)
                kt = jnp.concatenate([k[:lo] * jnp.exp(rb - b[:lo]),
                                      jnp.zeros((CHUNK - lo, HD), F32)], axis=0).astype(BF16)
                off.append(_dot_nt(qt, kt))
            h_off.append(off)
            h_inter.append(_dot_nt((q * jnp.exp(b)).astype(BF16), stt.astype(BF16)))
            bend = b[CHUNK - 1:CHUNK, :]
            kd = (k * jnp.exp(bend - b)).astype(BF16)
            hgt_ref[s, h] = stt * jnp.exp(bend) + _dot_tn(v, kd)
            h_q.append(q)
            h_k.append(k)
            h_b.append(b)
            h_v.append(v)

        for n, (s, r0, h) in enumerate(streams):
            o = _dot((r_att[n] * intra[h]).astype(BF16), r_v[n]) + r_inter[n]
            o = _head_norm(o, rg_ref[:, pl.ds(h * HD, HD)]) * ld(rgate_ref, r0, h)
            mix_ref[pl.ds(r0, CHUNK), pl.ds(h * HD, HD)] = o.astype(BF16)

        h_att = []
        for n in range(len(streams)):
            b4 = h_b[n].reshape(nb, SUB, HD)
            q4 = h_q[n].reshape(nb, SUB, HD)
            k4 = h_k[n].reshape(nb, SUB, HD)
            diag = jnp.zeros((nb, SUB, CHUNK), F32)
            for sp in range(SUB):
                arg = jnp.where(t_io >= sp, b4 - b4[:, sp:sp + 1, :], -jnp.inf)
                r = jnp.sum(jnp.exp(arg) * q4 * k4[:, sp:sp + 1, :], axis=-1, keepdims=True)
                diag = jnp.where(lane_io == blk_io * SUB + sp, r, diag)
            diag = diag.reshape(CHUNK, CHUNK)
            parts = [diag[0:SUB]]
            for blk in range(1, nb):
                parts.append(diag[blk * SUB:(blk + 1) * SUB] + h_off[n][blk - 1])
            h_att.append(jnp.concatenate(parts, axis=0).astype(BF16))
        for n, (s, r0, h) in enumerate(streams):
            o = _dot(h_att[n], h_v[n]) + h_inter[n]
            o = _head_norm(o, hg_ref[:, pl.ds(h * HD, HD)]) * ld(hgate_ref, r0, h)
            mix_ref[pl.ds(r0, CHUNK), pl.ds(GRP + h * HD, HD)] = o.astype(BF16)
        return carry

    lanes = MIXER_LANES if ns % MIXER_LANES == 0 else 1
    for i in range(ns * nc // lanes):
        chunk_body(i, 0)

    xo_ref[...] = (x + _dot(mix_ref[...], wout_ref[...])).reshape(xo_ref.shape)

    @pl.when(t == pl.num_programs(1) - 1)
    def _():
        for s in range(ns):
            for h in heads:
                shg_o[s, h] = hgt_ref[s, h].T


def _even_mixer(x, cos, sin, norm_g, w_in, w_out, ret_g, hg_g, lb_logits, s_ret, s_hg, layer_j):
    b, t, _ = x.shape
    ns, nc = _seq_tiling(b, t, MIXER_LANES)
    tt = ns * nc * CHUNK
    grid = (b // ns, t // (nc * CHUNK))
    st_spec = pl.BlockSpec((ns, HEADS, HD, HD), lambda i, j: (i, 0, 0, 0))
    x_spec = pl.BlockSpec((ns, nc * CHUNK, D_MODEL), lambda i, j: (i, j, 0))

    def full(a):
        return _resident(a.shape, lambda i, j: (0,) * a.ndim)

    tab_spec = pl.BlockSpec((nc * CHUNK, HD), lambda i, j: (j, 0))
    args = (x, cos, sin, norm_g.reshape(1, D_MODEL), w_in, w_out, ret_g.reshape(1, GRP),
            hg_g.reshape(1, GRP), lb_logits, s_ret, s_hg)
    in_specs = [x_spec, tab_spec, tab_spec] + [full(a) for a in args[3:9]] + [st_spec, st_spec]
    grp_bf16 = pltpu.VMEM((tt, GRP), BF16)
    grp_f32 = pltpu.VMEM((tt, GRP), F32)
    return pl.pallas_call(
        functools.partial(_even_kernel, ns=ns, nc=nc, layer_j=layer_j),
        grid=grid,
        in_specs=in_specs,
        out_specs=[x_spec, st_spec, st_spec],
        out_shape=[jax.ShapeDtypeStruct(x.shape, F32),
                   jax.ShapeDtypeStruct(s_ret.shape, F32),
                   jax.ShapeDtypeStruct(s_hg.shape, F32)],
        scratch_shapes=[grp_bf16] * 6 + [grp_f32] * 4
        + [pltpu.VMEM((tt, D_MODEL), BF16), pltpu.VMEM((ns, HEADS, HD, HD), F32)],
        compiler_params=pltpu.CompilerParams(
            dimension_semantics=("parallel", "arbitrary"), vmem_limit_bytes=VMEM_LIMIT),
        name="even_mixer",
    )(*args)


def _softplus(x):
    return jnp.maximum(x, 0.0) + jnp.log1p(jnp.exp(-jnp.abs(x)))


def _gelu_tanh(x):
    return 0.5 * x * (1.0 + jnp.tanh(math.sqrt(2.0 / math.pi) * (x + 0.044715 * (x * x * x))))


def _odd_kernel(x_ref, ng_ref, win_ref, wt_ref, wout_ref, cw_ref, lcb_ref, wa_ref, ba_ref,
                wx_ref, bx_ref, lam_ref, dnp_ref, dg_ref, lh_ref, lc_ref, ds_ref, dc_ref,
                xo_ref, lh_o, lc_o, ds_o, dc_o,
                z_ref, gl_ref, sg_ref, bt_ref, gt_ref, gtt_ref, mix_ref, xp_ref, *, ns, nc):
    t = pl.program_id(1)
    tt = ns * nc * CHUNK
    conv_ch = LRU_WIDTH + 3 * GRP
    hist = CONV_W - 1
    top = 8
    heads = range(HEADS)

    @pl.when(t == 0)
    def _():
        lh_o[...] = lh_ref[...]
        lc_o[...] = lc_ref[...]
        ds_o[...] = ds_ref[...]
        dc_o[...] = dc_ref[...]

    x = x_ref[...].reshape(tt, D_MODEL)
    hn = _rms(x, ng_ref[...]).astype(BF16)

    def proj(lo):
        return _dot(hn, win_ref[:, lo:lo + GRP])

    zs = _dot(hn, wt_ref[...])
    z5 = proj(5 * GRP)
    bt_ref[...] = jax.nn.sigmoid(zs)
    g_all = _chunk_cumsum(-jnp.exp(dnp_ref[1:2, :]) * _softplus(zs + dnp_ref[0:1, :]))
    gt_ref[...] = g_all
    for c in range(ns * nc):
        gtt_ref[c] = g_all[c * CHUNK:(c + 1) * CHUNK, :].T
    z1 = proj(GRP)
    sg_ref[...] = _silu(z5)
    z_ref[:, 0:GRP] = proj(0)
    gl_ref[...] = _gelu_tanh(z1)
    for g in range(3):
        z_ref[:, (g + 1) * GRP:(g + 2) * GRP] = proj((g + 2) * GRP)

    sp_lam = _softplus(-lam_ref[...])
    ri = lax.broadcasted_iota(jnp.int32, (CHUNK, CHUNK), 0)
    ci = lax.broadcasted_iota(jnp.int32, (CHUNK, CHUNK), 1)
    rows_w = lax.broadcasted_iota(jnp.int32, (CHUNK, LRU_WIDTH), 0)
    eye_c = (ri == ci).astype(F32)
    sub_bits = DN_SUB.bit_length() - 1
    same_blk = (ri >> sub_bits) == (ci >> sub_bits)
    merge_masks = []
    for lvl in range(sub_bits, CHUNK.bit_length() - 1):
        merge_masks.append(((ri >> (lvl + 1)) == (ci >> (lvl + 1)))
                           & (((ri >> lvl) & 1) == 1) & (((ci >> lvl) & 1) == 0))

    def chunk_body(i, carry):
        lane_seq, lane_rows, lane_chunk, ys = [], [], [], []
        for u in range(lanes):
            s, c = (i * lanes + u, 0) if nc == 1 else (u, i)
            r0 = (s * nc + c) * CHUNK
            rows = pl.ds(r0, CHUNK)
            lane_seq.append(s)
            lane_rows.append(rows)
            lane_chunk.append(s * nc + c)

            xp_ref[u, top - hist:top, 0:LRU_WIDTH] = lc_o[s]
            xp_ref[u, top - hist:top, LRU_WIDTH:conv_ch] = dc_o[s]
            xp_ref[u, top:top + CHUNK, :] = z_ref[rows, :]
            lc_o[s] = xp_ref[u, top + CHUNK - hist:top + CHUNK, 0:LRU_WIDTH]
            dc_o[s] = xp_ref[u, top + CHUNK - hist:top + CHUNK, LRU_WIDTH:conv_ch]
            y = xp_ref[u, top - hist:top - hist + CHUNK, :] * cw_ref[0:1, :]
            for jj in range(1, CONV_W):
                y = y + (xp_ref[u, top - hist + jj:top - hist + jj + CHUNK, :]
                         * cw_ref[jj:jj + 1, :])
            ys.append(y)

        def rg_lru_branch(lane):
            y, s, rows = ys[lane], lane_seq[lane], lane_rows[lane]
            lx = y[:, 0:LRU_WIDTH] + lcb_ref[...]
            xb = lx.astype(BF16)
            r = jax.nn.sigmoid(_dot(xb, wa_ref[...]) + ba_ref[...])
            ig = jax.nn.sigmoid(_dot(xb, wx_ref[...]) + bx_ref[...])
            a = jnp.exp(-LRU_C * r * sp_lam)
            u = jnp.sqrt(1.0 - a * a) * (ig * lx)
            k = 1
            while k < CHUNK:
                m = rows_w >= k
                u = jnp.where(m, a * pltpu.roll(u, k, 0) + u, u)
                a = jnp.where(m, a * pltpu.roll(a, k, 0), a)
                k *= 2
            hseq = a * lh_o[s] + u
            lh_o[s] = hseq[CHUNK - 1:CHUNK]
            mix_ref[rows, 0:LRU_WIDTH] = (gl_ref[rows, :] * hseq).astype(BF16)

        streams = [(u, h) for u in range(lanes) for h in heads]
        ns_ = range(len(streams))
        beta_t = [bt_ref[rows, :] for rows in lane_rows]
        g_t = [gt_ref[rows, :] for rows in lane_rows]
        g_tt = [gtt_ref[c] for c in lane_chunk]
        states = [ds_o[lane_seq[u], h] for u, h in streams]
        kn, both_k, both_s = [], [], []
        for n, (u, h) in enumerate(streams):
            y = ys[u]
            base = LRU_WIDTH + h * HD
            q = _silu(y[:, base:base + HD])
            kk_ = _silu(y[:, base + GRP:base + GRP + HD])
            q = q * lax.rsqrt(jnp.sum(q * q, axis=-1, keepdims=True) + EPS) * (HD ** -0.5)
            kn.append(kk_ * lax.rsqrt(jnp.sum(kk_ * kk_, axis=-1, keepdims=True) + EPS))
            knb = kn[n].astype(BF16)
            qk_in = jnp.concatenate([q.astype(BF16), knb], axis=0)
            both_k.append(_dot_nt(qk_in, knb))
            both_s.append(_dot(qk_in, states[n].astype(BF16)))
        gcol, eg, e_incl, amat, rhs = [], [], [], [], []
        for n, (u, h) in enumerate(streams):
            y = ys[u]
            v = _silu(y[:, LRU_WIDTH + 2 * GRP + h * HD:LRU_WIDTH + 2 * GRP + (h + 1) * HD])
            gcol.append(g_t[u][:, HEADS + h:HEADS + h + 1])
            rel = gcol[n] - g_tt[u][HEADS + h:HEADS + h + 1, :]
            e_strict = jnp.exp(jnp.where(ri > ci, rel, -jnp.inf))
            e_incl.append(jnp.where(ri == ci, 1.0, e_strict))
            beta = beta_t[u][:, h:h + 1]
            eg.append(jnp.exp(gcol[n]))
            amat.append(beta * both_k[n][CHUNK:] * e_strict)
            rhs.append(beta * (v - eg[n] * both_s[n][CHUNK:]))
        dblk = [jnp.where(same_blk, amat[n], 0.0) for n in ns_]
        tinv = [eye_c for _ in ns_]
        for jj in range(DN_SUB - 1):
            for n in ns_:
                colv = jnp.concatenate(
                    [dblk[n][b0:b0 + DN_SUB, b0 + jj:b0 + jj + 1]
                     for b0 in range(0, CHUNK, DN_SUB)], axis=0)
                rowm = jnp.concatenate(
                    [jnp.broadcast_to(tinv[n][b0 + jj:b0 + jj + 1, :], (DN_SUB, CHUNK))
                     for b0 in range(0, CHUNK, DN_SUB)], axis=0)
                tinv[n] = tinv[n] - colv * rowm
        for lane in range(lanes):
            rg_lru_branch(lane)
        tb = [tinv[n].astype(BF16) for n in ns_]
        pend = [[_dot(tb[n], jnp.where(lm, amat[n], 0.0).astype(BF16)) for lm in merge_masks]
                for n in ns_]
        for _ in merge_masks:
            xb_ = [pend[n][0].astype(BF16) for n in ns_]
            tb = [tinv[n].astype(BF16) for n in ns_]
            tinv = [tinv[n] - _dot(xb_[n], tb[n]) for n in ns_]
            pend = [[p - _dot(xb_[n], p.astype(BF16)) for p in pend[n][1:]] for n in ns_]
        w = [_dot(tinv[n].astype(BF16), rhs[n].astype(BF16)).astype(BF16) for n in ns_]
        for n, (u, h) in enumerate(streams):
            s, rows = lane_seq[u], lane_rows[u]
            o = eg[n] * both_s[n][:CHUNK] + _dot((both_k[n][:CHUNK] * e_incl[n]).astype(BF16), w[n])
            gend = gcol[n][CHUNK - 1:CHUNK, :]
            kd = (kn[n] * jnp.exp(gend - gcol[n])).astype(BF16)
            ds_o[s, h] = states[n] * jnp.exp(gend) + _dot_tn(kd, w[n])
            o = _head_norm(o, dg_ref[:, pl.ds(h * HD, HD)]) * sg_ref[rows, pl.ds(h * HD, HD)]
            mix_ref[rows, pl.ds(LRU_WIDTH + h * HD, HD)] = o.astype(BF16)
        return carry

    lanes = ODD_LANES if ns % ODD_LANES == 0 else 1
    for i in range(ns * nc // lanes):
        chunk_body(i, 0)

    xo_ref[...] = (x + _dot(mix_ref[...], wout_ref[...])).reshape(xo_ref.shape)


def _odd_mixer(x, norm_g, w_in, w_tail, w_out, cw, lcb, wa, ba, wx, bx, lam, dnp, dn_g,
               s_lh, s_lc, s_dn, s_dc):
    b, t, _ = x.shape
    ns, nc = _seq_tiling(b, t, ODD_LANES)
    tt = ns * nc * CHUNK
    grid = (b // ns, t // (nc * CHUNK))
    x_spec = pl.BlockSpec((ns, nc * CHUNK, D_MODEL), lambda i, j: (i, j, 0))

    def full(a):
        return _resident(a.shape, lambda i, j: (0,) * a.ndim)

    def st_spec(a):
        return pl.BlockSpec((ns,) + a.shape[1:], lambda i, j: (i,) + (0,) * (a.ndim - 1))

    params = (norm_g.reshape(1, D_MODEL), w_in, w_tail, w_out, cw, lcb, wa, ba, wx, bx, lam, dnp,
              dn_g)
    states = (s_lh, s_lc, s_dn, s_dc)
    grp_f32 = pltpu.VMEM((tt, GRP), F32)
    head_f32 = pltpu.VMEM((tt, HD), F32)
    return pl.pallas_call(
        functools.partial(_odd_kernel, ns=ns, nc=nc),
        grid=grid,
        in_specs=[x_spec] + [full(a) for a in params] + [st_spec(a) for a in states],
        out_specs=[x_spec] + [st_spec(a) for a in states],
        out_shape=[jax.ShapeDtypeStruct(x.shape, F32)]
        + [jax.ShapeDtypeStruct(a.shape, F32) for a in states],
        scratch_shapes=[pltpu.VMEM((tt, LRU_WIDTH + 3 * GRP), F32), grp_f32, grp_f32,
                        head_f32, head_f32, pltpu.VMEM((ns * nc, HD, CHUNK), F32),
                        pltpu.VMEM((tt, D_MODEL), BF16),
                        pltpu.VMEM((ODD_LANES, 8 + CHUNK, LRU_WIDTH + 3 * GRP), F32)],
        compiler_params=pltpu.CompilerParams(
            dimension_semantics=("parallel", "arbitrary"), vmem_limit_bytes=VMEM_LIMIT),
        name="odd_mixer",
    )(x, *params, *states)


def _rope_tables(pos):
    half = HD // 2
    freq = ROPE_BASE ** (-jnp.arange(half, dtype=F32) / half)
    ang = pos.astype(F32)[:, None] * freq[None, :]
    cos, sin = jnp.cos(ang), jnp.sin(ang)
    return jnp.concatenate([cos, cos], axis=-1), jnp.concatenate([-sin, sin], axis=-1)


def _block_diag(w):
    n, bs, _ = w.shape
    eye = jnp.eye(n, dtype=w.dtype)
    return (eye[:, None, :, None] * w[:, :, None, :]).reshape(n * bs, n * bs)


def kernel(x_prompt, x_sample, state_ret, state_hgrn, state_lru_h, state_lru_conv, state_dn,
           state_dn_conv, ffn1_norm, ffn1_w_in, ffn1_w_out, mix_norm, ffn2_norm, ffn2_w_in,
           ffn2_w_out, final_norm, even_w_in, even_w_out, ret_out_norm, hg_out_norm,
           hg_lb_logits, odd_w_in, odd_w_out, lru_conv_w, lru_conv_b, lru_w_a, lru_b_a, lru_w_x,
           lru_b_x, lru_lambda, dn_conv_w, dn_a_log, dn_dt_bias, dn_out_norm):
    depth = ffn1_norm.shape[0]
    bp, tp, _ = x_prompt.shape
    bs, ts, _ = x_sample.shape
    past_len = 2048
    tabs = (_rope_tables(jnp.arange(tp)), _rope_tables(past_len + jnp.arange(ts)))
    xs = [x_prompt, x_sample]
    nb = (bp, bs)
    outs = {k: ([], []) for k in ("ret", "hg", "lh", "lc", "dn", "dc")}

    ffn_sets = []
    for l in range(depth):
        ffn_sets += [(ffn1_norm[l], ffn1_w_in, ffn1_w_out, l),
                     (ffn2_norm[l], ffn2_w_in, ffn2_w_out, l)]
    ffn_w = [(ffn1_w_in[0].astype(BF16), ffn1_w_out[0].astype(BF16))]

    def run_ffn(apply_final, mixer_casts=()):
        k = len(ffn_w) - 1
        casts = []
        if k + 1 < len(ffn_sets):
            _, nwi, nwo, nl = ffn_sets[k + 1]
            casts = [(nwi, nl, FFN_CAST_ROWS, nwi.shape[-1]),
                     (nwo, nl, FFN_CAST_ROWS, nwo.shape[-1])]
        casts += list(mixer_casts)
        xs[0], xs[1], cast = _ffn(xs[0], xs[1], ffn_sets[k][0], ffn_w[k][0], ffn_w[k][1],
                                  final_norm, apply_final, casts)
        ffn_w.append(tuple(cast[:2]) if k + 1 < len(ffn_sets) else ())
        return cast[len(cast) - len(mixer_casts):]

    for l in range(depth):
        j = l // 2
        last = l == depth - 1
        if l % 2 == 0:
            w_in, w_out = run_ffn(False, [(even_w_in, j, None, EVEN_IN),
                                          (even_w_out, j, None, D_MODEL)])
            for g in range(2):
                if g == 0:
                    s_ret = jnp.zeros((bp, HEADS, HD, HD), F32)
                    s_hg = jnp.zeros((bp, HEADS, HD, HD), F32)
                else:
                    s_ret, s_hg = state_ret[j], state_hgrn[j]
                xs[g], n_ret, n_hg = _even_mixer(
                    xs[g], tabs[g][0], tabs[g][1], mix_norm[l], w_in, w_out, ret_out_norm[j],
                    hg_out_norm[j], hg_lb_logits, s_ret, s_hg, j)
                outs["ret"][g].append(n_ret)
                outs["hg"][g].append(n_hg)
        else:
            w_in, w_out = run_ffn(False, [(odd_w_in, j, None, 6 * GRP),
                                          (odd_w_out, j, None, D_MODEL)])
            n_tail = odd_w_in.shape[-1] - 6 * GRP
            w_tail = jnp.pad(odd_w_in[j][:, 6 * GRP:], ((0, 0), (0, HD - n_tail))).astype(BF16)
            cw = jnp.concatenate([lru_conv_w[j], dn_conv_w[j]], axis=-1)
            wa = _block_diag(lru_w_a[j]).astype(BF16)
            wx = _block_diag(lru_w_x[j]).astype(BF16)
            dnp = jnp.zeros((2, HD), F32)
            dnp = dnp.at[0, HEADS:2 * HEADS].set(dn_dt_bias[j])
            dnp = dnp.at[1, HEADS:2 * HEADS].set(dn_a_log[j])
            for g in range(2):
                if g == 0:
                    s_lh = jnp.zeros((bp, 1, LRU_WIDTH), F32)
                    s_lc = jnp.zeros((bp, CONV_W - 1, LRU_WIDTH), F32)
                    s_dn = jnp.zeros((bp, HEADS, HD, HD), F32)
                    s_dc = jnp.zeros((bp, CONV_W - 1, 3 * GRP), F32)
                else:
                    s_lh = state_lru_h[j].reshape(bs, 1, LRU_WIDTH)
                    s_lc, s_dn, s_dc = state_lru_conv[j], state_dn[j], state_dn_conv[j]
                xs[g], n_lh, n_lc, n_dn, n_dc = _odd_mixer(
                    xs[g], mix_norm[l], w_in, w_tail, w_out, cw, lru_conv_b[j].reshape(1, -1), wa,
                    lru_b_a[j].reshape(1, -1), wx, lru_b_x[j].reshape(1, -1),
                    lru_lambda[j].reshape(1, -1), dnp, dn_out_norm[j].reshape(1, -1),
                    s_lh, s_lc, s_dn, s_dc)
                outs["lh"][g].append(n_lh.reshape(nb[g], LRU_WIDTH))
                outs["lc"][g].append(n_lc)
                outs["dn"][g].append(n_dn)
                outs["dc"][g].append(n_dc)
        run_ffn(last)

    res = [xs[0], xs[1]]
    for k in ("ret", "hg", "lh", "lc", "dn", "dc"):
        for g in range(2):
            res.append(jnp.stack(outs[k][g]))
    return tuple(res)
```

```python
import functools
import math

import jax
import jax.numpy as jnp
from jax import lax
from jax.experimental import pallas as pl
from jax.experimental.pallas import tpu as pltpu

F32 = jnp.float32
BF16 = jnp.bfloat16

D_MODEL = 1024
FF_DIM = 2816
EPS = 1e-6
CHUNK = 64
CONV_W = 4
HEADS = 4
HD = 128
ROPE_BASE = 10000.0
LRU_WIDTH = 512
LRU_C = 8.0
GRP = HEADS * HD
EVEN_IN = 8 * GRP
SUB = 8
DN_SUB = 4

VMEM_LIMIT = 56 * 1024 * 1024

FFN_TM = 512
FFN_TF = 256
FFN_ROW_SPLIT = 2
FFN_CAST_ROWS = 16
SEQ_TILE = 512
MIXER_LANES = 1
ODD_LANES = 2

RET_LOG_G = [math.log1p(-(2.0 ** (-5.0 - h))) for h in range(HEADS)]


def _rms(x, g):
    return x * lax.rsqrt(jnp.mean(x * x, axis=-1, keepdims=True) + EPS) * g


def _silu(x):
    return x * jax.nn.sigmoid(x)


def _dot(a, b):
    return jnp.dot(a, b, preferred_element_type=F32)


def _dot_nt(a, b):
    return lax.dot_general(a, b, (((1,), (1,)), ((), ())), preferred_element_type=F32)


def _dot_tn(a, b):
    return lax.dot_general(a, b, (((0,), (0,)), ((), ())), preferred_element_type=F32)


def _head_norm(o, g):
    return o * lax.rsqrt(jnp.mean(o * o, axis=-1, keepdims=True) + EPS) * g


def _hcols(h):
    return slice(h * HD, (h + 1) * HD)


def _chunk_cumsum(x):
    pos = lax.broadcasted_iota(jnp.int32, x.shape, 0) & (CHUNK - 1)
    k = 1
    while k < CHUNK:
        x = jnp.where(pos >= k, x + pltpu.roll(x, k, 0), x)
        k *= 2
    return x


def _resident(block_shape, index_map):
    return pl.BlockSpec(block_shape, index_map, pipeline_mode=pl.Buffered(1))


def _ffn_kernel(*refs, n_first, apply_final, n_casts):
    xa_ref, xb_ref, g_ref, win_ref, wout_ref, fin_ref = refs[:6]
    cast_in = refs[6:6 + n_casts]
    oa_ref, ob_ref = refs[6 + n_casts:8 + n_casts]
    cast_out = refs[8 + n_casts:8 + 2 * n_casts]
    act_ref = refs[8 + 2 * n_casts]
    for src, dst in zip(cast_in, cast_out):
        dst[...] = src[...].astype(BF16)
    i = pl.program_id(0)

    def tile(x_ref, o_ref):
        tm = x_ref.shape[0]
        halves = [slice(r, r + tm // FFN_ROW_SPLIT) for r in range(0, tm, tm // FFN_ROW_SPLIT)]
        hs = [_rms(x_ref[rows, :], g_ref[...]).astype(BF16) for rows in halves]
        for c in range(FF_DIM // FFN_TF):
            lo = c * FFN_TF
            for rows, h in zip(halves, hs):
                gate = _dot(h, win_ref[:, lo:lo + FFN_TF])
                up = _dot(h, win_ref[:, FF_DIM + lo:FF_DIM + lo + FFN_TF])
                act_ref[rows, lo:lo + FFN_TF] = (_silu(gate) * up).astype(BF16)
        for rows in halves:
            y = x_ref[rows, :] + 0.5 * _dot(act_ref[rows, :], wout_ref[...])
            if apply_final:
                y = _rms(y, fin_ref[...])
            o_ref[rows, :] = y

    @pl.when(i < n_first)
    def _():
        tile(xa_ref, oa_ref)

    @pl.when(i >= n_first)
    def _():
        tile(xb_ref, ob_ref)


def _ffn(xa, xb, norm_g, w_in, w_out, final_g, apply_final, casts=()):
    sa, sb = xa.shape, xb.shape
    xa2, xb2 = xa.reshape(-1, D_MODEL), xb.reshape(-1, D_MODEL)
    tm = min(FFN_TM, xa2.shape[0], xb2.shape[0])
    na, nb = xa2.shape[0] // tm, xb2.shape[0] // tm
    x_blk = (tm, D_MODEL)
    in_specs = [
        pl.BlockSpec(x_blk, lambda i: (jnp.minimum(i, na - 1), 0)),
        pl.BlockSpec(x_blk, lambda i: (jnp.maximum(i - na, 0), 0)),
        _resident((1, D_MODEL), lambda i: (0, 0)),
        _resident((D_MODEL, 2 * FF_DIM), lambda i: (0, 0)),
        _resident((FF_DIM, D_MODEL), lambda i: (0, 0)),
        _resident((1, D_MODEL), lambda i: (0, 0)),
    ]
    out_specs = [
        pl.BlockSpec(x_blk, lambda i: (jnp.minimum(i, na - 1), 0)),
        pl.BlockSpec(x_blk, lambda i: (jnp.maximum(i - na, 0), 0)),
    ]
    out_shape = [jax.ShapeDtypeStruct(xa2.shape, F32), jax.ShapeDtypeStruct(xb2.shape, F32)]
    args = [xa2, xb2, norm_g.reshape(1, D_MODEL), w_in, w_out, final_g.reshape(1, D_MODEL)]
    cast_in_specs, cast_out_specs = [], []
    for arr, layer, row_blocks, c in casts:
        r = arr.shape[1]
        ncol = na // (row_blocks or na)
        rows = r // (na // ncol)

        def blk(i, layer=layer, ncol=ncol):
            i = jnp.minimum(i, na - 1)
            return layer, i // ncol, i % ncol

        cast_in_specs.append(pl.BlockSpec((None, rows, c // ncol), blk))
        cast_out_specs.append(pl.BlockSpec((rows, c // ncol), lambda i, blk=blk: blk(i)[1:]))
        out_shape.append(jax.ShapeDtypeStruct((r, c), BF16))
        args.append(arr)
    outs = pl.pallas_call(
        functools.partial(_ffn_kernel, n_first=na, apply_final=apply_final,
                          n_casts=len(casts)),
        grid=(na + nb,),
        in_specs=in_specs + cast_in_specs,
        out_specs=out_specs + cast_out_specs,
        out_shape=out_shape,
        scratch_shapes=[pltpu.VMEM((tm, FF_DIM), BF16)],
        compiler_params=pltpu.CompilerParams(
            dimension_semantics=("arbitrary",), vmem_limit_bytes=VMEM_LIMIT),
        name="ffn",
    )(*args)
    return outs[0].reshape(sa), outs[1].reshape(sb), list(outs[2:])


def _seq_tiling(b, t, lanes=1):
    if t >= SEQ_TILE:
        lanes = lanes if b % lanes == 0 else 1
        return lanes, SEQ_TILE // CHUNK // lanes
    return min(b, SEQ_TILE // t), t // CHUNK


def _even_kernel(x_ref, cos_ref, sin_ref, ng_ref, win_ref, wout_ref, rg_ref, hg_ref, lbl_ref,
                 sret_ref, shg_ref, xo_ref, sret_o, shg_o,
                 qs_ref, qd_ref, kb_ref, kd_ref, rv_ref, hv_ref,
                 rgate_ref, hq_ref, hf_ref, hgate_ref, mix_ref, hgt_ref, *, ns, nc, layer_j):
    t = pl.program_id(1)
    tt = ns * nc * CHUNK
    heads = range(HEADS)

    @pl.when(t == 0)
    def _():
        sret_o[...] = sret_ref[...]
        for s in range(ns):
            for h in heads:
                hgt_ref[s, h] = shg_ref[s, h].T

    x = x_ref[...].reshape(tt, D_MODEL)
    hn = _rms(x, ng_ref[...]).astype(BF16)

    def proj(g):
        return _dot(hn, win_ref[:, g * GRP:(g + 1) * GRP])

    cos = jnp.concatenate([cos_ref[...]] * ns, axis=0)
    sin = jnp.concatenate([sin_ref[...]] * ns, axis=0)
    pos = (lax.broadcasted_iota(jnp.int32, (tt, HD), 0) & (CHUNK - 1)).astype(F32)
    scale = HD ** -0.5

    def rope(z, h):
        v = z[:, _hcols(h)]
        return v * cos + pltpu.roll(v, HD // 2, 1) * sin

    z0 = proj(0)
    z1 = proj(1)
    for h in heads:
        q = rope(z0, h)
        qs_ref[:, _hcols(h)] = (q * scale).astype(BF16)
        qd_ref[:, _hcols(h)] = (q * (jnp.exp(RET_LOG_G[h] * (pos + 1.0)) * scale)).astype(BF16)
    z2 = proj(2)
    for h in heads:
        k = rope(z1, h)
        kb_ref[:, _hcols(h)] = k.astype(BF16)
        kd_ref[:, _hcols(h)] = (k * jnp.exp(RET_LOG_G[h] * (CHUNK - 1.0 - pos))).astype(BF16)
    z3 = proj(3)
    rv_ref[...] = z2.astype(BF16)
    z4 = proj(4)
    rgate_ref[...] = _silu(z3)
    z5 = proj(5)
    hq_ref[...] = _silu(z4)
    z6 = proj(6)
    hf_ref[...] = z5
    z7 = proj(7)
    hv_ref[...] = z6.astype(BF16)
    hgate_ref[...] = _silu(z7)

    ri = lax.broadcasted_iota(jnp.int32, (CHUNK, CHUNK), 0)
    ci = lax.broadcasted_iota(jnp.int32, (CHUNK, CHUNK), 1)
    absd = jnp.abs(ri - ci).astype(F32)
    intra = [jnp.exp(RET_LOG_G[h] * absd) for h in heads]
    sdec = [math.exp(RET_LOG_G[h] * CHUNK) for h in heads]

    lbl = lbl_ref[...]
    e = jnp.exp(lbl - jnp.max(lbl, axis=0, keepdims=True))
    sm = e / jnp.sum(e, axis=0, keepdims=True)
    lb_all = jnp.sum(sm[:layer_j + 1], axis=0, keepdims=True)

    nb = CHUNK // SUB
    t_io = lax.broadcasted_iota(jnp.int32, (nb, SUB, HD), 1)
    lane_io = lax.broadcasted_iota(jnp.int32, (nb, SUB, CHUNK), 2)
    blk_io = lax.broadcasted_iota(jnp.int32, (nb, SUB, CHUNK), 0)

    def chunk_body(i, carry):
        streams = []
        for u in range(lanes):
            s, c = (i * lanes + u, 0) if nc == 1 else (u, i)
            r0 = (s * nc + c) * CHUNK
            streams += [(s, r0, h) for h in heads]

        def ld(ref, r0, h):
            return ref[pl.ds(r0, CHUNK), pl.ds(h * HD, HD)]

        ret_states = [sret_o[s, h] for s, _, h in streams]
        hg_states = [hgt_ref[s, h] for s, _, h in streams]

        r_v, r_att, r_inter = [], [], []
        for n, (s, r0, h) in enumerate(streams):
            st = ret_states[n]
            r_v.append(ld(rv_ref, r0, h))
            r_att.append(_dot_nt(ld(qs_ref, r0, h), ld(kb_ref, r0, h)))
            r_inter.append(_dot(ld(qd_ref, r0, h), st.astype(BF16)))
            sret_o[s, h] = st * sdec[h] + _dot_tn(ld(kd_ref, r0, h), r_v[n])

        h_q, h_k, h_b, h_v, h_off, h_inter = [], [], [], [], [], []
        for n, (s, r0, h) in enumerate(streams):
            lb = lb_all[:, _hcols(h)]
            sig = jax.nn.sigmoid(ld(hf_ref, r0, h))
            v = ld(hv_ref, r0, h)
            q = ld(hq_ref, r0, h)
            k = (1.0 - lb) * (1.0 - sig)
            b = _chunk_cumsum(jnp.log(lb + (1.0 - lb) * sig))
            stt = hg_states[n]
            off = []
            for blk in range(1, nb):
                lo = blk * SUB
                rb = b[lo - 1:lo, :]
                qt = (q[lo:lo + SUB] * jnp.exp(b[lo:lo + SUB] - rb)).astype(BF16)
                kt = jnp.concatenate([k[:lo] * jnp.exp(rb - b[:lo]),
                                      jnp.zeros((CHUNK - lo, HD), F32)], axis=0).astype(BF16)
                off.append(_dot_nt(qt, kt))
            h_off.append(off)
            h_inter.append(_dot_nt((q * jnp.exp(b)).astype(BF16), stt.astype(BF16)))
            bend = b[CHUNK - 1:CHUNK, :]
            kd = (k * jnp.exp(bend - b)).astype(BF16)
            hgt_ref[s, h] = stt * jnp.exp(bend) + _dot_tn(v, kd)
            h_q.append(q)
            h_k.append(k)
            h_b.append(b)
            h_v.append(v)

        for n, (s, r0, h) in enumerate(streams):
            o = _dot((r_att[n] * intra[h]).astype(BF16), r_v[n]) + r_inter[n]
            o = _head_norm(o, rg_ref[:, pl.ds(h * HD, HD)]) * ld(rgate_ref, r0, h)
            mix_ref[pl.ds(r0, CHUNK), pl.ds(h * HD, HD)] = o.astype(BF16)

        h_att = []
        for n in range(len(streams)):
            b4 = h_b[n].reshape(nb, SUB, HD)
            q4 = h_q[n].reshape(nb, SUB, HD)
            k4 = h_k[n].reshape(nb, SUB, HD)
            diag = jnp.zeros((nb, SUB, CHUNK), F32)
            for sp in range(SUB):
                arg = jnp.where(t_io >= sp, b4 - b4[:, sp:sp + 1, :], -jnp.inf)
                r = jnp.sum(jnp.exp(arg) * q4 * k4[:, sp:sp + 1, :], axis=-1, keepdims=True)
                diag = jnp.where(lane_io == blk_io * SUB + sp, r, diag)
            diag = diag.reshape(CHUNK, CHUNK)
            parts = [diag[0:SUB]]
            for blk in range(1, nb):
                parts.append(diag[blk * SUB:(blk + 1) * SUB] + h_off[n][blk - 1])
            h_att.append(jnp.concatenate(parts, axis=0).astype(BF16))
        for n, (s, r0, h) in enumerate(streams):
            o = _dot(h_att[n], h_v[n]) + h_inter[n]
            o = _head_norm(o, hg_ref[:, pl.ds(h * HD, HD)]) * ld(hgate_ref, r0, h)
            mix_ref[pl.ds(r0, CHUNK), pl.ds(GRP + h * HD, HD)] = o.astype(BF16)
        return carry

    lanes = MIXER_LANES if ns % MIXER_LANES == 0 else 1
    for i in range(ns * nc // lanes):
        chunk_body(i, 0)

    xo_ref[...] = (x + _dot(mix_ref[...], wout_ref[...])).reshape(xo_ref.shape)

    @pl.when(t == pl.num_programs(1) - 1)
    def _():
        for s in range(ns):
            for h in heads:
                shg_o[s, h] = hgt_ref[s, h].T


def _even_mixer(x, cos, sin, norm_g, w_in, w_out, ret_g, hg_g, lb_logits, s_ret, s_hg, layer_j):
    b, t, _ = x.shape
    ns, nc = _seq_tiling(b, t, MIXER_LANES)
    tt = ns * nc * CHUNK
    grid = (b // ns, t // (nc * CHUNK))
    st_spec = pl.BlockSpec((ns, HEADS, HD, HD), lambda i, j: (i, 0, 0, 0))
    x_spec = pl.BlockSpec((ns, nc * CHUNK, D_MODEL), lambda i, j: (i, j, 0))

    def full(a):
        return _resident(a.shape, lambda i, j: (0,) * a.ndim)

    tab_spec = pl.BlockSpec((nc * CHUNK, HD), lambda i, j: (j, 0))
    args = (x, cos, sin, norm_g.reshape(1, D_MODEL), w_in, w_out, ret_g.reshape(1, GRP),
            hg_g.reshape(1, GRP), lb_logits, s_ret, s_hg)
    in_specs = [x_spec, tab_spec, tab_spec] + [full(a) for a in args[3:9]] + [st_spec, st_spec]
    grp_bf16 = pltpu.VMEM((tt, GRP), BF16)
    grp_f32 = pltpu.VMEM((tt, GRP), F32)
    return pl.pallas_call(
        functools.partial(_even_kernel, ns=ns, nc=nc, layer_j=layer_j),
        grid=grid,
        in_specs=in_specs,
        out_specs=[x_spec, st_spec, st_spec],
        out_shape=[jax.ShapeDtypeStruct(x.shape, F32),
                   jax.ShapeDtypeStruct(s_ret.shape, F32),
                   jax.ShapeDtypeStruct(s_hg.shape, F32)],
        scratch_shapes=[grp_bf16] * 6 + [grp_f32] * 4
        + [pltpu.VMEM((tt, D_MODEL), BF16), pltpu.VMEM((ns, HEADS, HD, HD), F32)],
        compiler_params=pltpu.CompilerParams(
            dimension_semantics=("parallel", "arbitrary"), vmem_limit_bytes=VMEM_LIMIT),
        name="even_mixer",
    )(*args)


def _softplus(x):
    return jnp.maximum(x, 0.0) + jnp.log1p(jnp.exp(-jnp.abs(x)))


def _gelu_tanh(x):
    return 0.5 * x * (1.0 + jnp.tanh(math.sqrt(2.0 / math.pi) * (x + 0.044715 * (x * x * x))))


def _odd_kernel(x_ref, ng_ref, win_ref, wt_ref, wout_ref, cw_ref, lcb_ref, wa_ref, ba_ref,
                wx_ref, bx_ref, lam_ref, dnp_ref, dg_ref, lh_ref, lc_ref, ds_ref, dc_ref,
                xo_ref, lh_o, lc_o, ds_o, dc_o,
                z_ref, gl_ref, sg_ref, bt_ref, gt_ref, gtt_ref, mix_ref, xp_ref, *, ns, nc):
    t = pl.program_id(1)
    tt = ns * nc * CHUNK
    conv_ch = LRU_WIDTH + 3 * GRP
    hist = CONV_W - 1
    top = 8
    heads = range(HEADS)

    @pl.when(t == 0)
    def _():
        lh_o[...] = lh_ref[...]
        lc_o[...] = lc_ref[...]
        ds_o[...] = ds_ref[...]
        dc_o[...] = dc_ref[...]

    x = x_ref[...].reshape(tt, D_MODEL)
    hn = _rms(x, ng_ref[...]).astype(BF16)

    def proj(lo):
        return _dot(hn, win_ref[:, lo:lo + GRP])

    zs = _dot(hn, wt_ref[...])
    z5 = proj(5 * GRP)
    bt_ref[...] = jax.nn.sigmoid(zs)
    g_all = _chunk_cumsum(-jnp.exp(dnp_ref[1:2, :]) * _softplus(zs + dnp_ref[0:1, :]))
    gt_ref[...] = g_all
    for c in range(ns * nc):
        gtt_ref[c] = g_all[c * CHUNK:(c + 1) * CHUNK, :].T
    z1 = proj(GRP)
    sg_ref[...] = _silu(z5)
    z_ref[:, 0:GRP] = proj(0)
    gl_ref[...] = _gelu_tanh(z1)
    for g in range(3):
        z_ref[:, (g + 1) * GRP:(g + 2) * GRP] = proj((g + 2) * GRP)

    sp_lam = _softplus(-lam_ref[...])
    ri = lax.broadcasted_iota(jnp.int32, (CHUNK, CHUNK), 0)
    ci = lax.broadcasted_iota(jnp.int32, (CHUNK, CHUNK), 1)
    rows_w = lax.broadcasted_iota(jnp.int32, (CHUNK, LRU_WIDTH), 0)
    eye_c = (ri == ci).astype(F32)
    sub_bits = DN_SUB.bit_length() - 1
    same_blk = (ri >> sub_bits) == (ci >> sub_bits)
    merge_masks = []
    for lvl in range(sub_bits, CHUNK.bit_length() - 1):
        merge_masks.append(((ri >> (lvl + 1)) == (ci >> (lvl + 1)))
                           & (((ri >> lvl) & 1) == 1) & (((ci >> lvl) & 1) == 0))

    def chunk_body(i, carry):
        lane_seq, lane_rows, lane_chunk, ys = [], [], [], []
        for u in range(lanes):
            s, c = (i * lanes + u, 0) if nc == 1 else (u, i)
            r0 = (s * nc + c) * CHUNK
            rows = pl.ds(r0, CHUNK)
            lane_seq.append(s)
            lane_rows.append(rows)
            lane_chunk.append(s * nc + c)

            xp_ref[u, top - hist:top, 0:LRU_WIDTH] = lc_o[s]
            xp_ref[u, top - hist:top, LRU_WIDTH:conv_ch] = dc_o[s]
            xp_ref[u, top:top + CHUNK, :] = z_ref[rows, :]
            lc_o[s] = xp_ref[u, top + CHUNK - hist:top + CHUNK, 0:LRU_WIDTH]
            dc_o[s] = xp_ref[u, top + CHUNK - hist:top + CHUNK, LRU_WIDTH:conv_ch]
            y = xp_ref[u, top - hist:top - hist + CHUNK, :] * cw_ref[0:1, :]
            for jj in range(1, CONV_W):
                y = y + (xp_ref[u, top - hist + jj:top - hist + jj + CHUNK, :]
                         * cw_ref[jj:jj + 1, :])
            ys.append(y)

        def rg_lru_branch(lane):
            y, s, rows = ys[lane], lane_seq[lane], lane_rows[lane]
            lx = y[:, 0:LRU_WIDTH] + lcb_ref[...]
            xb = lx.astype(BF16)
            r = jax.nn.sigmoid(_dot(xb, wa_ref[...]) + ba_ref[...])
            ig = jax.nn.sigmoid(_dot(xb, wx_ref[...]) + bx_ref[...])
            a = jnp.exp(-LRU_C * r * sp_lam)
            u = jnp.sqrt(1.0 - a * a) * (ig * lx)
            k = 1
            while k < CHUNK:
                m = rows_w >= k
                u = jnp.where(m, a * pltpu.roll(u, k, 0) + u, u)
                a = jnp.where(m, a * pltpu.roll(a, k, 0), a)
                k *= 2
            hseq = a * lh_o[s] + u
            lh_o[s] = hseq[CHUNK - 1:CHUNK]
            mix_ref[rows, 0:LRU_WIDTH] = (gl_ref[rows, :] * hseq).astype(BF16)

        streams = [(u, h) for u in range(lanes) for h in heads]
        ns_ = range(len(streams))
        beta_t = [bt_ref[rows, :] for rows in lane_rows]
        g_t = [gt_ref[rows, :] for rows in lane_rows]
        g_tt = [gtt_ref[c] for c in lane_chunk]
        states = [ds_o[lane_seq[u], h] for u, h in streams]
        kn, both_k, both_s = [], [], []
        for n, (u, h) in enumerate(streams):
            y = ys[u]
            base = LRU_WIDTH + h * HD
            q = _silu(y[:, base:base + HD])
            kk_ = _silu(y[:, base + GRP:base + GRP + HD])
            q = q * lax.rsqrt(jnp.sum(q * q, axis=-1, keepdims=True) + EPS) * (HD ** -0.5)
            kn.append(kk_ * lax.rsqrt(jnp.sum(kk_ * kk_, axis=-1, keepdims=True) + EPS))
            knb = kn[n].astype(BF16)
            qk_in = jnp.concatenate([q.astype(BF16), knb], axis=0)
            both_k.append(_dot_nt(qk_in, knb))
            both_s.append(_dot(qk_in, states[n].astype(BF16)))
        gcol, eg, e_incl, amat, rhs = [], [], [], [], []
        for n, (u, h) in enumerate(streams):
            y = ys[u]
            v = _silu(y[:, LRU_WIDTH + 2 * GRP + h * HD:LRU_WIDTH + 2 * GRP + (h + 1) * HD])
            gcol.append(g_t[u][:, HEADS + h:HEADS + h + 1])
            rel = gcol[n] - g_tt[u][HEADS + h:HEADS + h + 1, :]
            e_strict = jnp.exp(jnp.where(ri > ci, rel, -jnp.inf))
            e_incl.append(jnp.where(ri == ci, 1.0, e_strict))
            beta = beta_t[u][:, h:h + 1]
            eg.append(jnp.exp(gcol[n]))
            amat.append(beta * both_k[n][CHUNK:] * e_strict)
            rhs.append(beta * (v - eg[n] * both_s[n][CHUNK:]))
        dblk = [jnp.where(same_blk, amat[n], 0.0) for n in ns_]
        tinv = [eye_c for _ in ns_]
        for jj in range(DN_SUB - 1):
            for n in ns_:
                colv = jnp.concatenate(
                    [dblk[n][b0:b0 + DN_SUB, b0 + jj:b0 + jj + 1]
                     for b0 in range(0, CHUNK, DN_SUB)], axis=0)
                rowm = jnp.concatenate(
                    [jnp.broadcast_to(tinv[n][b0 + jj:b0 + jj + 1, :], (DN_SUB, CHUNK))
                     for b0 in range(0, CHUNK, DN_SUB)], axis=0)
                tinv[n] = tinv[n] - colv * rowm
        for lane in range(lanes):
            rg_lru_branch(lane)
        tb = [tinv[n].astype(BF16) for n in ns_]
        pend = [[_dot(tb[n], jnp.where(lm, amat[n], 0.0).astype(BF16)) for lm in merge_masks]
                for n in ns_]
        for _ in merge_masks:
            xb_ = [pend[n][0].astype(BF16) for n in ns_]
            tb = [tinv[n].astype(BF16) for n in ns_]
            tinv = [tinv[n] - _dot(xb_[n], tb[n]) for n in ns_]
            pend = [[p - _dot(xb_[n], p.astype(BF16)) for p in pend[n][1:]] for n in ns_]
        w = [_dot(tinv[n].astype(BF16), rhs[n].astype(BF16)).astype(BF16) for n in ns_]
        for n, (u, h) in enumerate(streams):
            s, rows = lane_seq[u], lane_rows[u]
            o = eg[n] * both_s[n][:CHUNK] + _dot((both_k[n][:CHUNK] * e_incl[n]).astype(BF16), w[n])
            gend = gcol[n][CHUNK - 1:CHUNK, :]
            kd = (kn[n] * jnp.exp(gend - gcol[n])).astype(BF16)
            ds_o[s, h] = states[n] * jnp.exp(gend) + _dot_tn(kd, w[n])
            o = _head_norm(o, dg_ref[:, pl.ds(h * HD, HD)]) * sg_ref[rows, pl.ds(h * HD, HD)]
            mix_ref[rows, pl.ds(LRU_WIDTH + h * HD, HD)] = o.astype(BF16)
        return carry

    lanes = ODD_LANES if ns % ODD_LANES == 0 else 1
    for i in range(ns * nc // lanes):
        chunk_body(i, 0)

    xo_ref[...] = (x + _dot(mix_ref[...], wout_ref[...])).reshape(xo_ref.shape)


def _odd_mixer(x, norm_g, w_in, w_tail, w_out, cw, lcb, wa, ba, wx, bx, lam, dnp, dn_g,
               s_lh, s_lc, s_dn, s_dc):
    b, t, _ = x.shape
    ns, nc = _seq_tiling(b, t, ODD_LANES)
    tt = ns * nc * CHUNK
    grid = (b // ns, t // (nc * CHUNK))
    x_spec = pl.BlockSpec((ns, nc * CHUNK, D_MODEL), lambda i, j: (i, j, 0))

    def full(a):
        return _resident(a.shape, lambda i, j: (0,) * a.ndim)

    def st_spec(a):
        return pl.BlockSpec((ns,) + a.shape[1:], lambda i, j: (i,) + (0,) * (a.ndim - 1))

    params = (norm_g.reshape(1, D_MODEL), w_in, w_tail, w_out, cw, lcb, wa, ba, wx, bx, lam, dnp,
              dn_g)
    states = (s_lh, s_lc, s_dn, s_dc)
    grp_f32 = pltpu.VMEM((tt, GRP), F32)
    head_f32 = pltpu.VMEM((tt, HD), F32)
    return pl.pallas_call(
        functools.partial(_odd_kernel, ns=ns, nc=nc),
        grid=grid,
        in_specs=[x_spec] + [full(a) for a in params] + [st_spec(a) for a in states],
        out_specs=[x_spec] + [st_spec(a) for a in states],
        out_shape=[jax.ShapeDtypeStruct(x.shape, F32)]
        + [jax.ShapeDtypeStruct(a.shape, F32) for a in states],
        scratch_shapes=[pltpu.VMEM((tt, LRU_WIDTH + 3 * GRP), F32), grp_f32, grp_f32,
                        head_f32, head_f32, pltpu.VMEM((ns * nc, HD, CHUNK), F32),
                        pltpu.VMEM((tt, D_MODEL), BF16),
                        pltpu.VMEM((ODD_LANES, 8 + CHUNK, LRU_WIDTH + 3 * GRP), F32)],
        compiler_params=pltpu.CompilerParams(
            dimension_semantics=("parallel", "arbitrary"), vmem_limit_bytes=VMEM_LIMIT),
        name="odd_mixer",
    )(x, *params, *states)


def _rope_tables(pos0, n):
    half = HD // 2
    freq = ROPE_BASE ** (-jnp.arange(half, dtype=F32) / half)
    ang_a = (pos0 + CHUNK * jnp.arange(n // CHUNK)).astype(F32)[:, None, None] * freq
    ang_b = jnp.arange(CHUNK, dtype=F32)[None, :, None] * freq
    cos_a, sin_a, cos_b, sin_b = jnp.cos(ang_a), jnp.sin(ang_a), jnp.cos(ang_b), jnp.sin(ang_b)
    cos = (cos_a * cos_b - sin_a * sin_b).reshape(n, half)
    sin = (sin_a * cos_b + cos_a * sin_b).reshape(n, half)
    return jnp.concatenate([cos, cos], axis=-1), jnp.concatenate([-sin, sin], axis=-1)


def _block_diag(w):
    n, bs, _ = w.shape
    eye = jnp.eye(n, dtype=w.dtype)
    return (eye[:, None, :, None] * w[:, :, None, :]).reshape(n * bs, n * bs)


def kernel(x_prompt, x_sample, state_ret, state_hgrn, state_lru_h, state_lru_conv, state_dn,
           state_dn_conv, ffn1_norm, ffn1_w_in, ffn1_w_out, mix_norm, ffn2_norm, ffn2_w_in,
           ffn2_w_out, final_norm, even_w_in, even_w_out, ret_out_norm, hg_out_norm,
           hg_lb_logits, odd_w_in, odd_w_out, lru_conv_w, lru_conv_b, lru_w_a, lru_b_a, lru_w_x,
           lru_b_x, lru_lambda, dn_conv_w, dn_a_log, dn_dt_bias, dn_out_norm):
    depth = ffn1_norm.shape[0]
    bp, tp, _ = x_prompt.shape
    bs, ts, _ = x_sample.shape
    past_len = 2048
    tabs = (_rope_tables(0, tp), _rope_tables(past_len, ts))
    xs = [x_prompt, x_sample]
    nb = (bp, bs)
    outs = {k: ([], []) for k in ("ret", "hg", "lh", "lc", "dn", "dc")}

    ffn_sets = []
    for l in range(depth):
        ffn_sets += [(ffn1_norm[l], ffn1_w_in, ffn1_w_out, l),
                     (ffn2_norm[l], ffn2_w_in, ffn2_w_out, l)]
    ffn_w = [(ffn1_w_in[0].astype(BF16), ffn1_w_out[0].astype(BF16))]

    def run_ffn(apply_final, mixer_casts=()):
        k = len(ffn_w) - 1
        casts = []
        if k + 1 < len(ffn_sets):
            _, nwi, nwo, nl = ffn_sets[k + 1]
            casts = [(nwi, nl, FFN_CAST_ROWS, nwi.shape[-1]),
                     (nwo, nl, FFN_CAST_ROWS, nwo.shape[-1])]
        casts += list(mixer_casts)
        xs[0], xs[1], cast = _ffn(xs[0], xs[1], ffn_sets[k][0], ffn_w[k][0], ffn_w[k][1],
                                  final_norm, apply_final, casts)
        ffn_w.append(tuple(cast[:2]) if k + 1 < len(ffn_sets) else ())
        return cast[len(cast) - len(mixer_casts):]

    for l in range(depth):
        j = l // 2
        last = l == depth - 1
        if l % 2 == 0:
            w_in, w_out = run_ffn(False, [(even_w_in, j, None, EVEN_IN),
                                          (even_w_out, j, None, D_MODEL)])
            for g in range(2):
                if g == 0:
                    s_ret = jnp.zeros((bp, HEADS, HD, HD), F32)
                    s_hg = jnp.zeros((bp, HEADS, HD, HD), F32)
                else:
                    s_ret, s_hg = state_ret[j], state_hgrn[j]
                xs[g], n_ret, n_hg = _even_mixer(
                    xs[g], tabs[g][0], tabs[g][1], mix_norm[l], w_in, w_out, ret_out_norm[j],
                    hg_out_norm[j], hg_lb_logits, s_ret, s_hg, j)
                outs["ret"][g].append(n_ret)
                outs["hg"][g].append(n_hg)
        else:
            w_in, w_out = run_ffn(False, [(odd_w_in, j, None, 6 * GRP),
                                          (odd_w_out, j, None, D_MODEL)])
            n_tail = odd_w_in.shape[-1] - 6 * GRP
            w_tail = jnp.pad(odd_w_in[j][:, 6 * GRP:], ((0, 0), (0, HD - n_tail))).astype(BF16)
            cw = jnp.concatenate([lru_conv_w[j], dn_conv_w[j]], axis=-1)
            wa = _block_diag(lru_w_a[j]).astype(BF16)
            wx = _block_diag(lru_w_x[j]).astype(BF16)
            dnp = jnp.zeros((2, HD), F32)
            dnp = dnp.at[0, HEADS:2 * HEADS].set(dn_dt_bias[j])
            dnp = dnp.at[1, HEADS:2 * HEADS].set(dn_a_log[j])
            for g in range(2):
                if g == 0:
                    s_lh = jnp.zeros((bp, 1, LRU_WIDTH), F32)
                    s_lc = jnp.zeros((bp, CONV_W - 1, LRU_WIDTH), F32)
                    s_dn = jnp.zeros((bp, HEADS, HD, HD), F32)
                    s_dc = jnp.zeros((bp, CONV_W - 1, 3 * GRP), F32)
                else:
                    s_lh = state_lru_h[j].reshape(bs, 1, LRU_WIDTH)
                    s_lc, s_dn, s_dc = state_lru_conv[j], state_dn[j], state_dn_conv[j]
                xs[g], n_lh, n_lc, n_dn, n_dc = _odd_mixer(
                    xs[g], mix_norm[l], w_in, w_tail, w_out, cw, lru_conv_b[j].reshape(1, -1), wa,
                    lru_b_a[j].reshape(1, -1), wx, lru_b_x[j].reshape(1, -1),
                    lru_lambda[j].reshape(1, -1), dnp, dn_out_norm[j].reshape(1, -1),
                    s_lh, s_lc, s_dn, s_dc)
                outs["lh"][g].append(n_lh.reshape(nb[g], LRU_WIDTH))
                outs["lc"][g].append(n_lc)
                outs["dn"][g].append(n_dn)
                outs["dc"][g].append(n_dc)
        run_ffn(last)

    res = [xs[0], xs[1]]
    for k in ("ret", "hg", "lh", "lc", "dn", "dc"):
        for g in range(2):
            res.append(jnp.stack(outs[k][g]))
    return tuple(res)
```

```python
import functools
import math

import jax
import jax.numpy as jnp
from jax import lax
from jax.experimental import pallas as pl
from jax.experimental.pallas import tpu as pltpu

F32 = jnp.float32
BF16 = jnp.bfloat16

D_MODEL = 1024
FF_DIM = 2816
EPS = 1e-6
CHUNK = 64
CONV_W = 4
HEADS = 4
HD = 128
ROPE_BASE = 10000.0
LRU_WIDTH = 512
LRU_C = 8.0
GRP = HEADS * HD
EVEN_IN = 8 * GRP
SUB = 8
DN_SUB = 4

VMEM_LIMIT = 56 * 1024 * 1024

FFN_TM = 512
FFN_TF = 256
FFN_ROW_SPLIT = 2
FFN_CAST_ROWS = 16
SEQ_TILE = 512
MIXER_LANES = 1
ODD_LANES = 2

RET_LOG_G = [math.log1p(-(2.0 ** (-5.0 - h))) for h in range(HEADS)]


def _rms(x, g):
    return x * lax.rsqrt(jnp.mean(x * x, axis=-1, keepdims=True) + EPS) * g


def _silu(x):
    return x * jax.nn.sigmoid(x)


def _dot(a, b):
    return jnp.dot(a, b, preferred_element_type=F32)


def _dot_nt(a, b):
    return lax.dot_general(a, b, (((1,), (1,)), ((), ())), preferred_element_type=F32)


def _dot_tn(a, b):
    return lax.dot_general(a, b, (((0,), (0,)), ((), ())), preferred_element_type=F32)


def _head_norm(o, g):
    return o * lax.rsqrt(jnp.mean(o * o, axis=-1, keepdims=True) + EPS) * g


def _hcols(h):
    return slice(h * HD, (h + 1) * HD)


def _chunk_cumsum(x):
    pos = lax.broadcasted_iota(jnp.int32, x.shape, 0) & (CHUNK - 1)
    k = 1
    while k < CHUNK:
        x = jnp.where(pos >= k, x + pltpu.roll(x, k, 0), x)
        k *= 2
    return x


def _resident(block_shape, index_map):
    return pl.BlockSpec(block_shape, index_map, pipeline_mode=pl.Buffered(1))


def _ffn_kernel(*refs, n_first, apply_final, n_casts):
    xa_ref, xb_ref, g_ref, win_ref, wout_ref, fin_ref = refs[:6]
    cast_in = refs[6:6 + n_casts]
    oa_ref, ob_ref = refs[6 + n_casts:8 + n_casts]
    cast_out = refs[8 + n_casts:8 + 2 * n_casts]
    act_ref = refs[8 + 2 * n_casts]
    for src, dst in zip(cast_in, cast_out):
        dst[...] = src[...].astype(BF16)
    i = pl.program_id(0)

    def tile(x_ref, o_ref):
        tm = x_ref.shape[0]
        halves = [slice(r, r + tm // FFN_ROW_SPLIT) for r in range(0, tm, tm // FFN_ROW_SPLIT)]
        hs = [_rms(x_ref[rows, :], g_ref[...]).astype(BF16) for rows in halves]
        for c in range(FF_DIM // FFN_TF):
            lo = c * FFN_TF
            for rows, h in zip(halves, hs):
                gate = _dot(h, win_ref[:, lo:lo + FFN_TF])
                up = _dot(h, win_ref[:, FF_DIM + lo:FF_DIM + lo + FFN_TF])
                act_ref[rows, lo:lo + FFN_TF] = (_silu(gate) * up).astype(BF16)
        for rows in halves:
            y = x_ref[rows, :] + 0.5 * _dot(act_ref[rows, :], wout_ref[...])
            if apply_final:
                y = _rms(y, fin_ref[...])
            o_ref[rows, :] = y

    @pl.when(i < n_first)
    def _():
        tile(xa_ref, oa_ref)

    @pl.when(i >= n_first)
    def _():
        tile(xb_ref, ob_ref)


def _ffn(xa, xb, norm_g, w_in, w_out, final_g, apply_final, casts=()):
    sa, sb = xa.shape, xb.shape
    xa2, xb2 = xa.reshape(-1, D_MODEL), xb.reshape(-1, D_MODEL)
    tm = min(FFN_TM, xa2.shape[0], xb2.shape[0])
    na, nb = xa2.shape[0] // tm, xb2.shape[0] // tm
    x_blk = (tm, D_MODEL)
    in_specs = [
        pl.BlockSpec(x_blk, lambda i: (jnp.minimum(i, na - 1), 0)),
        pl.BlockSpec(x_blk, lambda i: (jnp.maximum(i - na, 0), 0)),
        _resident((1, D_MODEL), lambda i: (0, 0)),
        _resident((D_MODEL, 2 * FF_DIM), lambda i: (0, 0)),
        _resident((FF_DIM, D_MODEL), lambda i: (0, 0)),
        _resident((1, D_MODEL), lambda i: (0, 0)),
    ]
    out_specs = [
        pl.BlockSpec(x_blk, lambda i: (jnp.minimum(i, na - 1), 0)),
        pl.BlockSpec(x_blk, lambda i: (jnp.maximum(i - na, 0), 0)),
    ]
    out_shape = [jax.ShapeDtypeStruct(xa2.shape, F32), jax.ShapeDtypeStruct(xb2.shape, F32)]
    args = [xa2, xb2, norm_g.reshape(1, D_MODEL), w_in, w_out, final_g.reshape(1, D_MODEL)]
    cast_in_specs, cast_out_specs = [], []
    for arr, layer, row_blocks, c in casts:
        r = arr.shape[1]
        ncol = na // (row_blocks or na)
        nrow = na // ncol
        rows = r // nrow

        def blk(i, ncol=ncol):
            i = jnp.minimum(i, na - 1)
            return i // ncol, i % ncol

        def blk_in(i, blk=blk, first=layer * nrow):
            rb, cb = blk(i)
            return first + rb, cb

        cast_in_specs.append(pl.BlockSpec((rows, c // ncol), blk_in))
        cast_out_specs.append(pl.BlockSpec((rows, c // ncol), blk))
        out_shape.append(jax.ShapeDtypeStruct((r, c), BF16))
        args.append(arr.reshape(-1, arr.shape[-1]))
    outs = pl.pallas_call(
        functools.partial(_ffn_kernel, n_first=na, apply_final=apply_final,
                          n_casts=len(casts)),
        grid=(na + nb,),
        in_specs=in_specs + cast_in_specs,
        out_specs=out_specs + cast_out_specs,
        out_shape=out_shape,
        scratch_shapes=[pltpu.VMEM((tm, FF_DIM), BF16)],
        compiler_params=pltpu.CompilerParams(
            dimension_semantics=("arbitrary",), vmem_limit_bytes=VMEM_LIMIT),
        name="ffn",
    )(*args)
    return outs[0].reshape(sa), outs[1].reshape(sb), list(outs[2:])


def _seq_tiling(b, t, lanes=1):
    if t >= SEQ_TILE:
        lanes = lanes if b % lanes == 0 else 1
        return lanes, SEQ_TILE // CHUNK // lanes
    return min(b, SEQ_TILE // t), t // CHUNK


def _even_kernel(x_ref, cos_ref, sin_ref, ng_ref, win_ref, wout_ref, rg_ref, hg_ref, lbl_ref,
                 sret_ref, shg_ref, xo_ref, sret_o, shg_o,
                 qs_ref, qd_ref, kb_ref, kd_ref, rv_ref, hv_ref,
                 rgate_ref, hq_ref, hf_ref, hgate_ref, mix_ref, hgt_ref, *, ns, nc, layer_j):
    t = pl.program_id(1)
    tt = ns * nc * CHUNK
    heads = range(HEADS)

    @pl.when(t == 0)
    def _():
        sret_o[...] = sret_ref[...]
        for s in range(ns):
            for h in heads:
                hgt_ref[s, h] = shg_ref[s, h].T

    x = x_ref[...].reshape(tt, D_MODEL)
    hn = _rms(x, ng_ref[...]).astype(BF16)

    def proj(g):
        return _dot(hn, win_ref[:, g * GRP:(g + 1) * GRP])

    cos = jnp.concatenate([cos_ref[...]] * ns, axis=0)
    sin = jnp.concatenate([sin_ref[...]] * ns, axis=0)
    pos = (lax.broadcasted_iota(jnp.int32, (tt, HD), 0) & (CHUNK - 1)).astype(F32)
    scale = HD ** -0.5

    def rope(z, h):
        v = z[:, _hcols(h)]
        return v * cos + pltpu.roll(v, HD // 2, 1) * sin

    z0 = proj(0)
    z1 = proj(1)
    for h in heads:
        q = rope(z0, h)
        qs_ref[:, _hcols(h)] = (q * scale).astype(BF16)
        qd_ref[:, _hcols(h)] = (q * (jnp.exp(RET_LOG_G[h] * (pos + 1.0)) * scale)).astype(BF16)
    z2 = proj(2)
    for h in heads:
        k = rope(z1, h)
        kb_ref[:, _hcols(h)] = k.astype(BF16)
        kd_ref[:, _hcols(h)] = (k * jnp.exp(RET_LOG_G[h] * (CHUNK - 1.0 - pos))).astype(BF16)
    z3 = proj(3)
    rv_ref[...] = z2.astype(BF16)
    z4 = proj(4)
    rgate_ref[...] = _silu(z3)
    z5 = proj(5)
    hq_ref[...] = _silu(z4)
    z6 = proj(6)
    hf_ref[...] = z5
    z7 = proj(7)
    hv_ref[...] = z6.astype(BF16)
    hgate_ref[...] = _silu(z7)

    ri = lax.broadcasted_iota(jnp.int32, (CHUNK, CHUNK), 0)
    ci = lax.broadcasted_iota(jnp.int32, (CHUNK, CHUNK), 1)
    absd = jnp.abs(ri - ci).astype(F32)
    intra = [jnp.exp(RET_LOG_G[h] * absd) for h in heads]
    sdec = [math.exp(RET_LOG_G[h] * CHUNK) for h in heads]

    lbl = lbl_ref[...]
    e = jnp.exp(lbl - jnp.max(lbl, axis=0, keepdims=True))
    sm = e / jnp.sum(e, axis=0, keepdims=True)
    lb_all = jnp.sum(sm[:layer_j + 1], axis=0, keepdims=True)

    nb = CHUNK // SUB
    t_io = lax.broadcasted_iota(jnp.int32, (nb, SUB, HD), 1)
    lane_io = lax.broadcasted_iota(jnp.int32, (nb, SUB, CHUNK), 2)
    blk_io = lax.broadcasted_iota(jnp.int32, (nb, SUB, CHUNK), 0)

    def chunk_body(i, carry):
        streams = []
        for u in range(lanes):
            s, c = (i * lanes + u, 0) if nc == 1 else (u, i)
            r0 = (s * nc + c) * CHUNK
            streams += [(s, r0, h) for h in heads]

        def ld(ref, r0, h):
            return ref[pl.ds(r0, CHUNK), pl.ds(h * HD, HD)]

        ret_states = [sret_o[s, h] for s, _, h in streams]
        hg_states = [hgt_ref[s, h] for s, _, h in streams]

        r_v, r_att, r_inter = [], [], []
        for n, (s, r0, h) in enumerate(streams):
            st = ret_states[n]
            r_v.append(ld(rv_ref, r0, h))
            r_att.append(_dot_nt(ld(qs_ref, r0, h), ld(kb_ref, r0, h)))
            r_inter.append(_dot(ld(qd_ref, r0, h), st.astype(BF16)))
            sret_o[s, h] = st * sdec[h] + _dot_tn(ld(kd_ref, r0, h), r_v[n])

        h_q, h_k, h_b, h_v, h_off, h_inter = [], [], [], [], [], []
        for n, (s, r0, h) in enumerate(streams):
            lb = lb_all[:, _hcols(h)]
            sig = jax.nn.sigmoid(ld(hf_ref, r0, h))
            v = ld(hv_ref, r0, h)
            q = ld(hq_ref, r0, h)
            k = (1.0 - lb) * (1.0 - sig)
            b = _chunk_cumsum(jnp.log(lb + (1.0 - lb) * sig))
            stt = hg_states[n]
            off = []
            for blk in range(1, nb):
                lo = blk * SUB
                rb = b[lo - 1:lo, :]
                qt = (q[lo:lo + SUB] * jnp.exp(b[lo:lo + SUB] - rb)).astype(BF16)
                kt = jnp.concatenate([k[:lo] * jnp.exp(rb - b[:lo]),
                                      jnp.zeros((CHUNK - lo, HD), F32)], axis=0).astype(BF16)
                off.append(_dot_nt(qt, kt))
            h_off.append(off)
            h_inter.append(_dot_nt((q * jnp.exp(b)).astype(BF16), stt.astype(BF16)))
            bend = b[CHUNK - 1:CHUNK, :]
            kd = (k * jnp.exp(bend - b)).astype(BF16)
            hgt_ref[s, h] = stt * jnp.exp(bend) + _dot_tn(v, kd)
            h_q.append(q)
            h_k.append(k)
            h_b.append(b)
            h_v.append(v)

        for n, (s, r0, h) in enumerate(streams):
            o = _dot((r_att[n] * intra[h]).astype(BF16), r_v[n]) + r_inter[n]
            o = _head_norm(o, rg_ref[:, pl.ds(h * HD, HD)]) * ld(rgate_ref, r0, h)
            mix_ref[pl.ds(r0, CHUNK), pl.ds(h * HD, HD)] = o.astype(BF16)

        h_att = []
        for n in range(len(streams)):
            b4 = h_b[n].reshape(nb, SUB, HD)
            q4 = h_q[n].reshape(nb, SUB, HD)
            k4 = h_k[n].reshape(nb, SUB, HD)
            diag = jnp.zeros((nb, SUB, CHUNK), F32)
            for sp in range(SUB):
                arg = jnp.where(t_io >= sp, b4 - b4[:, sp:sp + 1, :], -jnp.inf)
                r = jnp.sum(jnp.exp(arg) * q4 * k4[:, sp:sp + 1, :], axis=-1, keepdims=True)
                diag = jnp.where(lane_io == blk_io * SUB + sp, r, diag)
            diag = diag.reshape(CHUNK, CHUNK)
            parts = [diag[0:SUB]]
            for blk in range(1, nb):
                parts.append(diag[blk * SUB:(blk + 1) * SUB] + h_off[n][blk - 1])
            h_att.append(jnp.concatenate(parts, axis=0).astype(BF16))
        for n, (s, r0, h) in enumerate(streams):
            o = _dot(h_att[n], h_v[n]) + h_inter[n]
            o = _head_norm(o, hg_ref[:, pl.ds(h * HD, HD)]) * ld(hgate_ref, r0, h)
            mix_ref[pl.ds(r0, CHUNK), pl.ds(GRP + h * HD, HD)] = o.astype(BF16)
        return carry

    lanes = MIXER_LANES if ns % MIXER_LANES == 0 else 1
    for i in range(ns * nc // lanes):
        chunk_body(i, 0)

    xo_ref[...] = (x + _dot(mix_ref[...], wout_ref[...])).reshape(xo_ref.shape)

    @pl.when(t == pl.num_programs(1) - 1)
    def _():
        for s in range(ns):
            for h in heads:
                shg_o[s, h] = hgt_ref[s, h].T


def _even_mixer(x, cos, sin, norm_g, w_in, w_out, ret_g, hg_g, lb_logits, s_ret, s_hg, layer_j):
    b, t, _ = x.shape
    ns, nc = _seq_tiling(b, t, MIXER_LANES)
    tt = ns * nc * CHUNK
    grid = (b // ns, t // (nc * CHUNK))
    st_spec = pl.BlockSpec((ns, HEADS, HD, HD), lambda i, j: (i, 0, 0, 0))
    x_spec = pl.BlockSpec((ns, nc * CHUNK, D_MODEL), lambda i, j: (i, j, 0))

    def full(a):
        return _resident(a.shape, lambda i, j: (0,) * a.ndim)

    tab_spec = pl.BlockSpec((nc * CHUNK, HD), lambda i, j: (j, 0))
    args = (x, cos, sin, norm_g.reshape(1, D_MODEL), w_in, w_out, ret_g.reshape(1, GRP),
            hg_g.reshape(1, GRP), lb_logits, s_ret, s_hg)
    in_specs = [x_spec, tab_spec, tab_spec] + [full(a) for a in args[3:9]] + [st_spec, st_spec]
    grp_bf16 = pltpu.VMEM((tt, GRP), BF16)
    grp_f32 = pltpu.VMEM((tt, GRP), F32)
    return pl.pallas_call(
        functools.partial(_even_kernel, ns=ns, nc=nc, layer_j=layer_j),
        grid=grid,
        in_specs=in_specs,
        out_specs=[x_spec, st_spec, st_spec],
        out_shape=[jax.ShapeDtypeStruct(x.shape, F32),
                   jax.ShapeDtypeStruct(s_ret.shape, F32),
                   jax.ShapeDtypeStruct(s_hg.shape, F32)],
        scratch_shapes=[grp_bf16] * 6 + [grp_f32] * 4
        + [pltpu.VMEM((tt, D_MODEL), BF16), pltpu.VMEM((ns, HEADS, HD, HD), F32)],
        compiler_params=pltpu.CompilerParams(
            dimension_semantics=("parallel", "arbitrary"), vmem_limit_bytes=VMEM_LIMIT),
        name="even_mixer",
    )(*args)


def _softplus(x):
    return jnp.maximum(x, 0.0) + jnp.log1p(jnp.exp(-jnp.abs(x)))


def _gelu_tanh(x):
    return 0.5 * x * (1.0 + jnp.tanh(math.sqrt(2.0 / math.pi) * (x + 0.044715 * (x * x * x))))


def _odd_kernel(x_ref, ng_ref, win_ref, wt_ref, wout_ref, cw_ref, lcb_ref, wa_ref, ba_ref,
                wx_ref, bx_ref, lam_ref, dnp_ref, dg_ref, lh_ref, lc_ref, ds_ref, dc_ref,
                xo_ref, lh_o, lc_o, ds_o, dc_o,
                z_ref, gl_ref, sg_ref, bt_ref, gt_ref, gtt_ref, mix_ref, xp_ref, *, ns, nc):
    t = pl.program_id(1)
    tt = ns * nc * CHUNK
    conv_ch = LRU_WIDTH + 3 * GRP
    hist = CONV_W - 1
    top = 8
    heads = range(HEADS)

    @pl.when(t == 0)
    def _():
        lh_o[...] = lh_ref[...]
        lc_o[...] = lc_ref[...]
        ds_o[...] = ds_ref[...]
        dc_o[...] = dc_ref[...]

    x = x_ref[...].reshape(tt, D_MODEL)
    hn = _rms(x, ng_ref[...]).astype(BF16)

    def proj(lo):
        return _dot(hn, win_ref[:, lo:lo + GRP])

    zs = _dot(hn, wt_ref[...])
    z5 = proj(5 * GRP)
    bt_ref[...] = jax.nn.sigmoid(zs)
    g_all = _chunk_cumsum(-jnp.exp(dnp_ref[1:2, :]) * _softplus(zs + dnp_ref[0:1, :]))
    gt_ref[...] = g_all
    for c in range(ns * nc):
        gtt_ref[c] = g_all[c * CHUNK:(c + 1) * CHUNK, :].T
    z1 = proj(GRP)
    sg_ref[...] = _silu(z5)
    z_ref[:, 0:GRP] = proj(0)
    gl_ref[...] = _gelu_tanh(z1)
    for g in range(3):
        z_ref[:, (g + 1) * GRP:(g + 2) * GRP] = proj((g + 2) * GRP)

    sp_lam = _softplus(-lam_ref[...])
    ri = lax.broadcasted_iota(jnp.int32, (CHUNK, CHUNK), 0)
    ci = lax.broadcasted_iota(jnp.int32, (CHUNK, CHUNK), 1)
    rows_w = lax.broadcasted_iota(jnp.int32, (CHUNK, LRU_WIDTH), 0)
    eye_c = (ri == ci).astype(F32)
    sub_bits = DN_SUB.bit_length() - 1
    same_blk = (ri >> sub_bits) == (ci >> sub_bits)
    merge_masks = []
    for lvl in range(sub_bits, CHUNK.bit_length() - 1):
        merge_masks.append(((ri >> (lvl + 1)) == (ci >> (lvl + 1)))
                           & (((ri >> lvl) & 1) == 1) & (((ci >> lvl) & 1) == 0))

    def chunk_body(i, carry):
        lane_seq, lane_rows, lane_chunk, ys = [], [], [], []
        for u in range(lanes):
            s, c = (i * lanes + u, 0) if nc == 1 else (u, i)
            r0 = (s * nc + c) * CHUNK
            rows = pl.ds(r0, CHUNK)
            lane_seq.append(s)
            lane_rows.append(rows)
            lane_chunk.append(s * nc + c)

            xp_ref[u, top - hist:top, 0:LRU_WIDTH] = lc_o[s]
            xp_ref[u, top - hist:top, LRU_WIDTH:conv_ch] = dc_o[s]
            xp_ref[u, top:top + CHUNK, :] = z_ref[rows, :]
            lc_o[s] = xp_ref[u, top + CHUNK - hist:top + CHUNK, 0:LRU_WIDTH]
            dc_o[s] = xp_ref[u, top + CHUNK - hist:top + CHUNK, LRU_WIDTH:conv_ch]
            y = xp_ref[u, top - hist:top - hist + CHUNK, :] * cw_ref[0:1, :]
            for jj in range(1, CONV_W):
                y = y + (xp_ref[u, top - hist + jj:top - hist + jj + CHUNK, :]
                         * cw_ref[jj:jj + 1, :])
            ys.append(y)

        def rg_lru_branch(lane):
            y, s, rows = ys[lane], lane_seq[lane], lane_rows[lane]
            lx = y[:, 0:LRU_WIDTH] + lcb_ref[...]
            xb = lx.astype(BF16)
            r = jax.nn.sigmoid(_dot(xb, wa_ref[...]) + ba_ref[...])
            ig = jax.nn.sigmoid(_dot(xb, wx_ref[...]) + bx_ref[...])
            a = jnp.exp(-LRU_C * r * sp_lam)
            u = jnp.sqrt(1.0 - a * a) * (ig * lx)
            k = 1
            while k < CHUNK:
                m = rows_w >= k
                u = jnp.where(m, a * pltpu.roll(u, k, 0) + u, u)
                a = jnp.where(m, a * pltpu.roll(a, k, 0), a)
                k *= 2
            hseq = a * lh_o[s] + u
            lh_o[s] = hseq[CHUNK - 1:CHUNK]
            mix_ref[rows, 0:LRU_WIDTH] = (gl_ref[rows, :] * hseq).astype(BF16)

        streams = [(u, h) for u in range(lanes) for h in heads]
        ns_ = range(len(streams))
        beta_t = [bt_ref[rows, :] for rows in lane_rows]
        g_t = [gt_ref[rows, :] for rows in lane_rows]
        g_tt = [gtt_ref[c] for c in lane_chunk]
        states = [ds_o[lane_seq[u], h] for u, h in streams]
        kn, both_k, both_s = [], [], []
        for n, (u, h) in enumerate(streams):
            y = ys[u]
            base = LRU_WIDTH + h * HD
            q = _silu(y[:, base:base + HD])
            kk_ = _silu(y[:, base + GRP:base + GRP + HD])
            q = q * lax.rsqrt(jnp.sum(q * q, axis=-1, keepdims=True) + EPS) * (HD ** -0.5)
            kn.append(kk_ * lax.rsqrt(jnp.sum(kk_ * kk_, axis=-1, keepdims=True) + EPS))
            knb = kn[n].astype(BF16)
            qk_in = jnp.concatenate([q.astype(BF16), knb], axis=0)
            both_k.append(_dot_nt(qk_in, knb))
            both_s.append(_dot(qk_in, states[n].astype(BF16)))
        gcol, eg, e_incl, amat, rhs = [], [], [], [], []
        for n, (u, h) in enumerate(streams):
            y = ys[u]
            v = _silu(y[:, LRU_WIDTH + 2 * GRP + h * HD:LRU_WIDTH + 2 * GRP + (h + 1) * HD])
            gcol.append(g_t[u][:, HEADS + h:HEADS + h + 1])
            rel = gcol[n] - g_tt[u][HEADS + h:HEADS + h + 1, :]
            e_strict = jnp.exp(jnp.where(ri > ci, rel, -jnp.inf))
            e_incl.append(jnp.where(ri == ci, 1.0, e_strict))
            beta = beta_t[u][:, h:h + 1]
            eg.append(jnp.exp(gcol[n]))
            amat.append(beta * both_k[n][CHUNK:] * e_strict)
            rhs.append(beta * (v - eg[n] * both_s[n][CHUNK:]))
        dblk = [jnp.where(same_blk, amat[n], 0.0) for n in ns_]
        tinv = [eye_c for _ in ns_]
        for jj in range(DN_SUB - 1):
            for n in ns_:
                colv = jnp.concatenate(
                    [dblk[n][b0:b0 + DN_SUB, b0 + jj:b0 + jj + 1]
                     for b0 in range(0, CHUNK, DN_SUB)], axis=0)
                rowm = jnp.concatenate(
                    [jnp.broadcast_to(tinv[n][b0 + jj:b0 + jj + 1, :], (DN_SUB, CHUNK))
                     for b0 in range(0, CHUNK, DN_SUB)], axis=0)
                tinv[n] = tinv[n] - colv * rowm
        for lane in range(lanes):
            rg_lru_branch(lane)
        tb = [tinv[n].astype(BF16) for n in ns_]
        pend = [[_dot(tb[n], jnp.where(lm, amat[n], 0.0).astype(BF16)) for lm in merge_masks]
                for n in ns_]
        for _ in merge_masks:
            xb_ = [pend[n][0].astype(BF16) for n in ns_]
            tb = [tinv[n].astype(BF16) for n in ns_]
            tinv = [tinv[n] - _dot(xb_[n], tb[n]) for n in ns_]
            pend = [[p - _dot(xb_[n], p.astype(BF16)) for p in pend[n][1:]] for n in ns_]
        w = [_dot(tinv[n].astype(BF16), rhs[n].astype(BF16)).astype(BF16) for n in ns_]
        for n, (u, h) in enumerate(streams):
            s, rows = lane_seq[u], lane_rows[u]
            o = eg[n] * both_s[n][:CHUNK] + _dot((both_k[n][:CHUNK] * e_incl[n]).astype(BF16), w[n])
            gend = gcol[n][CHUNK - 1:CHUNK, :]
            kd = (kn[n] * jnp.exp(gend - gcol[n])).astype(BF16)
            ds_o[s, h] = states[n] * jnp.exp(gend) + _dot_tn(kd, w[n])
            o = _head_norm(o, dg_ref[:, pl.ds(h * HD, HD)]) * sg_ref[rows, pl.ds(h * HD, HD)]
            mix_ref[rows, pl.ds(LRU_WIDTH + h * HD, HD)] = o.astype(BF16)
        return carry

    lanes = ODD_LANES if ns % ODD_LANES == 0 else 1
    for i in range(ns * nc // lanes):
        chunk_body(i, 0)

    xo_ref[...] = (x + _dot(mix_ref[...], wout_ref[...])).reshape(xo_ref.shape)


def _odd_mixer(x, norm_g, w_in, w_tail, w_out, cw, lcb, wa, ba, wx, bx, lam, dnp, dn_g,
               s_lh, s_lc, s_dn, s_dc):
    b, t, _ = x.shape
    ns, nc = _seq_tiling(b, t, ODD_LANES)
    tt = ns * nc * CHUNK
    grid = (b // ns, t // (nc * CHUNK))
    x_spec = pl.BlockSpec((ns, nc * CHUNK, D_MODEL), lambda i, j: (i, j, 0))

    def full(a):
        return _resident(a.shape, lambda i, j: (0,) * a.ndim)

    def st_spec(a):
        return pl.BlockSpec((ns,) + a.shape[1:], lambda i, j: (i,) + (0,) * (a.ndim - 1))

    params = (norm_g.reshape(1, D_MODEL), w_in, w_tail, w_out, cw, lcb, wa, ba, wx, bx, lam, dnp,
              dn_g)
    states = (s_lh, s_lc, s_dn, s_dc)
    grp_f32 = pltpu.VMEM((tt, GRP), F32)
    head_f32 = pltpu.VMEM((tt, HD), F32)
    return pl.pallas_call(
        functools.partial(_odd_kernel, ns=ns, nc=nc),
        grid=grid,
        in_specs=[x_spec] + [full(a) for a in params] + [st_spec(a) for a in states],
        out_specs=[x_spec] + [st_spec(a) for a in states],
        out_shape=[jax.ShapeDtypeStruct(x.shape, F32)]
        + [jax.ShapeDtypeStruct(a.shape, F32) for a in states],
        scratch_shapes=[pltpu.VMEM((tt, LRU_WIDTH + 3 * GRP), F32), grp_f32, grp_f32,
                        head_f32, head_f32, pltpu.VMEM((ns * nc, HD, CHUNK), F32),
                        pltpu.VMEM((tt, D_MODEL), BF16),
                        pltpu.VMEM((ODD_LANES, 8 + CHUNK, LRU_WIDTH + 3 * GRP), F32)],
        compiler_params=pltpu.CompilerParams(
            dimension_semantics=("parallel", "arbitrary"), vmem_limit_bytes=VMEM_LIMIT),
        name="odd_mixer",
    )(x, *params, *states)


def _rope_tables(pos0, n):
    half = HD // 2
    freq = ROPE_BASE ** (-jnp.arange(half, dtype=F32) / half)
    ang_a = (pos0 + CHUNK * jnp.arange(n // CHUNK)).astype(F32)[:, None, None] * freq
    ang_b = jnp.arange(CHUNK, dtype=F32)[None, :, None] * freq
    cos_a, sin_a, cos_b, sin_b = jnp.cos(ang_a), jnp.sin(ang_a), jnp.cos(ang_b), jnp.sin(ang_b)
    cos = (cos_a * cos_b - sin_a * sin_b).reshape(n, half)
    sin = (sin_a * cos_b + cos_a * sin_b).reshape(n, half)
    return jnp.concatenate([cos, cos], axis=-1), jnp.concatenate([-sin, sin], axis=-1)


def _block_diag(w):
    n, bs, _ = w.shape
    eye = jnp.eye(n, dtype=w.dtype)
    return (eye[:, None, :, None] * w[:, :, None, :]).reshape(n * bs, n * bs)


def kernel(x_prompt, x_sample, state_ret, state_hgrn, state_lru_h, state_lru_conv, state_dn,
           state_dn_conv, ffn1_norm, ffn1_w_in, ffn1_w_out, mix_norm, ffn2_norm, ffn2_w_in,
           ffn2_w_out, final_norm, even_w_in, even_w_out, ret_out_norm, hg_out_norm,
           hg_lb_logits, odd_w_in, odd_w_out, lru_conv_w, lru_conv_b, lru_w_a, lru_b_a, lru_w_x,
           lru_b_x, lru_lambda, dn_conv_w, dn_a_log, dn_dt_bias, dn_out_norm):
    depth = ffn1_norm.shape[0]
    bp, tp, _ = x_prompt.shape
    bs, ts, _ = x_sample.shape
    past_len = 2048
    tabs = (_rope_tables(0, tp), _rope_tables(past_len, ts))
    xs = [x_prompt, x_sample]
    nb = (bp, bs)
    outs = {k: ([], []) for k in ("ret", "hg", "lh", "lc", "dn", "dc")}

    ffn_sets = []
    for l in range(depth):
        ffn_sets += [(ffn1_norm[l], ffn1_w_in, ffn1_w_out, l),
                     (ffn2_norm[l], ffn2_w_in, ffn2_w_out, l)]
    ffn_w = [(ffn1_w_in[0].astype(BF16), ffn1_w_out[0].astype(BF16))]

    def run_ffn(apply_final, mixer_casts=()):
        k = len(ffn_w) - 1
        casts = []
        if k + 1 < len(ffn_sets):
            _, nwi, nwo, nl = ffn_sets[k + 1]
            casts = [(nwi, nl, FFN_CAST_ROWS, nwi.shape[-1]),
                     (nwo, nl, FFN_CAST_ROWS, nwo.shape[-1])]
        casts += list(mixer_casts)
        xs[0], xs[1], cast = _ffn(xs[0], xs[1], ffn_sets[k][0], ffn_w[k][0], ffn_w[k][1],
                                  final_norm, apply_final, casts)
        ffn_w.append(tuple(cast[:2]) if k + 1 < len(ffn_sets) else ())
        return cast[len(cast) - len(mixer_casts):]

    for l in range(depth):
        j = l // 2
        last = l == depth - 1
        if l % 2 == 0:
            w_in, w_out = run_ffn(False, [(even_w_in, j, None, EVEN_IN),
                                          (even_w_out, j, None, D_MODEL)])
            for g in range(2):
                if g == 0:
                    s_ret = jnp.zeros((bp, HEADS, HD, HD), F32)
                    s_hg = jnp.zeros((bp, HEADS, HD, HD), F32)
                else:
                    s_ret, s_hg = state_ret[j], state_hgrn[j]
                xs[g], n_ret, n_hg = _even_mixer(
                    xs[g], tabs[g][0], tabs[g][1], mix_norm[l], w_in, w_out, ret_out_norm[j],
                    hg_out_norm[j], hg_lb_logits, s_ret, s_hg, j)
                outs["ret"][g].append(n_ret)
                outs["hg"][g].append(n_hg)
        else:
            w_in, w_out = run_ffn(False, [(odd_w_in, j, None, 6 * GRP),
                                          (odd_w_out, j, None, D_MODEL)])
            n_tail = odd_w_in.shape[-1] - 6 * GRP
            tail = lax.optimization_barrier(odd_w_in[j][:, 6 * GRP:])
            w_tail = jnp.pad(tail, ((0, 0), (0, HD - n_tail))).astype(BF16)
            cw = jnp.concatenate([lru_conv_w[j], dn_conv_w[j]], axis=-1)
            wa = _block_diag(lru_w_a[j]).astype(BF16)
            wx = _block_diag(lru_w_x[j]).astype(BF16)
            dnp = jnp.zeros((2, HD), F32)
            dnp = dnp.at[0, HEADS:2 * HEADS].set(dn_dt_bias[j])
            dnp = dnp.at[1, HEADS:2 * HEADS].set(dn_a_log[j])
            for g in range(2):
                if g == 0:
                    s_lh = jnp.zeros((bp, 1, LRU_WIDTH), F32)
                    s_lc = jnp.zeros((bp, CONV_W - 1, LRU_WIDTH), F32)
                    s_dn = jnp.zeros((bp, HEADS, HD, HD), F32)
                    s_dc = jnp.zeros((bp, CONV_W - 1, 3 * GRP), F32)
                else:
                    s_lh = state_lru_h[j].reshape(bs, 1, LRU_WIDTH)
                    s_lc, s_dn, s_dc = state_lru_conv[j], state_dn[j], state_dn_conv[j]
                xs[g], n_lh, n_lc, n_dn, n_dc = _odd_mixer(
                    xs[g], mix_norm[l], w_in, w_tail, w_out, cw, lru_conv_b[j].reshape(1, -1), wa,
                    lru_b_a[j].reshape(1, -1), wx, lru_b_x[j].reshape(1, -1),
                    lru_lambda[j].reshape(1, -1), dnp, dn_out_norm[j].reshape(1, -1),
                    s_lh, s_lc, s_dn, s_dc)
                outs["lh"][g].append(n_lh.reshape(nb[g], LRU_WIDTH))
                outs["lc"][g].append(n_lc)
                outs["dn"][g].append(n_dn)
                outs["dc"][g].append(n_dc)
        run_ffn(last)

    res = [xs[0], xs[1]]
    for k in ("ret", "hg", "lh", "lc", "dn", "dc"):
        for g in range(2):
            res.append(jnp.stack(outs[k][g]))
    return tuple(res)
```

```python
import functools
import math

import jax
import jax.numpy as jnp
from jax import lax
from jax.experimental import pallas as pl
from jax.experimental.pallas import tpu as pltpu

F32 = jnp.float32
BF16 = jnp.bfloat16

D_MODEL = 1024
FF_DIM = 2816
EPS = 1e-6
CHUNK = 64
CONV_W = 4
HEADS = 4
HD = 128
ROPE_BASE = 10000.0
LRU_WIDTH = 512
LRU_C = 8.0
GRP = HEADS * HD
EVEN_IN = 8 * GRP
SUB = 8
DN_SUB = 4

VMEM_LIMIT = 56 * 1024 * 1024

FFN_TM = 512
FFN_TF = 256
FFN_ROW_SPLIT = 2
FFN_CAST_ROWS = 16
SEQ_TILE = 512
MIXER_LANES = 1
ODD_LANES = 2

RET_LOG_G = [math.log1p(-(2.0 ** (-5.0 - h))) for h in range(HEADS)]


def _rms(x, g):
    return x * lax.rsqrt(jnp.mean(x * x, axis=-1, keepdims=True) + EPS) * g


def _silu(x):
    return x * jax.nn.sigmoid(x)


def _dot(a, b):
    return jnp.dot(a, b, preferred_element_type=F32)


def _dot_nt(a, b):
    return lax.dot_general(a, b, (((1,), (1,)), ((), ())), preferred_element_type=F32)


def _dot_tn(a, b):
    return lax.dot_general(a, b, (((0,), (0,)), ((), ())), preferred_element_type=F32)


def _head_norm(o, g):
    return o * lax.rsqrt(jnp.mean(o * o, axis=-1, keepdims=True) + EPS) * g


def _hcols(h):
    return slice(h * HD, (h + 1) * HD)


def _chunk_cumsum(x):
    pos = lax.broadcasted_iota(jnp.int32, x.shape, 0) & (CHUNK - 1)
    k = 1
    while k < CHUNK:
        x = jnp.where(pos >= k, x + pltpu.roll(x, k, 0), x)
        k *= 2
    return x


def _resident(block_shape, index_map):
    return pl.BlockSpec(block_shape, index_map, pipeline_mode=pl.Buffered(1))


def _ffn_kernel(*refs, n_first, apply_final, n_casts):
    xa_ref, xb_ref, g_ref, win_ref, wout_ref, fin_ref = refs[:6]
    cast_in = refs[6:6 + n_casts]
    oa_ref, ob_ref = refs[6 + n_casts:8 + n_casts]
    cast_out = refs[8 + n_casts:8 + 2 * n_casts]
    act_ref = refs[8 + 2 * n_casts]
    for src, dst in zip(cast_in, cast_out):
        dst[...] = src[...].astype(BF16)
    i = pl.program_id(0)

    def tile(x_ref, o_ref):
        tm = x_ref.shape[0]
        halves = [slice(r, r + tm // FFN_ROW_SPLIT) for r in range(0, tm, tm // FFN_ROW_SPLIT)]
        hs = [_rms(x_ref[rows, :], g_ref[...]).astype(BF16) for rows in halves]
        for c in range(FF_DIM // FFN_TF):
            lo = c * FFN_TF
            for rows, h in zip(halves, hs):
                gate = _dot(h, win_ref[:, lo:lo + FFN_TF])
                up = _dot(h, win_ref[:, FF_DIM + lo:FF_DIM + lo + FFN_TF])
                act_ref[rows, lo:lo + FFN_TF] = (_silu(gate) * up).astype(BF16)
        for rows in halves:
            y = x_ref[rows, :] + 0.5 * _dot(act_ref[rows, :], wout_ref[...])
            if apply_final:
                y = _rms(y, fin_ref[...])
            o_ref[rows, :] = y

    @pl.when(i < n_first)
    def _():
        tile(xa_ref, oa_ref)

    @pl.when(i >= n_first)
    def _():
        tile(xb_ref, ob_ref)


def _ffn(xa, xb, norm_g, w_in, w_out, final_g, apply_final, casts=()):
    sa, sb = xa.shape, xb.shape
    xa2, xb2 = xa.reshape(-1, D_MODEL), xb.reshape(-1, D_MODEL)
    tm = min(FFN_TM, xa2.shape[0], xb2.shape[0])
    na, nb = xa2.shape[0] // tm, xb2.shape[0] // tm
    x_blk = (tm, D_MODEL)
    in_specs = [
        pl.BlockSpec(x_blk, lambda i: (jnp.minimum(i, na - 1), 0)),
        pl.BlockSpec(x_blk, lambda i: (jnp.maximum(i - na, 0), 0)),
        _resident((1, D_MODEL), lambda i: (0, 0)),
        _resident((D_MODEL, 2 * FF_DIM), lambda i: (0, 0)),
        _resident((FF_DIM, D_MODEL), lambda i: (0, 0)),
        _resident((1, D_MODEL), lambda i: (0, 0)),
    ]
    out_specs = [
        pl.BlockSpec(x_blk, lambda i: (jnp.minimum(i, na - 1), 0)),
        pl.BlockSpec(x_blk, lambda i: (jnp.maximum(i - na, 0), 0)),
    ]
    out_shape = [jax.ShapeDtypeStruct(xa2.shape, F32), jax.ShapeDtypeStruct(xb2.shape, F32)]
    args = [xa2, xb2, norm_g.reshape(1, D_MODEL), w_in, w_out, final_g.reshape(1, D_MODEL)]
    cast_in_specs, cast_out_specs = [], []
    for arr, layer, row_blocks, c in casts:
        r = arr.shape[1]
        ncol = na // (row_blocks or na)
        nrow = na // ncol
        rows = r // nrow

        def blk(i, ncol=ncol):
            i = jnp.minimum(i, na - 1)
            return i // ncol, i % ncol

        def blk_in(i, blk=blk, first=layer * nrow):
            rb, cb = blk(i)
            return first + rb, cb

        cast_in_specs.append(pl.BlockSpec((rows, c // ncol), blk_in))
        cast_out_specs.append(pl.BlockSpec((rows, c // ncol), blk))
        out_shape.append(jax.ShapeDtypeStruct((r, c), BF16))
        args.append(arr.reshape(-1, arr.shape[-1]))
    outs = pl.pallas_call(
        functools.partial(_ffn_kernel, n_first=na, apply_final=apply_final,
                          n_casts=len(casts)),
        grid=(na + nb,),
        in_specs=in_specs + cast_in_specs,
        out_specs=out_specs + cast_out_specs,
        out_shape=out_shape,
        scratch_shapes=[pltpu.VMEM((tm, FF_DIM), BF16)],
        compiler_params=pltpu.CompilerParams(
            dimension_semantics=("arbitrary",), vmem_limit_bytes=VMEM_LIMIT),
        name="ffn",
    )(*args)
    return outs[0].reshape(sa), outs[1].reshape(sb), list(outs[2:])


def _seq_tiling(b, t, lanes=1):
    if t >= SEQ_TILE:
        lanes = lanes if b % lanes == 0 else 1
        return lanes, SEQ_TILE // CHUNK // lanes
    return min(b, SEQ_TILE // t), t // CHUNK


def _even_kernel(x_ref, cos_ref, sin_ref, ng_ref, win_ref, wout_ref, rg_ref, hg_ref, lbl_ref,
                 sret_ref, shg_ref, xo_ref, sret_o, shg_o,
                 qs_ref, qd_ref, kb_ref, kd_ref, rv_ref, hv_ref,
                 rgate_ref, hq_ref, hf_ref, hgate_ref, mix_ref, hgt_ref, *, ns, nc, layer_j):
    t = pl.program_id(1)
    tt = ns * nc * CHUNK
    heads = range(HEADS)

    @pl.when(t == 0)
    def _():
        sret_o[...] = sret_ref[...]
        for s in range(ns):
            for h in heads:
                hgt_ref[s, h] = shg_ref[s, h].T

    x = x_ref[...].reshape(tt, D_MODEL)
    hn = _rms(x, ng_ref[...]).astype(BF16)

    def proj(g):
        return _dot(hn, win_ref[:, g * GRP:(g + 1) * GRP])

    cos = jnp.concatenate([cos_ref[...]] * ns, axis=0)
    sin = jnp.concatenate([sin_ref[...]] * ns, axis=0)
    scale = HD ** -0.5
    pos = lax.broadcasted_iota(jnp.int32, (CHUNK, HD), 0).astype(F32)

    def tiled(v):
        return jnp.concatenate([v] * (tt // CHUNK), axis=0)

    qdec = [tiled(jnp.exp(RET_LOG_G[h] * (pos + 1.0)) * scale) for h in heads]
    kdec = [tiled(jnp.exp(RET_LOG_G[h] * (CHUNK - 1.0 - pos))) for h in heads]

    def rope(z, h):
        v = z[:, _hcols(h)]
        return v * cos + pltpu.roll(v, HD // 2, 1) * sin

    z0 = proj(0)
    z1 = proj(1)
    for h in heads:
        q = rope(z0, h)
        qs_ref[:, _hcols(h)] = (q * scale).astype(BF16)
        qd_ref[:, _hcols(h)] = (q * qdec[h]).astype(BF16)
    z2 = proj(2)
    for h in heads:
        k = rope(z1, h)
        kb_ref[:, _hcols(h)] = k.astype(BF16)
        kd_ref[:, _hcols(h)] = (k * kdec[h]).astype(BF16)
    z3 = proj(3)
    rv_ref[...] = z2.astype(BF16)
    z4 = proj(4)
    rgate_ref[...] = _silu(z3)
    z5 = proj(5)
    hq_ref[...] = _silu(z4)
    z6 = proj(6)
    hf_ref[...] = z5
    z7 = proj(7)
    hv_ref[...] = z6.astype(BF16)
    hgate_ref[...] = _silu(z7)

    ri = lax.broadcasted_iota(jnp.int32, (CHUNK, CHUNK), 0)
    ci = lax.broadcasted_iota(jnp.int32, (CHUNK, CHUNK), 1)
    absd = jnp.abs(ri - ci).astype(F32)
    intra = [jnp.exp(RET_LOG_G[h] * absd) for h in heads]
    sdec = [math.exp(RET_LOG_G[h] * CHUNK) for h in heads]

    lbl = lbl_ref[...]
    e = jnp.exp(lbl - jnp.max(lbl, axis=0, keepdims=True))
    sm = e / jnp.sum(e, axis=0, keepdims=True)
    lb_all = jnp.sum(sm[:layer_j + 1], axis=0, keepdims=True)

    nb = CHUNK // SUB
    trow_io = lax.broadcasted_iota(jnp.int32, (nb, SUB, CHUNK), 1)
    lane_io = lax.broadcasted_iota(jnp.int32, (nb, SUB, CHUNK), 2)
    blk_io = lax.broadcasted_iota(jnp.int32, (nb, SUB, CHUNK), 0)

    def chunk_body(i, carry):
        streams = []
        for u in range(lanes):
            s, c = (i * lanes + u, 0) if nc == 1 else (u, i)
            r0 = (s * nc + c) * CHUNK
            streams += [(s, r0, h) for h in heads]

        def ld(ref, r0, h):
            return ref[pl.ds(r0, CHUNK), pl.ds(h * HD, HD)]

        ret_states = [sret_o[s, h] for s, _, h in streams]
        hg_states = [hgt_ref[s, h] for s, _, h in streams]

        r_v, r_att, r_inter = [], [], []
        for n, (s, r0, h) in enumerate(streams):
            st = ret_states[n]
            r_v.append(ld(rv_ref, r0, h))
            r_att.append(_dot_nt(ld(qs_ref, r0, h), ld(kb_ref, r0, h)))
            r_inter.append(_dot(ld(qd_ref, r0, h), st.astype(BF16)))
            sret_o[s, h] = st * sdec[h] + _dot_tn(ld(kd_ref, r0, h), r_v[n])

        h_q, h_k, h_b, h_v, h_off, h_inter = [], [], [], [], [], []
        for n, (s, r0, h) in enumerate(streams):
            lb = lb_all[:, _hcols(h)]
            sig = jax.nn.sigmoid(ld(hf_ref, r0, h))
            v = ld(hv_ref, r0, h)
            q = ld(hq_ref, r0, h)
            k = (1.0 - lb) * (1.0 - sig)
            b = _chunk_cumsum(jnp.log(lb + (1.0 - lb) * sig))
            stt = hg_states[n]
            off = []
            for blk in range(1, nb):
                lo = blk * SUB
                rb = b[lo - 1:lo, :]
                qt = (q[lo:lo + SUB] * jnp.exp(b[lo:lo + SUB] - rb)).astype(BF16)
                kt = jnp.concatenate([k[:lo] * jnp.exp(rb - b[:lo]),
                                      jnp.zeros((CHUNK - lo, HD), F32)], axis=0).astype(BF16)
                off.append(_dot_nt(qt, kt))
            h_off.append(off)
            h_inter.append(_dot_nt((q * jnp.exp(b)).astype(BF16), stt.astype(BF16)))
            bend = b[CHUNK - 1:CHUNK, :]
            kd = (k * jnp.exp(bend - b)).astype(BF16)
            hgt_ref[s, h] = stt * jnp.exp(bend) + _dot_tn(v, kd)
            h_q.append(q)
            h_k.append(k)
            h_b.append(b)
            h_v.append(v)

        for n, (s, r0, h) in enumerate(streams):
            o = _dot((r_att[n] * intra[h]).astype(BF16), r_v[n]) + r_inter[n]
            o = _head_norm(o, rg_ref[:, pl.ds(h * HD, HD)]) * ld(rgate_ref, r0, h)
            mix_ref[pl.ds(r0, CHUNK), pl.ds(h * HD, HD)] = o.astype(BF16)

        h_att = []
        for n in range(len(streams)):
            b4 = h_b[n].reshape(nb, SUB, HD)
            q4 = h_q[n].reshape(nb, SUB, HD)
            k4 = h_k[n].reshape(nb, SUB, HD)
            diag = jnp.zeros((nb, SUB, CHUNK), F32)
            for sp in range(SUB):
                dec = jnp.exp(b4 - b4[:, sp:sp + 1, :])
                r = jnp.sum(dec * q4 * k4[:, sp:sp + 1, :], axis=-1, keepdims=True)
                diag = jnp.where(lane_io == blk_io * SUB + sp, r, diag)
            diag = jnp.where(lane_io - blk_io * SUB <= trow_io, diag, 0.0).reshape(CHUNK, CHUNK)
            parts = [diag[0:SUB]]
            for blk in range(1, nb):
                parts.append(diag[blk * SUB:(blk + 1) * SUB] + h_off[n][blk - 1])
            h_att.append(jnp.concatenate(parts, axis=0).astype(BF16))
        for n, (s, r0, h) in enumerate(streams):
            o = _dot(h_att[n], h_v[n]) + h_inter[n]
            o = _head_norm(o, hg_ref[:, pl.ds(h * HD, HD)]) * ld(hgate_ref, r0, h)
            mix_ref[pl.ds(r0, CHUNK), pl.ds(GRP + h * HD, HD)] = o.astype(BF16)
        return carry

    lanes = MIXER_LANES if ns % MIXER_LANES == 0 else 1
    for i in range(ns * nc // lanes):
        chunk_body(i, 0)

    xo_ref[...] = (x + _dot(mix_ref[...], wout_ref[...])).reshape(xo_ref.shape)

    @pl.when(t == pl.num_programs(1) - 1)
    def _():
        for s in range(ns):
            for h in heads:
                shg_o[s, h] = hgt_ref[s, h].T


def _even_mixer(x, cos, sin, norm_g, w_in, w_out, ret_g, hg_g, lb_logits, s_ret, s_hg, layer_j):
    b, t, _ = x.shape
    ns, nc = _seq_tiling(b, t, MIXER_LANES)
    tt = ns * nc * CHUNK
    grid = (b // ns, t // (nc * CHUNK))
    st_spec = pl.BlockSpec((ns, HEADS, HD, HD), lambda i, j: (i, 0, 0, 0))
    x_spec = pl.BlockSpec((ns, nc * CHUNK, D_MODEL), lambda i, j: (i, j, 0))

    def full(a):
        return _resident(a.shape, lambda i, j: (0,) * a.ndim)

    tab_spec = pl.BlockSpec((nc * CHUNK, HD), lambda i, j: (j, 0))
    args = (x, cos, sin, norm_g.reshape(1, D_MODEL), w_in, w_out, ret_g.reshape(1, GRP),
            hg_g.reshape(1, GRP), lb_logits, s_ret, s_hg)
    in_specs = [x_spec, tab_spec, tab_spec] + [full(a) for a in args[3:9]] + [st_spec, st_spec]
    grp_bf16 = pltpu.VMEM((tt, GRP), BF16)
    grp_f32 = pltpu.VMEM((tt, GRP), F32)
    return pl.pallas_call(
        functools.partial(_even_kernel, ns=ns, nc=nc, layer_j=layer_j),
        grid=grid,
        in_specs=in_specs,
        out_specs=[x_spec, st_spec, st_spec],
        out_shape=[jax.ShapeDtypeStruct(x.shape, F32),
                   jax.ShapeDtypeStruct(s_ret.shape, F32),
                   jax.ShapeDtypeStruct(s_hg.shape, F32)],
        scratch_shapes=[grp_bf16] * 6 + [grp_f32] * 4
        + [pltpu.VMEM((tt, D_MODEL), BF16), pltpu.VMEM((ns, HEADS, HD, HD), F32)],
        compiler_params=pltpu.CompilerParams(
            dimension_semantics=("parallel", "arbitrary"), vmem_limit_bytes=VMEM_LIMIT),
        name="even_mixer",
    )(*args)


def _softplus(x):
    return jnp.maximum(x, 0.0) + jnp.log1p(jnp.exp(-jnp.abs(x)))


def _gelu_tanh(x):
    return 0.5 * x * (1.0 + jnp.tanh(math.sqrt(2.0 / math.pi) * (x + 0.044715 * (x * x * x))))


def _odd_kernel(x_ref, ng_ref, win_ref, wt_ref, wout_ref, cw_ref, lcb_ref, wa_ref, ba_ref,
                wx_ref, bx_ref, lam_ref, dnp_ref, dg_ref, lh_ref, lc_ref, ds_ref, dc_ref,
                xo_ref, lh_o, lc_o, ds_o, dc_o,
                z_ref, gl_ref, sg_ref, bt_ref, gt_ref, gtt_ref, mix_ref, xp_ref, *, ns, nc):
    t = pl.program_id(1)
    tt = ns * nc * CHUNK
    conv_ch = LRU_WIDTH + 3 * GRP
    hist = CONV_W - 1
    top = 8
    heads = range(HEADS)

    @pl.when(t == 0)
    def _():
        lh_o[...] = lh_ref[...]
        lc_o[...] = lc_ref[...]
        ds_o[...] = ds_ref[...]
        dc_o[...] = dc_ref[...]

    x = x_ref[...].reshape(tt, D_MODEL)
    hn = _rms(x, ng_ref[...]).astype(BF16)

    def proj(lo):
        return _dot(hn, win_ref[:, lo:lo + GRP])

    zs = _dot(hn, wt_ref[...])
    z5 = proj(5 * GRP)
    bt_ref[...] = jax.nn.sigmoid(zs)
    g_all = _chunk_cumsum(-jnp.exp(dnp_ref[1:2, :]) * _softplus(zs + dnp_ref[0:1, :]))
    gt_ref[...] = g_all
    for c in range(ns * nc):
        gtt_ref[c] = g_all[c * CHUNK:(c + 1) * CHUNK, :].T
    z1 = proj(GRP)
    sg_ref[...] = _silu(z5)
    z_ref[:, 0:GRP] = proj(0)
    gl_ref[...] = _gelu_tanh(z1)
    for g in range(3):
        z_ref[:, (g + 1) * GRP:(g + 2) * GRP] = proj((g + 2) * GRP)

    sp_lam = _softplus(-lam_ref[...])
    ri = lax.broadcasted_iota(jnp.int32, (CHUNK, CHUNK), 0)
    ci = lax.broadcasted_iota(jnp.int32, (CHUNK, CHUNK), 1)
    rows_w = lax.broadcasted_iota(jnp.int32, (CHUNK, LRU_WIDTH), 0)
    eye_c = (ri == ci).astype(F32)
    sub_bits = DN_SUB.bit_length() - 1
    same_blk = (ri >> sub_bits) == (ci >> sub_bits)
    merge_masks = []
    for lvl in range(sub_bits, CHUNK.bit_length() - 1):
        merge_masks.append(((ri >> (lvl + 1)) == (ci >> (lvl + 1)))
                           & (((ri >> lvl) & 1) == 1) & (((ci >> lvl) & 1) == 0))

    def chunk_body(i, carry):
        lane_seq, lane_rows, lane_chunk, ys = [], [], [], []
        for u in range(lanes):
            s, c = (i * lanes + u, 0) if nc == 1 else (u, i)
            r0 = (s * nc + c) * CHUNK
            rows = pl.ds(r0, CHUNK)
            lane_seq.append(s)
            lane_rows.append(rows)
            lane_chunk.append(s * nc + c)

            xp_ref[u, top - hist:top, 0:LRU_WIDTH] = lc_o[s]
            xp_ref[u, top - hist:top, LRU_WIDTH:conv_ch] = dc_o[s]
            xp_ref[u, top:top + CHUNK, :] = z_ref[rows, :]
            lc_o[s] = xp_ref[u, top + CHUNK - hist:top + CHUNK, 0:LRU_WIDTH]
            dc_o[s] = xp_ref[u, top + CHUNK - hist:top + CHUNK, LRU_WIDTH:conv_ch]
            y = xp_ref[u, top - hist:top - hist + CHUNK, :] * cw_ref[0:1, :]
            for jj in range(1, CONV_W):
                y = y + (xp_ref[u, top - hist + jj:top - hist + jj + CHUNK, :]
                         * cw_ref[jj:jj + 1, :])
            ys.append(y)

        def rg_lru_branch(lane):
            y, s, rows = ys[lane], lane_seq[lane], lane_rows[lane]
            lx = y[:, 0:LRU_WIDTH] + lcb_ref[...]
            xb = lx.astype(BF16)
            r = jax.nn.sigmoid(_dot(xb, wa_ref[...]) + ba_ref[...])
            ig = jax.nn.sigmoid(_dot(xb, wx_ref[...]) + bx_ref[...])
            a = jnp.exp(-LRU_C * r * sp_lam)
            u = jnp.sqrt(1.0 - a * a) * (ig * lx)
            k = 1
            while k < CHUNK:
                m = rows_w >= k
                u = jnp.where(m, a * pltpu.roll(u, k, 0) + u, u)
                a = jnp.where(m, a * pltpu.roll(a, k, 0), a)
                k *= 2
            hseq = a * lh_o[s] + u
            lh_o[s] = hseq[CHUNK - 1:CHUNK]
            mix_ref[rows, 0:LRU_WIDTH] = (gl_ref[rows, :] * hseq).astype(BF16)

        streams = [(u, h) for u in range(lanes) for h in heads]
        ns_ = range(len(streams))
        beta_t = [bt_ref[rows, :] for rows in lane_rows]
        g_t = [gt_ref[rows, :] for rows in lane_rows]
        g_tt = [gtt_ref[c] for c in lane_chunk]
        states = [ds_o[lane_seq[u], h] for u, h in streams]
        kn, both_k, both_s = [], [], []
        for n, (u, h) in enumerate(streams):
            y = ys[u]
            base = LRU_WIDTH + h * HD
            q = _silu(y[:, base:base + HD])
            kk_ = _silu(y[:, base + GRP:base + GRP + HD])
            q = q * lax.rsqrt(jnp.sum(q * q, axis=-1, keepdims=True) + EPS) * (HD ** -0.5)
            kn.append(kk_ * lax.rsqrt(jnp.sum(kk_ * kk_, axis=-1, keepdims=True) + EPS))
            knb = kn[n].astype(BF16)
            qk_in = jnp.concatenate([q.astype(BF16), knb], axis=0)
            both_k.append(_dot_nt(qk_in, knb))
            both_s.append(_dot(qk_in, states[n].astype(BF16)))
        gcol, eg, e_incl, amat, rhs = [], [], [], [], []
        for n, (u, h) in enumerate(streams):
            y = ys[u]
            v = _silu(y[:, LRU_WIDTH + 2 * GRP + h * HD:LRU_WIDTH + 2 * GRP + (h + 1) * HD])
            gcol.append(g_t[u][:, HEADS + h:HEADS + h + 1])
            rel = gcol[n] - g_tt[u][HEADS + h:HEADS + h + 1, :]
            e_strict = jnp.exp(jnp.where(ri > ci, rel, -jnp.inf))
            e_incl.append(jnp.where(ri == ci, 1.0, e_strict))
            beta = beta_t[u][:, h:h + 1]
            eg.append(jnp.exp(gcol[n]))
            amat.append(beta * both_k[n][CHUNK:] * e_strict)
            rhs.append(beta * (v - eg[n] * both_s[n][CHUNK:]))
        dblk = [jnp.where(same_blk, amat[n], 0.0) for n in ns_]
        tinv = [eye_c for _ in ns_]
        for jj in range(DN_SUB - 1):
            for n in ns_:
                colv = jnp.concatenate(
                    [dblk[n][b0:b0 + DN_SUB, b0 + jj:b0 + jj + 1]
                     for b0 in range(0, CHUNK, DN_SUB)], axis=0)
                rowm = jnp.concatenate(
                    [jnp.broadcast_to(tinv[n][b0 + jj:b0 + jj + 1, :], (DN_SUB, CHUNK))
                     for b0 in range(0, CHUNK, DN_SUB)], axis=0)
                tinv[n] = tinv[n] - colv * rowm
        for lane in range(lanes):
            rg_lru_branch(lane)
        tb = [tinv[n].astype(BF16) for n in ns_]
        pend = [[_dot(tb[n], jnp.where(lm, amat[n], 0.0).astype(BF16)) for lm in merge_masks]
                for n in ns_]
        for _ in merge_masks:
            xb_ = [pend[n][0].astype(BF16) for n in ns_]
            tb = [tinv[n].astype(BF16) for n in ns_]
            tinv = [tinv[n] - _dot(xb_[n], tb[n]) for n in ns_]
            pend = [[p - _dot(xb_[n], p.astype(BF16)) for p in pend[n][1:]] for n in ns_]
        w = [_dot(tinv[n].astype(BF16), rhs[n].astype(BF16)).astype(BF16) for n in ns_]
        for n, (u, h) in enumerate(streams):
            s, rows = lane_seq[u], lane_rows[u]
            o = eg[n] * both_s[n][:CHUNK] + _dot((both_k[n][:CHUNK] * e_incl[n]).astype(BF16), w[n])
            gend = gcol[n][CHUNK - 1:CHUNK, :]
            kd = (kn[n] * jnp.exp(gend - gcol[n])).astype(BF16)
            ds_o[s, h] = states[n] * jnp.exp(gend) + _dot_tn(kd, w[n])
            o = _head_norm(o, dg_ref[:, pl.ds(h * HD, HD)]) * sg_ref[rows, pl.ds(h * HD, HD)]
            mix_ref[rows, pl.ds(LRU_WIDTH + h * HD, HD)] = o.astype(BF16)
        return carry

    lanes = ODD_LANES if ns % ODD_LANES == 0 else 1
    for i in range(ns * nc // lanes):
        chunk_body(i, 0)

    xo_ref[...] = (x + _dot(mix_ref[...], wout_ref[...])).reshape(xo_ref.shape)


def _odd_mixer(x, norm_g, w_in, w_tail, w_out, cw, lcb, wa, ba, wx, bx, lam, dnp, dn_g,
               s_lh, s_lc, s_dn, s_dc):
    b, t, _ = x.shape
    ns, nc = _seq_tiling(b, t, ODD_LANES)
    tt = ns * nc * CHUNK
    grid = (b // ns, t // (nc * CHUNK))
    x_spec = pl.BlockSpec((ns, nc * CHUNK, D_MODEL), lambda i, j: (i, j, 0))

    def full(a):
        return _resident(a.shape, lambda i, j: (0,) * a.ndim)

    def st_spec(a):
        return pl.BlockSpec((ns,) + a.shape[1:], lambda i, j: (i,) + (0,) * (a.ndim - 1))

    params = (norm_g.reshape(1, D_MODEL), w_in, w_tail, w_out, cw, lcb, wa, ba, wx, bx, lam, dnp,
              dn_g)
    states = (s_lh, s_lc, s_dn, s_dc)
    grp_f32 = pltpu.VMEM((tt, GRP), F32)
    head_f32 = pltpu.VMEM((tt, HD), F32)
    return pl.pallas_call(
        functools.partial(_odd_kernel, ns=ns, nc=nc),
        grid=grid,
        in_specs=[x_spec] + [full(a) for a in params] + [st_spec(a) for a in states],
        out_specs=[x_spec] + [st_spec(a) for a in states],
        out_shape=[jax.ShapeDtypeStruct(x.shape, F32)]
        + [jax.ShapeDtypeStruct(a.shape, F32) for a in states],
        scratch_shapes=[pltpu.VMEM((tt, LRU_WIDTH + 3 * GRP), F32), grp_f32, grp_f32,
                        head_f32, head_f32, pltpu.VMEM((ns * nc, HD, CHUNK), F32),
                        pltpu.VMEM((tt, D_MODEL), BF16),
                        pltpu.VMEM((ODD_LANES, 8 + CHUNK, LRU_WIDTH + 3 * GRP), F32)],
        compiler_params=pltpu.CompilerParams(
            dimension_semantics=("parallel", "arbitrary"), vmem_limit_bytes=VMEM_LIMIT),
        name="odd_mixer",
    )(x, *params, *states)


def _rope_tables(pos0, n):
    half = HD // 2
    freq = ROPE_BASE ** (-jnp.arange(half, dtype=F32) / half)
    ang_a = (pos0 + CHUNK * jnp.arange(n // CHUNK)).astype(F32)[:, None, None] * freq
    ang_b = jnp.arange(CHUNK, dtype=F32)[None, :, None] * freq
    cos_a, sin_a, cos_b, sin_b = jnp.cos(ang_a), jnp.sin(ang_a), jnp.cos(ang_b), jnp.sin(ang_b)
    cos = (cos_a * cos_b - sin_a * sin_b).reshape(n, half)
    sin = (sin_a * cos_b + cos_a * sin_b).reshape(n, half)
    return jnp.concatenate([cos, cos], axis=-1), jnp.concatenate([-sin, sin], axis=-1)


def _block_diag(w):
    n, bs, _ = w.shape
    eye = jnp.eye(n, dtype=w.dtype)
    return (eye[:, None, :, None] * w[:, :, None, :]).reshape(n * bs, n * bs)


def kernel(x_prompt, x_sample, state_ret, state_hgrn, state_lru_h, state_lru_conv, state_dn,
           state_dn_conv, ffn1_norm, ffn1_w_in, ffn1_w_out, mix_norm, ffn2_norm, ffn2_w_in,
           ffn2_w_out, final_norm, even_w_in, even_w_out, ret_out_norm, hg_out_norm,
           hg_lb_logits, odd_w_in, odd_w_out, lru_conv_w, lru_conv_b, lru_w_a, lru_b_a, lru_w_x,
           lru_b_x, lru_lambda, dn_conv_w, dn_a_log, dn_dt_bias, dn_out_norm):
    depth = ffn1_norm.shape[0]
    bp, tp, _ = x_prompt.shape
    bs, ts, _ = x_sample.shape
    past_len = 2048
    tabs = (_rope_tables(0, tp), _rope_tables(past_len, ts))
    xs = [x_prompt, x_sample]
    nb = (bp, bs)
    outs = {k: ([], []) for k in ("ret", "hg", "lh", "lc", "dn", "dc")}

    ffn_sets = []
    for l in range(depth):
        ffn_sets += [(ffn1_norm[l], ffn1_w_in, ffn1_w_out, l),
                     (ffn2_norm[l], ffn2_w_in, ffn2_w_out, l)]
    ffn_w = [(ffn1_w_in[0].astype(BF16), ffn1_w_out[0].astype(BF16))]

    def run_ffn(apply_final, mixer_casts=()):
        k = len(ffn_w) - 1
        casts = []
        if k + 1 < len(ffn_sets):
            _, nwi, nwo, nl = ffn_sets[k + 1]
            casts = [(nwi, nl, FFN_CAST_ROWS, nwi.shape[-1]),
                     (nwo, nl, FFN_CAST_ROWS, nwo.shape[-1])]
        casts += list(mixer_casts)
        xs[0], xs[1], cast = _ffn(xs[0], xs[1], ffn_sets[k][0], ffn_w[k][0], ffn_w[k][1],
                                  final_norm, apply_final, casts)
        ffn_w.append(tuple(cast[:2]) if k + 1 < len(ffn_sets) else ())
        return cast[len(cast) - len(mixer_casts):]

    for l in range(depth):
        j = l // 2
        last = l == depth - 1
        if l % 2 == 0:
            w_in, w_out = run_ffn(False, [(even_w_in, j, None, EVEN_IN),
                                          (even_w_out, j, None, D_MODEL)])
            for g in range(2):
                if g == 0:
                    s_ret = jnp.zeros((bp, HEADS, HD, HD), F32)
                    s_hg = jnp.zeros((bp, HEADS, HD, HD), F32)
                else:
                    s_ret, s_hg = state_ret[j], state_hgrn[j]
                xs[g], n_ret, n_hg = _even_mixer(
                    xs[g], tabs[g][0], tabs[g][1], mix_norm[l], w_in, w_out, ret_out_norm[j],
                    hg_out_norm[j], hg_lb_logits, s_ret, s_hg, j)
                outs["ret"][g].append(n_ret)
                outs["hg"][g].append(n_hg)
        else:
            w_in, w_out = run_ffn(False, [(odd_w_in, j, None, 6 * GRP),
                                          (odd_w_out, j, None, D_MODEL)])
            n_tail = odd_w_in.shape[-1] - 6 * GRP
            tail = lax.optimization_barrier(odd_w_in[j][:, 6 * GRP:])
            w_tail = jnp.pad(tail, ((0, 0), (0, HD - n_tail))).astype(BF16)
            cw = jnp.concatenate([lru_conv_w[j], dn_conv_w[j]], axis=-1)
            wa = _block_diag(lru_w_a[j]).astype(BF16)
            wx = _block_diag(lru_w_x[j]).astype(BF16)
            dnp = jnp.zeros((2, HD), F32)
            dnp = dnp.at[0, HEADS:2 * HEADS].set(dn_dt_bias[j])
            dnp = dnp.at[1, HEADS:2 * HEADS].set(dn_a_log[j])
            for g in range(2):
                if g == 0:
                    s_lh = jnp.zeros((bp, 1, LRU_WIDTH), F32)
                    s_lc = jnp.zeros((bp, CONV_W - 1, LRU_WIDTH), F32)
                    s_dn = jnp.zeros((bp, HEADS, HD, HD), F32)
                    s_dc = jnp.zeros((bp, CONV_W - 1, 3 * GRP), F32)
                else:
                    s_lh = state_lru_h[j].reshape(bs, 1, LRU_WIDTH)
                    s_lc, s_dn, s_dc = state_lru_conv[j], state_dn[j], state_dn_conv[j]
                xs[g], n_lh, n_lc, n_dn, n_dc = _odd_mixer(
                    xs[g], mix_norm[l], w_in, w_tail, w_out, cw, lru_conv_b[j].reshape(1, -1), wa,
                    lru_b_a[j].reshape(1, -1), wx, lru_b_x[j].reshape(1, -1),
                    lru_lambda[j].reshape(1, -1), dnp, dn_out_norm[j].reshape(1, -1),
                    s_lh, s_lc, s_dn, s_dc)
                outs["lh"][g].append(n_lh.reshape(nb[g], LRU_WIDTH))
                outs["lc"][g].append(n_lc)
                outs["dn"][g].append(n_dn)
                outs["dc"][g].append(n_dc)
        run_ffn(last)

    res = [xs[0], xs[1]]
    for k in ("ret", "hg", "lh", "lc", "dn", "dc"):
        for g in range(2):
            res.append(jnp.stack(outs[k][g]))
    return tuple(res)
```

```python
import functools
import math

import jax
import jax.numpy as jnp
from jax import lax
from jax.experimental import pallas as pl
from jax.experimental.pallas import tpu as pltpu

F32 = jnp.float32
BF16 = jnp.bfloat16

D_MODEL = 1024
FF_DIM = 2816
EPS = 1e-6
CHUNK = 64
CONV_W = 4
HEADS = 4
HD = 128
ROPE_BASE = 10000.0
PAST_LEN = 2048
LRU_WIDTH = 512
LRU_C = 8.0
GRP = HEADS * HD
EVEN_IN = 8 * GRP
SUB = 8
DN_SUB = 4

VMEM_LIMIT = 56 * 1024 * 1024

FFN_TM = 512
FFN_TF = 256
FFN_ROW_SPLIT = 2
FFN_CAST_ROWS = 16
SEQ_TILE = 512
MIXER_LANES = 1
ODD_LANES = 2

RET_LOG_G = [math.log1p(-(2.0 ** (-5.0 - h))) for h in range(HEADS)]


def _rms(x, g):
    return x * lax.rsqrt(jnp.mean(x * x, axis=-1, keepdims=True) + EPS) * g


def _silu(x):
    return x * jax.nn.sigmoid(x)


def _dot(a, b):
    return jnp.dot(a, b, preferred_element_type=F32)


def _dot_nt(a, b):
    return lax.dot_general(a, b, (((1,), (1,)), ((), ())), preferred_element_type=F32)


def _dot_tn(a, b):
    return lax.dot_general(a, b, (((0,), (0,)), ((), ())), preferred_element_type=F32)


def _head_norm(o, g):
    return o * lax.rsqrt(jnp.mean(o * o, axis=-1, keepdims=True) + EPS) * g


def _hcols(h):
    return slice(h * HD, (h + 1) * HD)


def _chunk_cumsum(x):
    pos = lax.broadcasted_iota(jnp.int32, x.shape, 0) & (CHUNK - 1)
    k = 1
    while k < CHUNK:
        x = jnp.where(pos >= k, x + pltpu.roll(x, k, 0), x)
        k *= 2
    return x


def _resident(block_shape, index_map):
    return pl.BlockSpec(block_shape, index_map, pipeline_mode=pl.Buffered(1))


def _ffn_kernel(*refs, n_first, apply_final, n_casts):
    xa_ref, xb_ref, g_ref, win_ref, wout_ref, fin_ref = refs[:6]
    cast_in = refs[6:6 + n_casts]
    oa_ref, ob_ref = refs[6 + n_casts:8 + n_casts]
    cast_out = refs[8 + n_casts:8 + 2 * n_casts]
    act_ref = refs[8 + 2 * n_casts]
    for src, dst in zip(cast_in, cast_out):
        dst[...] = src[...].astype(BF16)
    i = pl.program_id(0)

    def tile(x_ref, o_ref):
        tm = x_ref.shape[0]
        halves = [slice(r, r + tm // FFN_ROW_SPLIT) for r in range(0, tm, tm // FFN_ROW_SPLIT)]
        hs = [_rms(x_ref[rows, :], g_ref[...]).astype(BF16) for rows in halves]
        for c in range(FF_DIM // FFN_TF):
            lo = c * FFN_TF
            for rows, h in zip(halves, hs):
                gate = _dot(h, win_ref[:, lo:lo + FFN_TF])
                up = _dot(h, win_ref[:, FF_DIM + lo:FF_DIM + lo + FFN_TF])
                act_ref[rows, lo:lo + FFN_TF] = (_silu(gate) * up).astype(BF16)
        for rows in halves:
            y = x_ref[rows, :] + 0.5 * _dot(act_ref[rows, :], wout_ref[...])
            if apply_final:
                y = _rms(y, fin_ref[...])
            o_ref[rows, :] = y

    @pl.when(i < n_first)
    def _():
        tile(xa_ref, oa_ref)

    @pl.when(i >= n_first)
    def _():
        tile(xb_ref, ob_ref)


def _ffn(xa, xb, norm_g, w_in, w_out, final_g, apply_final, casts=()):
    sa, sb = xa.shape, xb.shape
    xa2, xb2 = xa.reshape(-1, D_MODEL), xb.reshape(-1, D_MODEL)
    tm = min(FFN_TM, xa2.shape[0], xb2.shape[0])
    na, nb = xa2.shape[0] // tm, xb2.shape[0] // tm
    x_blk = (tm, D_MODEL)
    in_specs = [
        pl.BlockSpec(x_blk, lambda i: (jnp.minimum(i, na - 1), 0)),
        pl.BlockSpec(x_blk, lambda i: (jnp.maximum(i - na, 0), 0)),
        _resident((1, D_MODEL), lambda i: (0, 0)),
        _resident((D_MODEL, 2 * FF_DIM), lambda i: (0, 0)),
        _resident((FF_DIM, D_MODEL), lambda i: (0, 0)),
        _resident((1, D_MODEL), lambda i: (0, 0)),
    ]
    out_specs = [
        pl.BlockSpec(x_blk, lambda i: (jnp.minimum(i, na - 1), 0)),
        pl.BlockSpec(x_blk, lambda i: (jnp.maximum(i - na, 0), 0)),
    ]
    out_shape = [jax.ShapeDtypeStruct(xa2.shape, F32), jax.ShapeDtypeStruct(xb2.shape, F32)]
    args = [xa2, xb2, norm_g.reshape(1, D_MODEL), w_in, w_out, final_g.reshape(1, D_MODEL)]
    cast_in_specs, cast_out_specs = [], []
    for arr, layer, row_blocks, c in casts:
        r = arr.shape[1]
        ncol = na // (row_blocks or na)
        nrow = na // ncol
        rows = r // nrow

        def blk(i, ncol=ncol):
            i = jnp.minimum(i, na - 1)
            return i // ncol, i % ncol

        def blk_in(i, blk=blk, first=layer * nrow):
            rb, cb = blk(i)
            return first + rb, cb

        cast_in_specs.append(pl.BlockSpec((rows, c // ncol), blk_in))
        cast_out_specs.append(pl.BlockSpec((rows, c // ncol), blk))
        out_shape.append(jax.ShapeDtypeStruct((r, c), BF16))
        args.append(arr.reshape(-1, arr.shape[-1]))
    outs = pl.pallas_call(
        functools.partial(_ffn_kernel, n_first=na, apply_final=apply_final,
                          n_casts=len(casts)),
        grid=(na + nb,),
        in_specs=in_specs + cast_in_specs,
        out_specs=out_specs + cast_out_specs,
        out_shape=out_shape,
        scratch_shapes=[pltpu.VMEM((tm, FF_DIM), BF16)],
        compiler_params=pltpu.CompilerParams(
            dimension_semantics=("arbitrary",), vmem_limit_bytes=VMEM_LIMIT),
        name="ffn",
    )(*args)
    return outs[0].reshape(sa), outs[1].reshape(sb), list(outs[2:])


def _seq_tiling(b, t, lanes=1):
    if t >= SEQ_TILE:
        lanes = lanes if b % lanes == 0 else 1
        return lanes, SEQ_TILE // CHUNK // lanes
    return min(b, SEQ_TILE // t), t // CHUNK


def _even_kernel(x_ref, cos_ref, sin_ref, ng_ref, win_ref, wout_ref, rg_ref, hg_ref, lbl_ref,
                 sret_ref, shg_ref, xo_ref, sret_o, shg_o,
                 qs_ref, qd_ref, kb_ref, kd_ref, rv_ref, hv_ref,
                 rgate_ref, hq_ref, hf_ref, hgate_ref, mix_ref, hgt_ref, *, ns, nc, layer_j):
    t = pl.program_id(1)
    tt = ns * nc * CHUNK
    heads = range(HEADS)

    @pl.when(t == 0)
    def _():
        sret_o[...] = sret_ref[...]
        for s in range(ns):
            for h in heads:
                hgt_ref[s, h] = shg_ref[s, h].T

    x = x_ref[...].reshape(tt, D_MODEL)
    hn = _rms(x, ng_ref[...]).astype(BF16)

    def proj(g):
        return _dot(hn, win_ref[:, g * GRP:(g + 1) * GRP])

    cos = jnp.concatenate([cos_ref[...]] * ns, axis=0)
    sin = jnp.concatenate([sin_ref[...]] * ns, axis=0)
    scale = HD ** -0.5
    pos = lax.broadcasted_iota(jnp.int32, (CHUNK, HD), 0).astype(F32)

    def tiled(v):
        return jnp.concatenate([v] * (tt // CHUNK), axis=0)

    qdec = [tiled(jnp.exp(RET_LOG_G[h] * (pos + 1.0)) * scale) for h in heads]
    kdec = [tiled(jnp.exp(RET_LOG_G[h] * (CHUNK - 1.0 - pos))) for h in heads]

    def rope(z, h):
        v = z[:, _hcols(h)]
        return v * cos + pltpu.roll(v, HD // 2, 1) * sin

    z0 = proj(0)
    z1 = proj(1)
    for h in heads:
        q = rope(z0, h)
        qs_ref[:, _hcols(h)] = (q * scale).astype(BF16)
        qd_ref[:, _hcols(h)] = (q * qdec[h]).astype(BF16)
    z2 = proj(2)
    for h in heads:
        k = rope(z1, h)
        kb_ref[:, _hcols(h)] = k.astype(BF16)
        kd_ref[:, _hcols(h)] = (k * kdec[h]).astype(BF16)
    z3 = proj(3)
    rv_ref[...] = z2.astype(BF16)
    z4 = proj(4)
    rgate_ref[...] = _silu(z3)
    z5 = proj(5)
    hq_ref[...] = _silu(z4)
    z6 = proj(6)
    hf_ref[...] = z5
    z7 = proj(7)
    hv_ref[...] = z6.astype(BF16)
    hgate_ref[...] = _silu(z7)

    ri = lax.broadcasted_iota(jnp.int32, (CHUNK, CHUNK), 0)
    ci = lax.broadcasted_iota(jnp.int32, (CHUNK, CHUNK), 1)
    absd = jnp.abs(ri - ci).astype(F32)
    intra = [jnp.exp(RET_LOG_G[h] * absd) for h in heads]
    sdec = [math.exp(RET_LOG_G[h] * CHUNK) for h in heads]

    lbl = lbl_ref[...]
    e = jnp.exp(lbl - jnp.max(lbl, axis=0, keepdims=True))
    sm = e / jnp.sum(e, axis=0, keepdims=True)
    lb_all = jnp.sum(sm[:layer_j + 1], axis=0, keepdims=True)

    nb = CHUNK // SUB
    trow_io = lax.broadcasted_iota(jnp.int32, (nb, SUB, CHUNK), 1)
    lane_io = lax.broadcasted_iota(jnp.int32, (nb, SUB, CHUNK), 2)
    blk_io = lax.broadcasted_iota(jnp.int32, (nb, SUB, CHUNK), 0)

    def chunk_body(i, carry):
        streams = []
        for u in range(lanes):
            s, c = (i * lanes + u, 0) if nc == 1 else (u, i)
            r0 = (s * nc + c) * CHUNK
            streams += [(s, r0, h) for h in heads]

        def ld(ref, r0, h):
            return ref[pl.ds(r0, CHUNK), pl.ds(h * HD, HD)]

        ret_states = [sret_o[s, h] for s, _, h in streams]
        hg_states = [hgt_ref[s, h] for s, _, h in streams]

        r_v, r_att, r_inter = [], [], []
        for n, (s, r0, h) in enumerate(streams):
            st = ret_states[n]
            r_v.append(ld(rv_ref, r0, h))
            r_att.append(_dot_nt(ld(qs_ref, r0, h), ld(kb_ref, r0, h)))
            r_inter.append(_dot(ld(qd_ref, r0, h), st.astype(BF16)))
            sret_o[s, h] = st * sdec[h] + _dot_tn(ld(kd_ref, r0, h), r_v[n])

        h_q, h_k, h_b, h_v, h_off, h_inter = [], [], [], [], [], []
        for n, (s, r0, h) in enumerate(streams):
            lb = lb_all[:, _hcols(h)]
            sig = jax.nn.sigmoid(ld(hf_ref, r0, h))
            v = ld(hv_ref, r0, h)
            q = ld(hq_ref, r0, h)
            k = (1.0 - lb) * (1.0 - sig)
            b = _chunk_cumsum(jnp.log(lb + (1.0 - lb) * sig))
            stt = hg_states[n]
            off = []
            for blk in range(1, nb):
                lo = blk * SUB
                rb = b[lo - 1:lo, :]
                qt = (q[lo:lo + SUB] * jnp.exp(b[lo:lo + SUB] - rb)).astype(BF16)
                kt = jnp.concatenate([k[:lo] * jnp.exp(rb - b[:lo]),
                                      jnp.zeros((CHUNK - lo, HD), F32)], axis=0).astype(BF16)
                off.append(_dot_nt(qt, kt))
            h_off.append(off)
            h_inter.append(_dot_nt((q * jnp.exp(b)).astype(BF16), stt.astype(BF16)))
            bend = b[CHUNK - 1:CHUNK, :]
            kd = (k * jnp.exp(bend - b)).astype(BF16)
            hgt_ref[s, h] = stt * jnp.exp(bend) + _dot_tn(v, kd)
            h_q.append(q)
            h_k.append(k)
            h_b.append(b)
            h_v.append(v)

        for n, (s, r0, h) in enumerate(streams):
            o = _dot((r_att[n] * intra[h]).astype(BF16), r_v[n]) + r_inter[n]
            o = _head_norm(o, rg_ref[:, pl.ds(h * HD, HD)]) * ld(rgate_ref, r0, h)
            mix_ref[pl.ds(r0, CHUNK), pl.ds(h * HD, HD)] = o.astype(BF16)

        h_att = []
        for n in range(len(streams)):
            b4 = h_b[n].reshape(nb, SUB, HD)
            q4 = h_q[n].reshape(nb, SUB, HD)
            k4 = h_k[n].reshape(nb, SUB, HD)
            diag = jnp.zeros((nb, SUB, CHUNK), F32)
            for sp in range(SUB):
                dec = jnp.exp(b4 - b4[:, sp:sp + 1, :])
                r = jnp.sum(dec * q4 * k4[:, sp:sp + 1, :], axis=-1, keepdims=True)
                diag = jnp.where(lane_io == blk_io * SUB + sp, r, diag)
            diag = jnp.where(lane_io - blk_io * SUB <= trow_io, diag, 0.0).reshape(CHUNK, CHUNK)
            parts = [diag[0:SUB]]
            for blk in range(1, nb):
                parts.append(diag[blk * SUB:(blk + 1) * SUB] + h_off[n][blk - 1])
            h_att.append(jnp.concatenate(parts, axis=0).astype(BF16))
        for n, (s, r0, h) in enumerate(streams):
            o = _dot(h_att[n], h_v[n]) + h_inter[n]
            o = _head_norm(o, hg_ref[:, pl.ds(h * HD, HD)]) * ld(hgate_ref, r0, h)
            mix_ref[pl.ds(r0, CHUNK), pl.ds(GRP + h * HD, HD)] = o.astype(BF16)
        return carry

    lanes = MIXER_LANES if ns % MIXER_LANES == 0 else 1
    for i in range(ns * nc // lanes):
        chunk_body(i, 0)

    xo_ref[...] = (x + _dot(mix_ref[...], wout_ref[...])).reshape(xo_ref.shape)

    @pl.when(t == pl.num_programs(1) - 1)
    def _():
        for s in range(ns):
            for h in heads:
                shg_o[s, h] = hgt_ref[s, h].T


def _even_mixer(x, cos, sin, norm_g, w_in, w_out, ret_g, hg_g, lb_logits, s_ret, s_hg, layer_j):
    b, t, _ = x.shape
    ns, nc = _seq_tiling(b, t, MIXER_LANES)
    tt = ns * nc * CHUNK
    grid = (b // ns, t // (nc * CHUNK))
    st_spec = pl.BlockSpec((ns, HEADS, HD, HD), lambda i, j: (i, 0, 0, 0))
    x_spec = pl.BlockSpec((ns, nc * CHUNK, D_MODEL), lambda i, j: (i, j, 0))

    def full(a):
        return _resident(a.shape, lambda i, j: (0,) * a.ndim)

    tab_spec = pl.BlockSpec((nc * CHUNK, HD), lambda i, j: (j, 0))
    args = (x, cos, sin, norm_g.reshape(1, D_MODEL), w_in, w_out, ret_g.reshape(1, GRP),
            hg_g.reshape(1, GRP), lb_logits, s_ret, s_hg)
    in_specs = [x_spec, tab_spec, tab_spec] + [full(a) for a in args[3:9]] + [st_spec, st_spec]
    grp_bf16 = pltpu.VMEM((tt, GRP), BF16)
    grp_f32 = pltpu.VMEM((tt, GRP), F32)
    return pl.pallas_call(
        functools.partial(_even_kernel, ns=ns, nc=nc, layer_j=layer_j),
        grid=grid,
        in_specs=in_specs,
        out_specs=[x_spec, st_spec, st_spec],
        out_shape=[jax.ShapeDtypeStruct(x.shape, F32),
                   jax.ShapeDtypeStruct(s_ret.shape, F32),
                   jax.ShapeDtypeStruct(s_hg.shape, F32)],
        scratch_shapes=[grp_bf16] * 6 + [grp_f32] * 4
        + [pltpu.VMEM((tt, D_MODEL), BF16), pltpu.VMEM((ns, HEADS, HD, HD), F32)],
        compiler_params=pltpu.CompilerParams(
            dimension_semantics=("parallel", "arbitrary"), vmem_limit_bytes=VMEM_LIMIT),
        name="even_mixer",
    )(*args)


def _softplus(x):
    return jnp.maximum(x, 0.0) + jnp.log1p(jnp.exp(-jnp.abs(x)))


def _gelu_tanh(x):
    return 0.5 * x * (1.0 + jnp.tanh(math.sqrt(2.0 / math.pi) * (x + 0.044715 * (x * x * x))))


def _odd_kernel(x_ref, ng_ref, win_ref, wt_ref, wout_ref, cw_ref, lcb_ref, wa_ref, ba_ref,
                wx_ref, bx_ref, lam_ref, dnp_ref, dg_ref, lh_ref, lc_ref, ds_ref, dc_ref,
                xo_ref, lh_o, lc_o, ds_o, dc_o,
                z_ref, gl_ref, sg_ref, bt_ref, gt_ref, gtt_ref, mix_ref, xp_ref, *, ns, nc):
    t = pl.program_id(1)
    tt = ns * nc * CHUNK
    conv_ch = LRU_WIDTH + 3 * GRP
    hist = CONV_W - 1
    top = 8
    heads = range(HEADS)

    @pl.when(t == 0)
    def _():
        lh_o[...] = lh_ref[...]
        lc_o[...] = lc_ref[...]
        ds_o[...] = ds_ref[...]
        dc_o[...] = dc_ref[...]

    x = x_ref[...].reshape(tt, D_MODEL)
    hn = _rms(x, ng_ref[...]).astype(BF16)

    def proj(lo):
        return _dot(hn, win_ref[:, lo:lo + GRP])

    zs = _dot(hn, wt_ref[...])
    z5 = proj(5 * GRP)
    bt_ref[...] = jax.nn.sigmoid(zs)
    g_all = _chunk_cumsum(-jnp.exp(dnp_ref[1:2, :]) * _softplus(zs + dnp_ref[0:1, :]))
    gt_ref[...] = g_all
    for c in range(ns * nc):
        gtt_ref[c] = g_all[c * CHUNK:(c + 1) * CHUNK, :].T
    z1 = proj(GRP)
    sg_ref[...] = _silu(z5)
    z_ref[:, 0:GRP] = proj(0)
    gl_ref[...] = _gelu_tanh(z1)
    for g in range(3):
        z_ref[:, (g + 1) * GRP:(g + 2) * GRP] = proj((g + 2) * GRP)

    sp_lam = _softplus(-lam_ref[...])
    ri = lax.broadcasted_iota(jnp.int32, (CHUNK, CHUNK), 0)
    ci = lax.broadcasted_iota(jnp.int32, (CHUNK, CHUNK), 1)
    rows_w = lax.broadcasted_iota(jnp.int32, (CHUNK, LRU_WIDTH), 0)
    eye_c = (ri == ci).astype(F32)
    sub_bits = DN_SUB.bit_length() - 1
    same_blk = (ri >> sub_bits) == (ci >> sub_bits)
    merge_masks = []
    for lvl in range(sub_bits, CHUNK.bit_length() - 1):
        merge_masks.append(((ri >> (lvl + 1)) == (ci >> (lvl + 1)))
                           & (((ri >> lvl) & 1) == 1) & (((ci >> lvl) & 1) == 0))

    def chunk_body(i, carry):
        lane_seq, lane_rows, lane_chunk, ys = [], [], [], []
        for u in range(lanes):
            s, c = (i * lanes + u, 0) if nc == 1 else (u, i)
            r0 = (s * nc + c) * CHUNK
            rows = pl.ds(r0, CHUNK)
            lane_seq.append(s)
            lane_rows.append(rows)
            lane_chunk.append(s * nc + c)

            xp_ref[u, top - hist:top, 0:LRU_WIDTH] = lc_o[s]
            xp_ref[u, top - hist:top, LRU_WIDTH:conv_ch] = dc_o[s]
            xp_ref[u, top:top + CHUNK, :] = z_ref[rows, :]
            lc_o[s] = xp_ref[u, top + CHUNK - hist:top + CHUNK, 0:LRU_WIDTH]
            dc_o[s] = xp_ref[u, top + CHUNK - hist:top + CHUNK, LRU_WIDTH:conv_ch]
            y = xp_ref[u, top - hist:top - hist + CHUNK, :] * cw_ref[0:1, :]
            for jj in range(1, CONV_W):
                y = y + (xp_ref[u, top - hist + jj:top - hist + jj + CHUNK, :]
                         * cw_ref[jj:jj + 1, :])
            ys.append(y)

        def rg_lru_branch(lane):
            y, s, rows = ys[lane], lane_seq[lane], lane_rows[lane]
            lx = y[:, 0:LRU_WIDTH] + lcb_ref[...]
            xb = lx.astype(BF16)
            r = jax.nn.sigmoid(_dot(xb, wa_ref[...]) + ba_ref[...])
            ig = jax.nn.sigmoid(_dot(xb, wx_ref[...]) + bx_ref[...])
            a = jnp.exp(-LRU_C * r * sp_lam)
            u = jnp.sqrt(1.0 - a * a) * (ig * lx)
            k = 1
            while k < CHUNK:
                m = rows_w >= k
                u = jnp.where(m, a * pltpu.roll(u, k, 0) + u, u)
                a = jnp.where(m, a * pltpu.roll(a, k, 0), a)
                k *= 2
            hseq = a * lh_o[s] + u
            lh_o[s] = hseq[CHUNK - 1:CHUNK]
            mix_ref[rows, 0:LRU_WIDTH] = (gl_ref[rows, :] * hseq).astype(BF16)

        streams = [(u, h) for u in range(lanes) for h in heads]
        ns_ = range(len(streams))
        beta_t = [bt_ref[rows, :] for rows in lane_rows]
        g_t = [gt_ref[rows, :] for rows in lane_rows]
        g_tt = [gtt_ref[c] for c in lane_chunk]
        states = [ds_o[lane_seq[u], h] for u, h in streams]
        kn, both_k, both_s = [], [], []
        for n, (u, h) in enumerate(streams):
            y = ys[u]
            base = LRU_WIDTH + h * HD
            q = _silu(y[:, base:base + HD])
            kk_ = _silu(y[:, base + GRP:base + GRP + HD])
            q = q * lax.rsqrt(jnp.sum(q * q, axis=-1, keepdims=True) + EPS) * (HD ** -0.5)
            kn.append(kk_ * lax.rsqrt(jnp.sum(kk_ * kk_, axis=-1, keepdims=True) + EPS))
            knb = kn[n].astype(BF16)
            qk_in = jnp.concatenate([q.astype(BF16), knb], axis=0)
            both_k.append(_dot_nt(qk_in, knb))
            both_s.append(_dot(qk_in, states[n].astype(BF16)))
        gcol, eg, e_incl, amat, rhs = [], [], [], [], []
        for n, (u, h) in enumerate(streams):
            y = ys[u]
            v = _silu(y[:, LRU_WIDTH + 2 * GRP + h * HD:LRU_WIDTH + 2 * GRP + (h + 1) * HD])
            gcol.append(g_t[u][:, HEADS + h:HEADS + h + 1])
            rel = gcol[n] - g_tt[u][HEADS + h:HEADS + h + 1, :]
            e_strict = jnp.exp(jnp.where(ri > ci, rel, -jnp.inf))
            e_incl.append(jnp.where(ri == ci, 1.0, e_strict))
            beta = beta_t[u][:, h:h + 1]
            eg.append(jnp.exp(gcol[n]))
            amat.append(beta * both_k[n][CHUNK:] * e_strict)
            rhs.append(beta * (v - eg[n] * both_s[n][CHUNK:]))
        dblk = [jnp.where(same_blk, amat[n], 0.0) for n in ns_]
        tinv = [eye_c for _ in ns_]
        for jj in range(DN_SUB - 1):
            for n in ns_:
                colv = jnp.concatenate(
                    [dblk[n][b0:b0 + DN_SUB, b0 + jj:b0 + jj + 1]
                     for b0 in range(0, CHUNK, DN_SUB)], axis=0)
                rowm = jnp.concatenate(
                    [jnp.broadcast_to(tinv[n][b0 + jj:b0 + jj + 1, :], (DN_SUB, CHUNK))
                     for b0 in range(0, CHUNK, DN_SUB)], axis=0)
                tinv[n] = tinv[n] - colv * rowm
        for lane in range(lanes):
            rg_lru_branch(lane)
        tb = [tinv[n].astype(BF16) for n in ns_]
        pend = [[_dot(tb[n], jnp.where(lm, amat[n], 0.0).astype(BF16)) for lm in merge_masks]
                for n in ns_]
        for _ in merge_masks:
            xb_ = [pend[n][0].astype(BF16) for n in ns_]
            tb = [tinv[n].astype(BF16) for n in ns_]
            tinv = [tinv[n] - _dot(xb_[n], tb[n]) for n in ns_]
            pend = [[p - _dot(xb_[n], p.astype(BF16)) for p in pend[n][1:]] for n in ns_]
        w = [_dot(tinv[n].astype(BF16), rhs[n].astype(BF16)).astype(BF16) for n in ns_]
        for n, (u, h) in enumerate(streams):
            s, rows = lane_seq[u], lane_rows[u]
            o = eg[n] * both_s[n][:CHUNK] + _dot((both_k[n][:CHUNK] * e_incl[n]).astype(BF16), w[n])
            gend = gcol[n][CHUNK - 1:CHUNK, :]
            kd = (kn[n] * jnp.exp(gend - gcol[n])).astype(BF16)
            ds_o[s, h] = states[n] * jnp.exp(gend) + _dot_tn(kd, w[n])
            o = _head_norm(o, dg_ref[:, pl.ds(h * HD, HD)]) * sg_ref[rows, pl.ds(h * HD, HD)]
            mix_ref[rows, pl.ds(LRU_WIDTH + h * HD, HD)] = o.astype(BF16)
        return carry

    lanes = ODD_LANES if ns % ODD_LANES == 0 else 1
    for i in range(ns * nc // lanes):
        chunk_body(i, 0)

    xo_ref[...] = (x + _dot(mix_ref[...], wout_ref[...])).reshape(xo_ref.shape)


def _odd_mixer(x, norm_g, w_in, w_tail, w_out, cw, lcb, wa, ba, wx, bx, lam, dnp, dn_g,
               s_lh, s_lc, s_dn, s_dc):
    b, t, _ = x.shape
    ns, nc = _seq_tiling(b, t, ODD_LANES)
    tt = ns * nc * CHUNK
    grid = (b // ns, t // (nc * CHUNK))
    x_spec = pl.BlockSpec((ns, nc * CHUNK, D_MODEL), lambda i, j: (i, j, 0))

    def full(a):
        return _resident(a.shape, lambda i, j: (0,) * a.ndim)

    def st_spec(a):
        return pl.BlockSpec((ns,) + a.shape[1:], lambda i, j: (i,) + (0,) * (a.ndim - 1))

    params = (norm_g.reshape(1, D_MODEL), w_in, w_tail, w_out, cw, lcb, wa, ba, wx, bx, lam, dnp,
              dn_g)
    states = (s_lh, s_lc, s_dn, s_dc)
    grp_f32 = pltpu.VMEM((tt, GRP), F32)
    head_f32 = pltpu.VMEM((tt, HD), F32)
    return pl.pallas_call(
        functools.partial(_odd_kernel, ns=ns, nc=nc),
        grid=grid,
        in_specs=[x_spec] + [full(a) for a in params] + [st_spec(a) for a in states],
        out_specs=[x_spec] + [st_spec(a) for a in states],
        out_shape=[jax.ShapeDtypeStruct(x.shape, F32)]
        + [jax.ShapeDtypeStruct(a.shape, F32) for a in states],
        scratch_shapes=[pltpu.VMEM((tt, LRU_WIDTH + 3 * GRP), F32), grp_f32, grp_f32,
                        head_f32, head_f32, pltpu.VMEM((ns * nc, HD, CHUNK), F32),
                        pltpu.VMEM((tt, D_MODEL), BF16),
                        pltpu.VMEM((ODD_LANES, 8 + CHUNK, LRU_WIDTH + 3 * GRP), F32)],
        compiler_params=pltpu.CompilerParams(
            dimension_semantics=("parallel", "arbitrary"), vmem_limit_bytes=VMEM_LIMIT),
        name="odd_mixer",
    )(x, *params, *states)


def _rope_tables(pos0, n):
    half = HD // 2
    freq = ROPE_BASE ** (-jnp.arange(half, dtype=F32) / half)
    ang_a = (pos0 + CHUNK * jnp.arange(n // CHUNK)).astype(F32)[:, None, None] * freq
    ang_b = jnp.arange(CHUNK, dtype=F32)[None, :, None] * freq
    cos_a, sin_a, cos_b, sin_b = jnp.cos(ang_a), jnp.sin(ang_a), jnp.cos(ang_b), jnp.sin(ang_b)
    cos = (cos_a * cos_b - sin_a * sin_b).reshape(n, half)
    sin = (sin_a * cos_b + cos_a * sin_b).reshape(n, half)
    return jnp.concatenate([cos, cos], axis=-1), jnp.concatenate([-sin, sin], axis=-1)


def _block_diag(w):
    n, bs, _ = w.shape
    eye = jnp.eye(n, dtype=w.dtype)
    return (eye[:, None, :, None] * w[:, :, None, :]).reshape(n * bs, n * bs)


def kernel(x_prompt, x_sample, state_ret, state_hgrn, state_lru_h, state_lru_conv, state_dn,
           state_dn_conv, ffn1_norm, ffn1_w_in, ffn1_w_out, mix_norm, ffn2_norm, ffn2_w_in,
           ffn2_w_out, final_norm, even_w_in, even_w_out, ret_out_norm, hg_out_norm,
           hg_lb_logits, odd_w_in, odd_w_out, lru_conv_w, lru_conv_b, lru_w_a, lru_b_a, lru_w_x,
           lru_b_x, lru_lambda, dn_conv_w, dn_a_log, dn_dt_bias, dn_out_norm):
    depth = ffn1_norm.shape[0]
    bp, tp, _ = x_prompt.shape
    bs, ts, _ = x_sample.shape
    tabs = (_rope_tables(0, tp), _rope_tables(PAST_LEN, ts))
    xs = [x_prompt, x_sample]
    nb = (bp, bs)
    outs = {k: ([], []) for k in ("ret", "hg", "lh", "lc", "dn", "dc")}

    ffn_sets = []
    for l in range(depth):
        ffn_sets += [(ffn1_norm[l], ffn1_w_in, ffn1_w_out, l),
                     (ffn2_norm[l], ffn2_w_in, ffn2_w_out, l)]
    ffn_w = [(ffn1_w_in[0].astype(BF16), ffn1_w_out[0].astype(BF16))]

    def run_ffn(apply_final, mixer_casts=()):
        k = len(ffn_w) - 1
        casts = []
        if k + 1 < len(ffn_sets):
            _, nwi, nwo, nl = ffn_sets[k + 1]
            casts = [(nwi, nl, FFN_CAST_ROWS, nwi.shape[-1]),
                     (nwo, nl, FFN_CAST_ROWS, nwo.shape[-1])]
        casts += list(mixer_casts)
        xs[0], xs[1], cast = _ffn(xs[0], xs[1], ffn_sets[k][0], ffn_w[k][0], ffn_w[k][1],
                                  final_norm, apply_final, casts)
        ffn_w.append(tuple(cast[:2]) if k + 1 < len(ffn_sets) else ())
        return cast[len(cast) - len(mixer_casts):]

    for l in range(depth):
        j = l // 2
        last = l == depth - 1
        if l % 2 == 0:
            w_in, w_out = run_ffn(False, [(even_w_in, j, None, EVEN_IN),
                                          (even_w_out, j, None, D_MODEL)])
            for g in range(2):
                if g == 0:
                    s_ret = jnp.zeros((bp, HEADS, HD, HD), F32)
                    s_hg = jnp.zeros((bp, HEADS, HD, HD), F32)
                else:
                    s_ret, s_hg = state_ret[j], state_hgrn[j]
                xs[g], n_ret, n_hg = _even_mixer(
                    xs[g], tabs[g][0], tabs[g][1], mix_norm[l], w_in, w_out, ret_out_norm[j],
                    hg_out_norm[j], hg_lb_logits, s_ret, s_hg, j)
                outs["ret"][g].append(n_ret)
                outs["hg"][g].append(n_hg)
        else:
            w_in, w_out = run_ffn(False, [(odd_w_in, j, None, 6 * GRP),
                                          (odd_w_out, j, None, D_MODEL)])
            n_tail = odd_w_in.shape[-1] - 6 * GRP
            tail = lax.optimization_barrier(odd_w_in[j][:, 6 * GRP:])
            w_tail = jnp.pad(tail, ((0, 0), (0, HD - n_tail))).astype(BF16)
            cw = jnp.concatenate([lru_conv_w[j], dn_conv_w[j]], axis=-1)
            wa = _block_diag(lru_w_a[j]).astype(BF16)
            wx = _block_diag(lru_w_x[j]).astype(BF16)
            dnp = jnp.zeros((2, HD), F32)
            dnp = dnp.at[0, HEADS:2 * HEADS].set(dn_dt_bias[j])
            dnp = dnp.at[1, HEADS:2 * HEADS].set(dn_a_log[j])
            for g in range(2):
                if g == 0:
                    s_lh = jnp.zeros((bp, 1, LRU_WIDTH), F32)
                    s_lc = jnp.zeros((bp, CONV_W - 1, LRU_WIDTH), F32)
                    s_dn = jnp.zeros((bp, HEADS, HD, HD), F32)
                    s_dc = jnp.zeros((bp, CONV_W - 1, 3 * GRP), F32)
                else:
                    s_lh = state_lru_h[j].reshape(bs, 1, LRU_WIDTH)
                    s_lc, s_dn, s_dc = state_lru_conv[j], state_dn[j], state_dn_conv[j]
                xs[g], n_lh, n_lc, n_dn, n_dc = _odd_mixer(
                    xs[g], mix_norm[l], w_in, w_tail, w_out, cw, lru_conv_b[j].reshape(1, -1), wa,
                    lru_b_a[j].reshape(1, -1), wx, lru_b_x[j].reshape(1, -1),
                    lru_lambda[j].reshape(1, -1), dnp, dn_out_norm[j].reshape(1, -1),
                    s_lh, s_lc, s_dn, s_dc)
                outs["lh"][g].append(n_lh.reshape(nb[g], LRU_WIDTH))
                outs["lc"][g].append(n_lc)
                outs["dn"][g].append(n_dn)
                outs["dc"][g].append(n_dc)
        run_ffn(last)

    res = [xs[0], xs[1]]
    for k in ("ret", "hg", "lh", "lc", "dn", "dc"):
        for g in range(2):
            res.append(jnp.stack(outs[k][g]))
    return tuple(res)
```

```python
import functools
import math

import jax
import jax.numpy as jnp
from jax import lax
from jax.experimental import pallas as pl
from jax.experimental.pallas import tpu as pltpu

F32 = jnp.float32
BF16 = jnp.bfloat16

D_MODEL = 1024
FF_DIM = 2816
EPS = 1e-6
CHUNK = 64
CONV_W = 4
HEADS = 4
HD = 128
ROPE_BASE = 10000.0
PAST_LEN = 2048
LRU_WIDTH = 512
LRU_C = 8.0
GRP = HEADS * HD
EVEN_IN = 8 * GRP
SUB = 8
DN_SUB = 4

VMEM_LIMIT = 56 * 1024 * 1024

FFN_TM = 512
FFN_TF = 256
FFN_ROW_SPLIT = 2
FFN_CAST_ROWS = 16
SEQ_TILE = 512
MIXER_LANES = 1
ODD_LANES = 2

RET_LOG_G = [math.log1p(-(2.0 ** (-5.0 - h))) for h in range(HEADS)]


def _rms(x, g):
    return x * lax.rsqrt(jnp.mean(x * x, axis=-1, keepdims=True) + EPS) * g


def _silu(x):
    return x * jax.nn.sigmoid(x)


def _dot(a, b):
    return jnp.dot(a, b, preferred_element_type=F32)


def _dot_nt(a, b):
    return lax.dot_general(a, b, (((1,), (1,)), ((), ())), preferred_element_type=F32)


def _dot_tn(a, b):
    return lax.dot_general(a, b, (((0,), (0,)), ((), ())), preferred_element_type=F32)


def _head_norm(o, g):
    return o * lax.rsqrt(jnp.mean(o * o, axis=-1, keepdims=True) + EPS) * g


def _hcols(h):
    return slice(h * HD, (h + 1) * HD)


def _chunk_cumsum(x):
    pos = lax.broadcasted_iota(jnp.int32, x.shape, 0) & (CHUNK - 1)
    k = 1
    while k < CHUNK:
        x = jnp.where(pos >= k, x + pltpu.roll(x, k, 0), x)
        k *= 2
    return x


def _resident(block_shape, index_map):
    return pl.BlockSpec(block_shape, index_map, pipeline_mode=pl.Buffered(1))


def _ffn_kernel(*refs, n_first, apply_final, n_casts):
    xa_ref, xb_ref, g_ref, win_ref, wout_ref, fin_ref = refs[:6]
    cast_in = refs[6:6 + n_casts]
    oa_ref, ob_ref = refs[6 + n_casts:8 + n_casts]
    cast_out = refs[8 + n_casts:8 + 2 * n_casts]
    act_ref = refs[8 + 2 * n_casts]
    for src, dst in zip(cast_in, cast_out):
        dst[...] = src[...].astype(BF16)
    i = pl.program_id(0)

    def tile(x_ref, o_ref):
        tm = x_ref.shape[0]
        halves = [slice(r, r + tm // FFN_ROW_SPLIT) for r in range(0, tm, tm // FFN_ROW_SPLIT)]
        hs = [_rms(x_ref[rows, :], g_ref[...]).astype(BF16) for rows in halves]
        for c in range(FF_DIM // FFN_TF):
            lo = c * FFN_TF
            for rows, h in zip(halves, hs):
                gate = _dot(h, win_ref[:, lo:lo + FFN_TF])
                up = _dot(h, win_ref[:, FF_DIM + lo:FF_DIM + lo + FFN_TF])
                act_ref[rows, lo:lo + FFN_TF] = (_silu(gate) * up).astype(BF16)
        for rows in halves:
            y = x_ref[rows, :] + 0.5 * _dot(act_ref[rows, :], wout_ref[...])
            if apply_final:
                y = _rms(y, fin_ref[...])
            o_ref[rows, :] = y

    @pl.when(i < n_first)
    def _():
        tile(xa_ref, oa_ref)

    @pl.when(i >= n_first)
    def _():
        tile(xb_ref, ob_ref)


def _ffn(xa, xb, norm_g, w_in, w_out, final_g, apply_final, casts=()):
    sa, sb = xa.shape, xb.shape
    xa2, xb2 = xa.reshape(-1, D_MODEL), xb.reshape(-1, D_MODEL)
    tm = min(FFN_TM, xa2.shape[0], xb2.shape[0])
    na, nb = xa2.shape[0] // tm, xb2.shape[0] // tm
    x_blk = (tm, D_MODEL)
    in_specs = [
        pl.BlockSpec(x_blk, lambda i: (jnp.minimum(i, na - 1), 0)),
        pl.BlockSpec(x_blk, lambda i: (jnp.maximum(i - na, 0), 0)),
        _resident((1, D_MODEL), lambda i: (0, 0)),
        _resident((D_MODEL, 2 * FF_DIM), lambda i: (0, 0)),
        _resident((FF_DIM, D_MODEL), lambda i: (0, 0)),
        _resident((1, D_MODEL), lambda i: (0, 0)),
    ]
    out_specs = [
        pl.BlockSpec(x_blk, lambda i: (jnp.minimum(i, na - 1), 0)),
        pl.BlockSpec(x_blk, lambda i: (jnp.maximum(i - na, 0), 0)),
    ]
    out_shape = [jax.ShapeDtypeStruct(xa2.shape, F32), jax.ShapeDtypeStruct(xb2.shape, F32)]
    args = [xa2, xb2, norm_g.reshape(1, D_MODEL), w_in, w_out, final_g.reshape(1, D_MODEL)]
    cast_in_specs, cast_out_specs = [], []
    for arr, layer, row_blocks, c, r in casts:
        ncol = na // (row_blocks or na)
        nrow = na // ncol
        rows = r // nrow
        assert layer == 0 or arr.shape[1] % rows == 0

        def blk(i, ncol=ncol):
            i = jnp.minimum(i, na - 1)
            return i // ncol, i % ncol

        def blk_in(i, blk=blk, first=layer * (arr.shape[1] // rows)):
            rb, cb = blk(i)
            return first + rb, cb

        cast_in_specs.append(pl.BlockSpec((rows, c // ncol), blk_in))
        cast_out_specs.append(pl.BlockSpec((rows, c // ncol), blk))
        out_shape.append(jax.ShapeDtypeStruct((r, c), BF16))
        args.append(arr.reshape(-1, arr.shape[-1]))
    outs = pl.pallas_call(
        functools.partial(_ffn_kernel, n_first=na, apply_final=apply_final,
                          n_casts=len(casts)),
        grid=(na + nb,),
        in_specs=in_specs + cast_in_specs,
        out_specs=out_specs + cast_out_specs,
        out_shape=out_shape,
        scratch_shapes=[pltpu.VMEM((tm, FF_DIM), BF16)],
        compiler_params=pltpu.CompilerParams(
            dimension_semantics=("arbitrary",), vmem_limit_bytes=VMEM_LIMIT),
        name="ffn",
    )(*args)
    return outs[0].reshape(sa), outs[1].reshape(sb), list(outs[2:])


def _seq_tiling(b, t, lanes=1):
    if t >= SEQ_TILE:
        lanes = lanes if b % lanes == 0 else 1
        return lanes, SEQ_TILE // CHUNK // lanes
    return min(b, SEQ_TILE // t), t // CHUNK


def _even_kernel(x_ref, cos_ref, sin_ref, ng_ref, win_ref, wout_ref, rg_ref, hg_ref, lbl_ref,
                 sret_ref, shg_ref, xo_ref, sret_o, shg_o,
                 qs_ref, qd_ref, kb_ref, kd_ref, rv_ref, hv_ref,
                 rgate_ref, hq_ref, hf_ref, hgate_ref, mix_ref, hgt_ref, *, ns, nc, layer_j):
    t = pl.program_id(1)
    tt = ns * nc * CHUNK
    heads = range(HEADS)

    @pl.when(t == 0)
    def _():
        sret_o[...] = sret_ref[...]
        for s in range(ns):
            for h in heads:
                hgt_ref[s, h] = shg_ref[s, h].T

    x = x_ref[...].reshape(tt, D_MODEL)
    hn = _rms(x, ng_ref[...]).astype(BF16)

    def proj(g):
        return _dot(hn, win_ref[:, g * GRP:(g + 1) * GRP])

    cos = jnp.concatenate([cos_ref[...]] * ns, axis=0)
    sin = jnp.concatenate([sin_ref[...]] * ns, axis=0)
    scale = HD ** -0.5
    pos = lax.broadcasted_iota(jnp.int32, (CHUNK, HD), 0).astype(F32)

    def tiled(v):
        return jnp.concatenate([v] * (tt // CHUNK), axis=0)

    qdec = [tiled(jnp.exp(RET_LOG_G[h] * (pos + 1.0)) * scale) for h in heads]
    kdec = [tiled(jnp.exp(RET_LOG_G[h] * (CHUNK - 1.0 - pos))) for h in heads]

    def rope(z, h):
        v = z[:, _hcols(h)]
        return v * cos + pltpu.roll(v, HD // 2, 1) * sin

    z0 = proj(0)
    z1 = proj(1)
    for h in heads:
        q = rope(z0, h)
        qs_ref[:, _hcols(h)] = (q * scale).astype(BF16)
        qd_ref[:, _hcols(h)] = (q * qdec[h]).astype(BF16)
    z2 = proj(2)
    for h in heads:
        k = rope(z1, h)
        kb_ref[:, _hcols(h)] = k.astype(BF16)
        kd_ref[:, _hcols(h)] = (k * kdec[h]).astype(BF16)
    z3 = proj(3)
    rv_ref[...] = z2.astype(BF16)
    z4 = proj(4)
    rgate_ref[...] = _silu(z3)
    z5 = proj(5)
    hq_ref[...] = _silu(z4)
    z6 = proj(6)
    hf_ref[...] = z5
    z7 = proj(7)
    hv_ref[...] = z6.astype(BF16)
    hgate_ref[...] = _silu(z7)

    ri = lax.broadcasted_iota(jnp.int32, (CHUNK, CHUNK), 0)
    ci = lax.broadcasted_iota(jnp.int32, (CHUNK, CHUNK), 1)
    absd = jnp.abs(ri - ci).astype(F32)
    intra = [jnp.exp(RET_LOG_G[h] * absd) for h in heads]
    sdec = [math.exp(RET_LOG_G[h] * CHUNK) for h in heads]

    lbl = lbl_ref[...]
    e = jnp.exp(lbl - jnp.max(lbl, axis=0, keepdims=True))
    sm = e / jnp.sum(e, axis=0, keepdims=True)
    lb_all = jnp.sum(sm[:layer_j + 1], axis=0, keepdims=True)

    nb = CHUNK // SUB
    trow_io = lax.broadcasted_iota(jnp.int32, (nb, SUB, CHUNK), 1)
    lane_io = lax.broadcasted_iota(jnp.int32, (nb, SUB, CHUNK), 2)
    blk_io = lax.broadcasted_iota(jnp.int32, (nb, SUB, CHUNK), 0)

    def chunk_body(i, carry):
        streams = []
        for u in range(lanes):
            s, c = (i * lanes + u, 0) if nc == 1 else (u, i)
            r0 = (s * nc + c) * CHUNK
            streams += [(s, r0, h) for h in heads]

        def ld(ref, r0, h):
            return ref[pl.ds(r0, CHUNK), pl.ds(h * HD, HD)]

        ret_states = [sret_o[s, h] for s, _, h in streams]
        hg_states = [hgt_ref[s, h] for s, _, h in streams]

        r_v, r_att, r_inter = [], [], []
        for n, (s, r0, h) in enumerate(streams):
            st = ret_states[n]
            r_v.append(ld(rv_ref, r0, h))
            r_att.append(_dot_nt(ld(qs_ref, r0, h), ld(kb_ref, r0, h)))
            r_inter.append(_dot(ld(qd_ref, r0, h), st.astype(BF16)))
            sret_o[s, h] = st * sdec[h] + _dot_tn(ld(kd_ref, r0, h), r_v[n])

        h_q, h_k, h_b, h_v, h_off, h_inter = [], [], [], [], [], []
        for n, (s, r0, h) in enumerate(streams):
            lb = lb_all[:, _hcols(h)]
            sig = jax.nn.sigmoid(ld(hf_ref, r0, h))
            v = ld(hv_ref, r0, h)
            q = ld(hq_ref, r0, h)
            k = (1.0 - lb) * (1.0 - sig)
            b = _chunk_cumsum(jnp.log(lb + (1.0 - lb) * sig))
            stt = hg_states[n]
            off = []
            for blk in range(1, nb):
                lo = blk * SUB
                rb = b[lo - 1:lo, :]
                qt = (q[lo:lo + SUB] * jnp.exp(b[lo:lo + SUB] - rb)).astype(BF16)
                kt = jnp.concatenate([k[:lo] * jnp.exp(rb - b[:lo]),
                                      jnp.zeros((CHUNK - lo, HD), F32)], axis=0).astype(BF16)
                off.append(_dot_nt(qt, kt))
            h_off.append(off)
            h_inter.append(_dot_nt((q * jnp.exp(b)).astype(BF16), stt.astype(BF16)))
            bend = b[CHUNK - 1:CHUNK, :]
            kd = (k * jnp.exp(bend - b)).astype(BF16)
            hgt_ref[s, h] = stt * jnp.exp(bend) + _dot_tn(v, kd)
            h_q.append(q)
            h_k.append(k)
            h_b.append(b)
            h_v.append(v)

        for n, (s, r0, h) in enumerate(streams):
            o = _dot((r_att[n] * intra[h]).astype(BF16), r_v[n]) + r_inter[n]
            o = _head_norm(o, rg_ref[:, pl.ds(h * HD, HD)]) * ld(rgate_ref, r0, h)
            mix_ref[pl.ds(r0, CHUNK), pl.ds(h * HD, HD)] = o.astype(BF16)

        h_att = []
        for n in range(len(streams)):
            b4 = h_b[n].reshape(nb, SUB, HD)
            q4 = h_q[n].reshape(nb, SUB, HD)
            k4 = h_k[n].reshape(nb, SUB, HD)
            diag = jnp.zeros((nb, SUB, CHUNK), F32)
            for sp in range(SUB):
                dec = jnp.exp(b4 - b4[:, sp:sp + 1, :])
                r = jnp.sum(dec * q4 * k4[:, sp:sp + 1, :], axis=-1, keepdims=True)
                diag = jnp.where(lane_io == blk_io * SUB + sp, r, diag)
            diag = jnp.where(lane_io - blk_io * SUB <= trow_io, diag, 0.0).reshape(CHUNK, CHUNK)
            parts = [diag[0:SUB]]
            for blk in range(1, nb):
                parts.append(diag[blk * SUB:(blk + 1) * SUB] + h_off[n][blk - 1])
            h_att.append(jnp.concatenate(parts, axis=0).astype(BF16))
        for n, (s, r0, h) in enumerate(streams):
            o = _dot(h_att[n], h_v[n]) + h_inter[n]
            o = _head_norm(o, hg_ref[:, pl.ds(h * HD, HD)]) * ld(hgate_ref, r0, h)
            mix_ref[pl.ds(r0, CHUNK), pl.ds(GRP + h * HD, HD)] = o.astype(BF16)
        return carry

    lanes = MIXER_LANES if ns % MIXER_LANES == 0 else 1
    for i in range(ns * nc // lanes):
        chunk_body(i, 0)

    xo_ref[...] = (x + _dot(mix_ref[...], wout_ref[...])).reshape(xo_ref.shape)

    @pl.when(t == pl.num_programs(1) - 1)
    def _():
        for s in range(ns):
            for h in heads:
                shg_o[s, h] = hgt_ref[s, h].T


def _even_mixer(x, cos, sin, norm_g, w_in, w_out, ret_g, hg_g, lb_logits, s_ret, s_hg, layer_j):
    b, t, _ = x.shape
    ns, nc = _seq_tiling(b, t, MIXER_LANES)
    tt = ns * nc * CHUNK
    grid = (b // ns, t // (nc * CHUNK))
    st_spec = pl.BlockSpec((ns, HEADS, HD, HD), lambda i, j: (i, 0, 0, 0))
    x_spec = pl.BlockSpec((ns, nc * CHUNK, D_MODEL), lambda i, j: (i, j, 0))

    def full(a):
        return _resident(a.shape, lambda i, j: (0,) * a.ndim)

    tab_spec = pl.BlockSpec((nc * CHUNK, HD), lambda i, j: (j, 0))
    args = (x, cos, sin, norm_g.reshape(1, D_MODEL), w_in, w_out, ret_g.reshape(1, GRP),
            hg_g.reshape(1, GRP), lb_logits, s_ret, s_hg)
    in_specs = [x_spec, tab_spec, tab_spec] + [full(a) for a in args[3:9]] + [st_spec, st_spec]
    grp_bf16 = pltpu.VMEM((tt, GRP), BF16)
    grp_f32 = pltpu.VMEM((tt, GRP), F32)
    return pl.pallas_call(
        functools.partial(_even_kernel, ns=ns, nc=nc, layer_j=layer_j),
        grid=grid,
        in_specs=in_specs,
        out_specs=[x_spec, st_spec, st_spec],
        out_shape=[jax.ShapeDtypeStruct(x.shape, F32),
                   jax.ShapeDtypeStruct(s_ret.shape, F32),
                   jax.ShapeDtypeStruct(s_hg.shape, F32)],
        scratch_shapes=[grp_bf16] * 6 + [grp_f32] * 4
        + [pltpu.VMEM((tt, D_MODEL), BF16), pltpu.VMEM((ns, HEADS, HD, HD), F32)],
        compiler_params=pltpu.CompilerParams(
            dimension_semantics=("parallel", "arbitrary"), vmem_limit_bytes=VMEM_LIMIT),
        name="even_mixer",
    )(*args)


def _softplus(x):
    return jnp.maximum(x, 0.0) + jnp.log1p(jnp.exp(-jnp.abs(x)))


def _gelu_tanh(x):
    return 0.5 * x * (1.0 + jnp.tanh(math.sqrt(2.0 / math.pi) * (x + 0.044715 * (x * x * x))))


def _odd_kernel(x_ref, ng_ref, win_ref, wt_ref, wout_ref, cw_ref, lcb_ref, wa_ref, ba_ref,
                wx_ref, bx_ref, lam_ref, dnp_ref, dg_ref, lh_ref, lc_ref, ds_ref, dc_ref,
                xo_ref, lh_o, lc_o, ds_o, dc_o,
                z_ref, gl_ref, sg_ref, bt_ref, gt_ref, gtt_ref, mix_ref, xp_ref, *, ns, nc):
    t = pl.program_id(1)
    tt = ns * nc * CHUNK
    conv_ch = LRU_WIDTH + 3 * GRP
    hist = CONV_W - 1
    top = 8
    heads = range(HEADS)

    @pl.when(t == 0)
    def _():
        lh_o[...] = lh_ref[...]
        lc_o[...] = lc_ref[...]
        ds_o[...] = ds_ref[...]
        dc_o[...] = dc_ref[...]

    x = x_ref[...].reshape(tt, D_MODEL)
    hn = _rms(x, ng_ref[...]).astype(BF16)

    def proj(lo):
        return _dot_nt(hn, win_ref[lo:lo + GRP, :])

    zs = _dot_nt(hn, wt_ref[...])
    z5 = proj(5 * GRP)
    bt_ref[...] = jax.nn.sigmoid(zs)
    g_all = _chunk_cumsum(-jnp.exp(dnp_ref[1:2, :]) * _softplus(zs + dnp_ref[0:1, :]))
    gt_ref[...] = g_all
    for c in range(ns * nc):
        gtt_ref[c] = g_all[c * CHUNK:(c + 1) * CHUNK, :].T
    z1 = proj(GRP)
    sg_ref[...] = _silu(z5)
    z_ref[:, 0:GRP] = proj(0)
    gl_ref[...] = _gelu_tanh(z1)
    for g in range(3):
        z_ref[:, (g + 1) * GRP:(g + 2) * GRP] = proj((g + 2) * GRP)

    sp_lam = _softplus(-lam_ref[...])
    ri = lax.broadcasted_iota(jnp.int32, (CHUNK, CHUNK), 0)
    ci = lax.broadcasted_iota(jnp.int32, (CHUNK, CHUNK), 1)
    rows_w = lax.broadcasted_iota(jnp.int32, (CHUNK, LRU_WIDTH), 0)
    eye_c = (ri == ci).astype(F32)
    sub_bits = DN_SUB.bit_length() - 1
    same_blk = (ri >> sub_bits) == (ci >> sub_bits)
    merge_masks = []
    for lvl in range(sub_bits, CHUNK.bit_length() - 1):
        merge_masks.append(((ri >> (lvl + 1)) == (ci >> (lvl + 1)))
                           & (((ri >> lvl) & 1) == 1) & (((ci >> lvl) & 1) == 0))

    def chunk_body(i, carry):
        lane_seq, lane_rows, lane_chunk, ys = [], [], [], []
        for u in range(lanes):
            s, c = (i * lanes + u, 0) if nc == 1 else (u, i)
            r0 = (s * nc + c) * CHUNK
            rows = pl.ds(r0, CHUNK)
            lane_seq.append(s)
            lane_rows.append(rows)
            lane_chunk.append(s * nc + c)

            xp_ref[u, top - hist:top, 0:LRU_WIDTH] = lc_o[s]
            xp_ref[u, top - hist:top, LRU_WIDTH:conv_ch] = dc_o[s]
            xp_ref[u, top:top + CHUNK, :] = z_ref[rows, :]
            lc_o[s] = xp_ref[u, top + CHUNK - hist:top + CHUNK, 0:LRU_WIDTH]
            dc_o[s] = xp_ref[u, top + CHUNK - hist:top + CHUNK, LRU_WIDTH:conv_ch]
            y = xp_ref[u, top - hist:top - hist + CHUNK, :] * cw_ref[0:1, :]
            for jj in range(1, CONV_W):
                y = y + (xp_ref[u, top - hist + jj:top - hist + jj + CHUNK, :]
                         * cw_ref[jj:jj + 1, :])
            ys.append(y)

        def rg_lru_branch(lane):
            y, s, rows = ys[lane], lane_seq[lane], lane_rows[lane]
            lx = y[:, 0:LRU_WIDTH] + lcb_ref[...]
            xb = lx.astype(BF16)
            r = jax.nn.sigmoid(_dot(xb, wa_ref[...]) + ba_ref[...])
            ig = jax.nn.sigmoid(_dot(xb, wx_ref[...]) + bx_ref[...])
            a = jnp.exp(-LRU_C * r * sp_lam)
            u = jnp.sqrt(1.0 - a * a) * (ig * lx)
            k = 1
            while k < CHUNK:
                m = rows_w >= k
                u = jnp.where(m, a * pltpu.roll(u, k, 0) + u, u)
                a = jnp.where(m, a * pltpu.roll(a, k, 0), a)
                k *= 2
            hseq = a * lh_o[s] + u
            lh_o[s] = hseq[CHUNK - 1:CHUNK]
            mix_ref[rows, 0:LRU_WIDTH] = (gl_ref[rows, :] * hseq).astype(BF16)

        streams = [(u, h) for u in range(lanes) for h in heads]
        ns_ = range(len(streams))
        beta_t = [bt_ref[rows, :] for rows in lane_rows]
        g_t = [gt_ref[rows, :] for rows in lane_rows]
        g_tt = [gtt_ref[c] for c in lane_chunk]
        states = [ds_o[lane_seq[u], h] for u, h in streams]
        kn, both_k, both_s = [], [], []
        for n, (u, h) in enumerate(streams):
            y = ys[u]
            base = LRU_WIDTH + h * HD
            q = _silu(y[:, base:base + HD])
            kk_ = _silu(y[:, base + GRP:base + GRP + HD])
            q = q * lax.rsqrt(jnp.sum(q * q, axis=-1, keepdims=True) + EPS) * (HD ** -0.5)
            kn.append(kk_ * lax.rsqrt(jnp.sum(kk_ * kk_, axis=-1, keepdims=True) + EPS))
            knb = kn[n].astype(BF16)
            qk_in = jnp.concatenate([q.astype(BF16), knb], axis=0)
            both_k.append(_dot_nt(qk_in, knb))
            both_s.append(_dot(qk_in, states[n].astype(BF16)))
        gcol, eg, e_incl, amat, rhs = [], [], [], [], []
        for n, (u, h) in enumerate(streams):
            y = ys[u]
            v = _silu(y[:, LRU_WIDTH + 2 * GRP + h * HD:LRU_WIDTH + 2 * GRP + (h + 1) * HD])
            gcol.append(g_t[u][:, HEADS + h:HEADS + h + 1])
            rel = gcol[n] - g_tt[u][HEADS + h:HEADS + h + 1, :]
            e_strict = jnp.exp(jnp.where(ri > ci, rel, -jnp.inf))
            e_incl.append(jnp.where(ri == ci, 1.0, e_strict))
            beta = beta_t[u][:, h:h + 1]
            eg.append(jnp.exp(gcol[n]))
            amat.append(beta * both_k[n][CHUNK:] * e_strict)
            rhs.append(beta * (v - eg[n] * both_s[n][CHUNK:]))
        dblk = [jnp.where(same_blk, amat[n], 0.0) for n in ns_]
        tinv = [eye_c for _ in ns_]
        for jj in range(DN_SUB - 1):
            for n in ns_:
                colv = jnp.concatenate(
                    [dblk[n][b0:b0 + DN_SUB, b0 + jj:b0 + jj + 1]
                     for b0 in range(0, CHUNK, DN_SUB)], axis=0)
                rowm = jnp.concatenate(
                    [jnp.broadcast_to(tinv[n][b0 + jj:b0 + jj + 1, :], (DN_SUB, CHUNK))
                     for b0 in range(0, CHUNK, DN_SUB)], axis=0)
                tinv[n] = tinv[n] - colv * rowm
        for lane in range(lanes):
            rg_lru_branch(lane)
        tb = [tinv[n].astype(BF16) for n in ns_]
        pend = [[_dot(tb[n], jnp.where(lm, amat[n], 0.0).astype(BF16)) for lm in merge_masks]
                for n in ns_]
        for _ in merge_masks:
            xb_ = [pend[n][0].astype(BF16) for n in ns_]
            tb = [tinv[n].astype(BF16) for n in ns_]
            tinv = [tinv[n] - _dot(xb_[n], tb[n]) for n in ns_]
            pend = [[p - _dot(xb_[n], p.astype(BF16)) for p in pend[n][1:]] for n in ns_]
        w = [_dot(tinv[n].astype(BF16), rhs[n].astype(BF16)).astype(BF16) for n in ns_]
        for n, (u, h) in enumerate(streams):
            s, rows = lane_seq[u], lane_rows[u]
            o = eg[n] * both_s[n][:CHUNK] + _dot((both_k[n][:CHUNK] * e_incl[n]).astype(BF16), w[n])
            gend = gcol[n][CHUNK - 1:CHUNK, :]
            kd = (kn[n] * jnp.exp(gend - gcol[n])).astype(BF16)
            ds_o[s, h] = states[n] * jnp.exp(gend) + _dot_tn(kd, w[n])
            o = _head_norm(o, dg_ref[:, pl.ds(h * HD, HD)]) * sg_ref[rows, pl.ds(h * HD, HD)]
            mix_ref[rows, pl.ds(LRU_WIDTH + h * HD, HD)] = o.astype(BF16)
        return carry

    lanes = ODD_LANES if ns % ODD_LANES == 0 else 1
    for i in range(ns * nc // lanes):
        chunk_body(i, 0)

    xo_ref[...] = (x + _dot(mix_ref[...], wout_ref[...])).reshape(xo_ref.shape)


def _odd_mixer(x, norm_g, w_in, w_tail, w_out, cw, lcb, wa, ba, wx, bx, lam, dnp, dn_g,
               s_lh, s_lc, s_dn, s_dc):
    b, t, _ = x.shape
    ns, nc = _seq_tiling(b, t, ODD_LANES)
    tt = ns * nc * CHUNK
    grid = (b // ns, t // (nc * CHUNK))
    x_spec = pl.BlockSpec((ns, nc * CHUNK, D_MODEL), lambda i, j: (i, j, 0))

    def full(a):
        return _resident(a.shape, lambda i, j: (0,) * a.ndim)

    def st_spec(a):
        return pl.BlockSpec((ns,) + a.shape[1:], lambda i, j: (i,) + (0,) * (a.ndim - 1))

    params = (norm_g.reshape(1, D_MODEL), w_in, w_tail, w_out, cw, lcb, wa, ba, wx, bx, lam, dnp,
              dn_g)
    states = (s_lh, s_lc, s_dn, s_dc)
    grp_f32 = pltpu.VMEM((tt, GRP), F32)
    head_f32 = pltpu.VMEM((tt, HD), F32)
    return pl.pallas_call(
        functools.partial(_odd_kernel, ns=ns, nc=nc),
        grid=grid,
        in_specs=[x_spec] + [full(a) for a in params] + [st_spec(a) for a in states],
        out_specs=[x_spec] + [st_spec(a) for a in states],
        out_shape=[jax.ShapeDtypeStruct(x.shape, F32)]
        + [jax.ShapeDtypeStruct(a.shape, F32) for a in states],
        scratch_shapes=[pltpu.VMEM((tt, LRU_WIDTH + 3 * GRP), F32), grp_f32, grp_f32,
                        head_f32, head_f32, pltpu.VMEM((ns * nc, HD, CHUNK), F32),
                        pltpu.VMEM((tt, D_MODEL), BF16),
                        pltpu.VMEM((ODD_LANES, 8 + CHUNK, LRU_WIDTH + 3 * GRP), F32)],
        compiler_params=pltpu.CompilerParams(
            dimension_semantics=("parallel", "arbitrary"), vmem_limit_bytes=VMEM_LIMIT),
        name="odd_mixer",
    )(x, *params, *states)


def _rope_tables(pos0, n):
    half = HD // 2
    freq = ROPE_BASE ** (-jnp.arange(half, dtype=F32) / half)
    ang_a = (pos0 + CHUNK * jnp.arange(n // CHUNK)).astype(F32)[:, None, None] * freq
    ang_b = jnp.arange(CHUNK, dtype=F32)[None, :, None] * freq
    cos_a, sin_a, cos_b, sin_b = jnp.cos(ang_a), jnp.sin(ang_a), jnp.cos(ang_b), jnp.sin(ang_b)
    cos = (cos_a * cos_b - sin_a * sin_b).reshape(n, half)
    sin = (sin_a * cos_b + cos_a * sin_b).reshape(n, half)
    return jnp.concatenate([cos, cos], axis=-1), jnp.concatenate([-sin, sin], axis=-1)


def _block_diag(w):
    n, bs, _ = w.shape
    eye = jnp.eye(n, dtype=w.dtype)
    return (eye[:, None, :, None] * w[:, :, None, :]).reshape(n * bs, n * bs)


def kernel(x_prompt, x_sample, state_ret, state_hgrn, state_lru_h, state_lru_conv, state_dn,
           state_dn_conv, ffn1_norm, ffn1_w_in, ffn1_w_out, mix_norm, ffn2_norm, ffn2_w_in,
           ffn2_w_out, final_norm, even_w_in, even_w_out, ret_out_norm, hg_out_norm,
           hg_lb_logits, odd_w_in, odd_w_out, lru_conv_w, lru_conv_b, lru_w_a, lru_b_a, lru_w_x,
           lru_b_x, lru_lambda, dn_conv_w, dn_a_log, dn_dt_bias, dn_out_norm):
    depth = ffn1_norm.shape[0]
    bp, tp, _ = x_prompt.shape
    bs, ts, _ = x_sample.shape
    tabs = (_rope_tables(0, tp), _rope_tables(PAST_LEN, ts))
    xs = [x_prompt, x_sample]
    nb = (bp, bs)
    outs = {k: ([], []) for k in ("ret", "hg", "lh", "lc", "dn", "dc")}

    ffn_sets = []
    for l in range(depth):
        ffn_sets += [(ffn1_norm[l], ffn1_w_in, ffn1_w_out, l),
                     (ffn2_norm[l], ffn2_w_in, ffn2_w_out, l)]
    ffn_w = [(ffn1_w_in[0].astype(BF16), ffn1_w_out[0].astype(BF16))]

    def run_ffn(apply_final, mixer_casts=()):
        k = len(ffn_w) - 1
        casts = []
        if k + 1 < len(ffn_sets):
            _, nwi, nwo, nl = ffn_sets[k + 1]
            casts = [(nwi, nl, FFN_CAST_ROWS, nwi.shape[2], nwi.shape[1]),
                     (nwo, nl, FFN_CAST_ROWS, nwo.shape[2], nwo.shape[1])]
        casts += list(mixer_casts)
        xs[0], xs[1], cast = _ffn(xs[0], xs[1], ffn_sets[k][0], ffn_w[k][0], ffn_w[k][1],
                                  final_norm, apply_final, casts)
        ffn_w.append(tuple(cast[:2]) if k + 1 < len(ffn_sets) else ())
        return cast[len(cast) - len(mixer_casts):]

    for l in range(depth):
        j = l // 2
        last = l == depth - 1
        if l % 2 == 0:
            w_in, w_out = run_ffn(False, [(even_w_in, j, None, EVEN_IN, D_MODEL),
                                          (even_w_out, j, None, D_MODEL, 2 * GRP)])
            for g in range(2):
                if g == 0:
                    s_ret = jnp.zeros((bp, HEADS, HD, HD), F32)
                    s_hg = jnp.zeros((bp, HEADS, HD, HD), F32)
                else:
                    s_ret, s_hg = state_ret[j], state_hgrn[j]
                xs[g], n_ret, n_hg = _even_mixer(
                    xs[g], tabs[g][0], tabs[g][1], mix_norm[l], w_in, w_out, ret_out_norm[j],
                    hg_out_norm[j], hg_lb_logits, s_ret, s_hg, j)
                outs["ret"][g].append(n_ret)
                outs["hg"][g].append(n_hg)
        else:
            w_in_t = jnp.swapaxes(odd_w_in, 1, 2)
            w_in, w_out = run_ffn(False, [(w_in_t, j, None, D_MODEL, 6 * GRP),
                                          (odd_w_out, j, None, D_MODEL, 2 * GRP)])
            n_tail = odd_w_in.shape[-1] - 6 * GRP
            tail = lax.optimization_barrier(w_in_t[j][6 * GRP:, :])
            w_tail = jnp.pad(tail, ((0, HD - n_tail), (0, 0))).astype(BF16)
            cw = jnp.concatenate([lru_conv_w[j], dn_conv_w[j]], axis=-1)
            wa = _block_diag(lru_w_a[j]).astype(BF16)
            wx = _block_diag(lru_w_x[j]).astype(BF16)
            dnp = jnp.zeros((2, HD), F32)
            dnp = dnp.at[0, HEADS:2 * HEADS].set(dn_dt_bias[j])
            dnp = dnp.at[1, HEADS:2 * HEADS].set(dn_a_log[j])
            for g in range(2):
                if g == 0:
                    s_lh = jnp.zeros((bp, 1, LRU_WIDTH), F32)
                    s_lc = jnp.zeros((bp, CONV_W - 1, LRU_WIDTH), F32)
                    s_dn = jnp.zeros((bp, HEADS, HD, HD), F32)
                    s_dc = jnp.zeros((bp, CONV_W - 1, 3 * GRP), F32)
                else:
                    s_lh = state_lru_h[j].reshape(bs, 1, LRU_WIDTH)
                    s_lc, s_dn, s_dc = state_lru_conv[j], state_dn[j], state_dn_conv[j]
                xs[g], n_lh, n_lc, n_dn, n_dc = _odd_mixer(
                    xs[g], mix_norm[l], w_in, w_tail, w_out, cw, lru_conv_b[j].reshape(1, -1), wa,
                    lru_b_a[j].reshape(1, -1), wx, lru_b_x[j].reshape(1, -1),
                    lru_lambda[j].reshape(1, -1), dnp, dn_out_norm[j].reshape(1, -1),
                    s_lh, s_lc, s_dn, s_dc)
                outs["lh"][g].append(n_lh.reshape(nb[g], LRU_WIDTH))
                outs["lc"][g].append(n_lc)
                outs["dn"][g].append(n_dn)
                outs["dc"][g].append(n_dc)
        run_ffn(last)

    res = [xs[0], xs[1]]
    for k in ("ret", "hg", "lh", "lc", "dn", "dc"):
        for g in range(2):
            res.append(jnp.stack(outs[k][g]))
    return tuple(res)
```

```python
import functools
import math

import jax
import jax.numpy as jnp
from jax import lax
from jax.experimental import pallas as pl
from jax.experimental.pallas import tpu as pltpu

F32 = jnp.float32
BF16 = jnp.bfloat16

D_MODEL = 1024
FF_DIM = 2816
EPS = 1e-6
CHUNK = 64
CONV_W = 4
HEADS = 4
HD = 128
ROPE_BASE = 10000.0
PAST_LEN = 2048
LRU_WIDTH = 512
LRU_C = 8.0
GRP = HEADS * HD
EVEN_IN = 8 * GRP
SUB = 8
DN_SUB = 4

VMEM_LIMIT = 56 * 1024 * 1024

FFN_TM = 512
FFN_TF = 256
FFN_ROW_SPLIT = 2
FFN_CAST_ROWS = 16
SEQ_TILE = 512
MIXER_LANES = 1
ODD_LANES = 2
ODD_SEQ_TILE = 256

RET_LOG_G = [math.log1p(-(2.0 ** (-5.0 - h))) for h in range(HEADS)]


def _rms(x, g):
    return x * lax.rsqrt(jnp.mean(x * x, axis=-1, keepdims=True) + EPS) * g


def _silu(x):
    return x * jax.nn.sigmoid(x)


def _dot(a, b):
    return jnp.dot(a, b, preferred_element_type=F32)


def _dot_nt(a, b):
    return lax.dot_general(a, b, (((1,), (1,)), ((), ())), preferred_element_type=F32)


def _dot_tn(a, b):
    return lax.dot_general(a, b, (((0,), (0,)), ((), ())), preferred_element_type=F32)


def _head_norm(o, g):
    return o * lax.rsqrt(jnp.mean(o * o, axis=-1, keepdims=True) + EPS) * g


def _hcols(h):
    return slice(h * HD, (h + 1) * HD)


def _chunk_cumsum(x):
    pos = lax.broadcasted_iota(jnp.int32, x.shape, 0) & (CHUNK - 1)
    k = 1
    while k < CHUNK:
        x = jnp.where(pos >= k, x + pltpu.roll(x, k, 0), x)
        k *= 2
    return x


def _resident(block_shape, index_map):
    return pl.BlockSpec(block_shape, index_map, pipeline_mode=pl.Buffered(1))


def _ffn_kernel(*refs, n_first, apply_final, n_casts):
    xa_ref, xb_ref, g_ref, win_ref, wout_ref, fin_ref = refs[:6]
    cast_in = refs[6:6 + n_casts]
    oa_ref, ob_ref = refs[6 + n_casts:8 + n_casts]
    cast_out = refs[8 + n_casts:8 + 2 * n_casts]
    act_ref = refs[8 + 2 * n_casts]
    for src, dst in zip(cast_in, cast_out):
        dst[...] = src[...].astype(BF16)
    i = pl.program_id(0)

    def tile(x_ref, o_ref):
        tm = x_ref.shape[0]
        halves = [slice(r, r + tm // FFN_ROW_SPLIT) for r in range(0, tm, tm // FFN_ROW_SPLIT)]
        hs = [_rms(x_ref[rows, :], g_ref[...]).astype(BF16) for rows in halves]
        for c in range(FF_DIM // FFN_TF):
            lo = c * FFN_TF
            for rows, h in zip(halves, hs):
                gate = _dot(h, win_ref[:, lo:lo + FFN_TF])
                up = _dot(h, win_ref[:, FF_DIM + lo:FF_DIM + lo + FFN_TF])
                act_ref[rows, lo:lo + FFN_TF] = (_silu(gate) * up).astype(BF16)
        for rows in halves:
            y = x_ref[rows, :] + 0.5 * _dot(act_ref[rows, :], wout_ref[...])
            if apply_final:
                y = _rms(y, fin_ref[...])
            o_ref[rows, :] = y

    @pl.when(i < n_first)
    def _():
        tile(xa_ref, oa_ref)

    @pl.when(i >= n_first)
    def _():
        tile(xb_ref, ob_ref)


def _ffn(xa, xb, norm_g, w_in, w_out, final_g, apply_final, casts=()):
    sa, sb = xa.shape, xb.shape
    xa2, xb2 = xa.reshape(-1, D_MODEL), xb.reshape(-1, D_MODEL)
    tm = min(FFN_TM, xa2.shape[0], xb2.shape[0])
    na, nb = xa2.shape[0] // tm, xb2.shape[0] // tm
    x_blk = (tm, D_MODEL)
    in_specs = [
        pl.BlockSpec(x_blk, lambda i: (jnp.minimum(i, na - 1), 0)),
        pl.BlockSpec(x_blk, lambda i: (jnp.maximum(i - na, 0), 0)),
        _resident((1, D_MODEL), lambda i: (0, 0)),
        _resident((D_MODEL, 2 * FF_DIM), lambda i: (0, 0)),
        _resident((FF_DIM, D_MODEL), lambda i: (0, 0)),
        _resident((1, D_MODEL), lambda i: (0, 0)),
    ]
    out_specs = [
        pl.BlockSpec(x_blk, lambda i: (jnp.minimum(i, na - 1), 0)),
        pl.BlockSpec(x_blk, lambda i: (jnp.maximum(i - na, 0), 0)),
    ]
    out_shape = [jax.ShapeDtypeStruct(xa2.shape, F32), jax.ShapeDtypeStruct(xb2.shape, F32)]
    args = [xa2, xb2, norm_g.reshape(1, D_MODEL), w_in, w_out, final_g.reshape(1, D_MODEL)]
    cast_in_specs, cast_out_specs = [], []
    for arr, layer, row_blocks, c, r in casts:
        ncol = na // (row_blocks or na)
        nrow = na // ncol
        rows = r // nrow
        assert layer == 0 or arr.shape[1] % rows == 0

        def blk(i, ncol=ncol):
            i = jnp.minimum(i, na - 1)
            return i // ncol, i % ncol

        def blk_in(i, blk=blk, first=layer * (arr.shape[1] // rows)):
            rb, cb = blk(i)
            return first + rb, cb

        cast_in_specs.append(pl.BlockSpec((rows, c // ncol), blk_in))
        cast_out_specs.append(pl.BlockSpec((rows, c // ncol), blk))
        out_shape.append(jax.ShapeDtypeStruct((r, c), BF16))
        args.append(arr.reshape(-1, arr.shape[-1]))
    outs = pl.pallas_call(
        functools.partial(_ffn_kernel, n_first=na, apply_final=apply_final,
                          n_casts=len(casts)),
        grid=(na + nb,),
        in_specs=in_specs + cast_in_specs,
        out_specs=out_specs + cast_out_specs,
        out_shape=out_shape,
        scratch_shapes=[pltpu.VMEM((tm, FF_DIM), BF16)],
        compiler_params=pltpu.CompilerParams(
            dimension_semantics=("arbitrary",), vmem_limit_bytes=VMEM_LIMIT),
        name="ffn",
    )(*args)
    return outs[0].reshape(sa), outs[1].reshape(sb), list(outs[2:])


def _seq_tiling(b, t, lanes=1, tile=SEQ_TILE):
    if t >= tile:
        lanes = lanes if b % lanes == 0 else 1
        return lanes, tile // CHUNK // lanes
    return min(b, tile // t), t // CHUNK


def _even_kernel(x_ref, cos_ref, sin_ref, ng_ref, win_ref, wout_ref, rg_ref, hg_ref, lbl_ref,
                 sret_ref, shg_ref, xo_ref, sret_o, shg_o,
                 qs_ref, qd_ref, kb_ref, kd_ref, rv_ref, hv_ref,
                 rgate_ref, hq_ref, hf_ref, hgate_ref, mix_ref, hgt_ref, *, ns, nc, layer_j):
    t = pl.program_id(1)
    tt = ns * nc * CHUNK
    heads = range(HEADS)

    @pl.when(t == 0)
    def _():
        sret_o[...] = sret_ref[...]
        for s in range(ns):
            for h in heads:
                hgt_ref[s, h] = shg_ref[s, h].T

    x = x_ref[...].reshape(tt, D_MODEL)
    hn = _rms(x, ng_ref[...]).astype(BF16)

    def proj(g):
        return _dot(hn, win_ref[:, g * GRP:(g + 1) * GRP])

    cos = jnp.concatenate([cos_ref[...]] * ns, axis=0)
    sin = jnp.concatenate([sin_ref[...]] * ns, axis=0)
    scale = HD ** -0.5
    pos = lax.broadcasted_iota(jnp.int32, (CHUNK, HD), 0).astype(F32)

    def tiled(v):
        return jnp.concatenate([v] * (tt // CHUNK), axis=0)

    qdec = [tiled(jnp.exp(RET_LOG_G[h] * (pos + 1.0)) * scale) for h in heads]
    kdec = [tiled(jnp.exp(RET_LOG_G[h] * (CHUNK - 1.0 - pos))) for h in heads]

    def rope(z, h):
        v = z[:, _hcols(h)]
        return v * cos + pltpu.roll(v, HD // 2, 1) * sin

    z0 = proj(0)
    z1 = proj(1)
    for h in heads:
        q = rope(z0, h)
        qs_ref[:, _hcols(h)] = (q * scale).astype(BF16)
        qd_ref[:, _hcols(h)] = (q * qdec[h]).astype(BF16)
    z2 = proj(2)
    for h in heads:
        k = rope(z1, h)
        kb_ref[:, _hcols(h)] = k.astype(BF16)
        kd_ref[:, _hcols(h)] = (k * kdec[h]).astype(BF16)
    z3 = proj(3)
    rv_ref[...] = z2.astype(BF16)
    z4 = proj(4)
    rgate_ref[...] = _silu(z3)
    z5 = proj(5)
    hq_ref[...] = _silu(z4)
    z6 = proj(6)
    hf_ref[...] = z5
    z7 = proj(7)
    hv_ref[...] = z6.astype(BF16)
    hgate_ref[...] = _silu(z7)

    ri = lax.broadcasted_iota(jnp.int32, (CHUNK, CHUNK), 0)
    ci = lax.broadcasted_iota(jnp.int32, (CHUNK, CHUNK), 1)
    absd = jnp.abs(ri - ci).astype(F32)
    intra = [jnp.exp(RET_LOG_G[h] * absd) for h in heads]
    sdec = [math.exp(RET_LOG_G[h] * CHUNK) for h in heads]

    lbl = lbl_ref[...]
    e = jnp.exp(lbl - jnp.max(lbl, axis=0, keepdims=True))
    sm = e / jnp.sum(e, axis=0, keepdims=True)
    lb_all = jnp.sum(sm[:layer_j + 1], axis=0, keepdims=True)

    nb = CHUNK // SUB
    trow_io = lax.broadcasted_iota(jnp.int32, (nb, SUB, CHUNK), 1)
    lane_io = lax.broadcasted_iota(jnp.int32, (nb, SUB, CHUNK), 2)
    blk_io = lax.broadcasted_iota(jnp.int32, (nb, SUB, CHUNK), 0)

    def chunk_body(i, carry):
        streams = []
        for u in range(lanes):
            s, c = (i * lanes + u, 0) if nc == 1 else (u, i)
            r0 = (s * nc + c) * CHUNK
            streams += [(s, r0, h) for h in heads]

        def ld(ref, r0, h):
            return ref[pl.ds(r0, CHUNK), pl.ds(h * HD, HD)]

        ret_states = [sret_o[s, h] for s, _, h in streams]
        hg_states = [hgt_ref[s, h] for s, _, h in streams]

        r_v, r_att, r_inter = [], [], []
        for n, (s, r0, h) in enumerate(streams):
            st = ret_states[n]
            r_v.append(ld(rv_ref, r0, h))
            r_att.append(_dot_nt(ld(qs_ref, r0, h), ld(kb_ref, r0, h)))
            r_inter.append(_dot(ld(qd_ref, r0, h), st.astype(BF16)))
            sret_o[s, h] = st * sdec[h] + _dot_tn(ld(kd_ref, r0, h), r_v[n])

        h_q, h_k, h_b, h_v, h_off, h_inter = [], [], [], [], [], []
        for n, (s, r0, h) in enumerate(streams):
            lb = lb_all[:, _hcols(h)]
            sig = jax.nn.sigmoid(ld(hf_ref, r0, h))
            v = ld(hv_ref, r0, h)
            q = ld(hq_ref, r0, h)
            k = (1.0 - lb) * (1.0 - sig)
            b = _chunk_cumsum(jnp.log(lb + (1.0 - lb) * sig))
            stt = hg_states[n]
            off = []
            for blk in range(1, nb):
                lo = blk * SUB
                rb = b[lo - 1:lo, :]
                qt = (q[lo:lo + SUB] * jnp.exp(b[lo:lo + SUB] - rb)).astype(BF16)
                kt = jnp.concatenate([k[:lo] * jnp.exp(rb - b[:lo]),
                                      jnp.zeros((CHUNK - lo, HD), F32)], axis=0).astype(BF16)
                off.append(_dot_nt(qt, kt))
            h_off.append(off)
            h_inter.append(_dot_nt((q * jnp.exp(b)).astype(BF16), stt.astype(BF16)))
            bend = b[CHUNK - 1:CHUNK, :]
            kd = (k * jnp.exp(bend - b)).astype(BF16)
            hgt_ref[s, h] = stt * jnp.exp(bend) + _dot_tn(v, kd)
            h_q.append(q)
            h_k.append(k)
            h_b.append(b)
            h_v.append(v)

        for n, (s, r0, h) in enumerate(streams):
            o = _dot((r_att[n] * intra[h]).astype(BF16), r_v[n]) + r_inter[n]
            o = _head_norm(o, rg_ref[:, pl.ds(h * HD, HD)]) * ld(rgate_ref, r0, h)
            mix_ref[pl.ds(r0, CHUNK), pl.ds(h * HD, HD)] = o.astype(BF16)

        h_att = []
        for n in range(len(streams)):
            b4 = h_b[n].reshape(nb, SUB, HD)
            q4 = h_q[n].reshape(nb, SUB, HD)
            k4 = h_k[n].reshape(nb, SUB, HD)
            diag = jnp.zeros((nb, SUB, CHUNK), F32)
            for sp in range(SUB):
                dec = jnp.exp(b4 - b4[:, sp:sp + 1, :])
                r = jnp.sum(dec * q4 * k4[:, sp:sp + 1, :], axis=-1, keepdims=True)
                diag = jnp.where(lane_io == blk_io * SUB + sp, r, diag)
            diag = jnp.where(lane_io - blk_io * SUB <= trow_io, diag, 0.0).reshape(CHUNK, CHUNK)
            parts = [diag[0:SUB]]
            for blk in range(1, nb):
                parts.append(diag[blk * SUB:(blk + 1) * SUB] + h_off[n][blk - 1])
            h_att.append(jnp.concatenate(parts, axis=0).astype(BF16))
        for n, (s, r0, h) in enumerate(streams):
            o = _dot(h_att[n], h_v[n]) + h_inter[n]
            o = _head_norm(o, hg_ref[:, pl.ds(h * HD, HD)]) * ld(hgate_ref, r0, h)
            mix_ref[pl.ds(r0, CHUNK), pl.ds(GRP + h * HD, HD)] = o.astype(BF16)
        return carry

    lanes = MIXER_LANES if ns % MIXER_LANES == 0 else 1
    for i in range(ns * nc // lanes):
        chunk_body(i, 0)

    xo_ref[...] = (x + _dot(mix_ref[...], wout_ref[...])).reshape(xo_ref.shape)

    @pl.when(t == pl.num_programs(1) - 1)
    def _():
        for s in range(ns):
            for h in heads:
                shg_o[s, h] = hgt_ref[s, h].T


def _even_mixer(x, cos, sin, norm_g, w_in, w_out, ret_g, hg_g, lb_logits, s_ret, s_hg, layer_j):
    b, t, _ = x.shape
    ns, nc = _seq_tiling(b, t, MIXER_LANES)
    tt = ns * nc * CHUNK
    grid = (b // ns, t // (nc * CHUNK))
    st_spec = pl.BlockSpec((ns, HEADS, HD, HD), lambda i, j: (i, 0, 0, 0))
    x_spec = pl.BlockSpec((ns, nc * CHUNK, D_MODEL), lambda i, j: (i, j, 0))

    def full(a):
        return _resident(a.shape, lambda i, j: (0,) * a.ndim)

    tab_spec = pl.BlockSpec((nc * CHUNK, HD), lambda i, j: (j, 0))
    args = (x, cos, sin, norm_g.reshape(1, D_MODEL), w_in, w_out, ret_g.reshape(1, GRP),
            hg_g.reshape(1, GRP), lb_logits, s_ret, s_hg)
    in_specs = [x_spec, tab_spec, tab_spec] + [full(a) for a in args[3:9]] + [st_spec, st_spec]
    grp_bf16 = pltpu.VMEM((tt, GRP), BF16)
    grp_f32 = pltpu.VMEM((tt, GRP), F32)
    return pl.pallas_call(
        functools.partial(_even_kernel, ns=ns, nc=nc, layer_j=layer_j),
        grid=grid,
        in_specs=in_specs,
        out_specs=[x_spec, st_spec, st_spec],
        out_shape=[jax.ShapeDtypeStruct(x.shape, F32),
                   jax.ShapeDtypeStruct(s_ret.shape, F32),
                   jax.ShapeDtypeStruct(s_hg.shape, F32)],
        scratch_shapes=[grp_bf16] * 6 + [grp_f32] * 4
        + [pltpu.VMEM((tt, D_MODEL), BF16), pltpu.VMEM((ns, HEADS, HD, HD), F32)],
        compiler_params=pltpu.CompilerParams(
            dimension_semantics=("parallel", "arbitrary"), vmem_limit_bytes=VMEM_LIMIT),
        name="even_mixer",
    )(*args)


def _softplus(x):
    return jnp.maximum(x, 0.0) + jnp.log1p(jnp.exp(-jnp.abs(x)))


def _gelu_tanh(x):
    return 0.5 * x * (1.0 + jnp.tanh(math.sqrt(2.0 / math.pi) * (x + 0.044715 * (x * x * x))))


def _odd_kernel(x_ref, ng_ref, win_ref, wt_ref, wout_ref, cw_ref, lcb_ref, wa_ref, ba_ref,
                wx_ref, bx_ref, lam_ref, dnp_ref, dg_ref, lh_ref, lc_ref, ds_ref, dc_ref,
                xo_ref, lh_o, lc_o, ds_o, dc_o,
                z_ref, gl_ref, sg_ref, bt_ref, gt_ref, gtt_ref, mix_ref, xp_ref, *, ns, nc):
    t = pl.program_id(1)
    tt = ns * nc * CHUNK
    conv_ch = LRU_WIDTH + 3 * GRP
    hist = CONV_W - 1
    top = 8
    heads = range(HEADS)

    @pl.when(t == 0)
    def _():
        lh_o[...] = lh_ref[...]
        lc_o[...] = lc_ref[...]
        ds_o[...] = ds_ref[...]
        dc_o[...] = dc_ref[...]

    x = x_ref[...].reshape(tt, D_MODEL)
    hn = _rms(x, ng_ref[...]).astype(BF16)

    def proj(lo):
        return _dot_nt(hn, win_ref[lo:lo + GRP, :])

    zs = _dot_nt(hn, wt_ref[...])
    z5 = proj(5 * GRP)
    bt_ref[...] = jax.nn.sigmoid(zs)
    g_all = _chunk_cumsum(-jnp.exp(dnp_ref[1:2, :]) * _softplus(zs + dnp_ref[0:1, :]))
    gt_ref[...] = g_all
    for c in range(ns * nc):
        gtt_ref[c] = g_all[c * CHUNK:(c + 1) * CHUNK, :].T
    z1 = proj(GRP)
    sg_ref[...] = _silu(z5)
    z_ref[:, 0:GRP] = proj(0)
    gl_ref[...] = _gelu_tanh(z1)
    for g in range(3):
        z_ref[:, (g + 1) * GRP:(g + 2) * GRP] = proj((g + 2) * GRP)

    sp_lam = _softplus(-lam_ref[...])
    ri = lax.broadcasted_iota(jnp.int32, (CHUNK, CHUNK), 0)
    ci = lax.broadcasted_iota(jnp.int32, (CHUNK, CHUNK), 1)
    rows_w = lax.broadcasted_iota(jnp.int32, (CHUNK, LRU_WIDTH), 0)
    eye_c = (ri == ci).astype(F32)
    sub_bits = DN_SUB.bit_length() - 1
    same_blk = (ri >> sub_bits) == (ci >> sub_bits)
    merge_masks = []
    for lvl in range(sub_bits, CHUNK.bit_length() - 1):
        merge_masks.append(((ri >> (lvl + 1)) == (ci >> (lvl + 1)))
                           & (((ri >> lvl) & 1) == 1) & (((ci >> lvl) & 1) == 0))

    def chunk_body(i, carry):
        lane_seq, lane_rows, lane_chunk, ys = [], [], [], []
        for u in range(lanes):
            s, c = (i * lanes + u, 0) if nc == 1 else (u, i)
            r0 = (s * nc + c) * CHUNK
            rows = pl.ds(r0, CHUNK)
            lane_seq.append(s)
            lane_rows.append(rows)
            lane_chunk.append(s * nc + c)

            xp_ref[u, top - hist:top, 0:LRU_WIDTH] = lc_o[s]
            xp_ref[u, top - hist:top, LRU_WIDTH:conv_ch] = dc_o[s]
            xp_ref[u, top:top + CHUNK, :] = z_ref[rows, :]
            lc_o[s] = xp_ref[u, top + CHUNK - hist:top + CHUNK, 0:LRU_WIDTH]
            dc_o[s] = xp_ref[u, top + CHUNK - hist:top + CHUNK, LRU_WIDTH:conv_ch]
            y = xp_ref[u, top - hist:top - hist + CHUNK, :] * cw_ref[0:1, :]
            for jj in range(1, CONV_W):
                y = y + (xp_ref[u, top - hist + jj:top - hist + jj + CHUNK, :]
                         * cw_ref[jj:jj + 1, :])
            ys.append(y)

        def rg_lru_branch(lane):
            y, s, rows = ys[lane], lane_seq[lane], lane_rows[lane]
            lx = y[:, 0:LRU_WIDTH] + lcb_ref[...]
            xb = lx.astype(BF16)
            r = jax.nn.sigmoid(_dot(xb, wa_ref[...]) + ba_ref[...])
            ig = jax.nn.sigmoid(_dot(xb, wx_ref[...]) + bx_ref[...])
            a = jnp.exp(-LRU_C * r * sp_lam)
            u = jnp.sqrt(1.0 - a * a) * (ig * lx)
            k = 1
            while k < CHUNK:
                m = rows_w >= k
                u = jnp.where(m, a * pltpu.roll(u, k, 0) + u, u)
                a = jnp.where(m, a * pltpu.roll(a, k, 0), a)
                k *= 2
            hseq = a * lh_o[s] + u
            lh_o[s] = hseq[CHUNK - 1:CHUNK]
            mix_ref[rows, 0:LRU_WIDTH] = (gl_ref[rows, :] * hseq).astype(BF16)

        streams = [(u, h) for u in range(lanes) for h in heads]
        ns_ = range(len(streams))
        beta_t = [bt_ref[rows, :] for rows in lane_rows]
        g_t = [gt_ref[rows, :] for rows in lane_rows]
        g_tt = [gtt_ref[c] for c in lane_chunk]
        states = [ds_o[lane_seq[u], h] for u, h in streams]
        kn, both_k, both_s = [], [], []
        for n, (u, h) in enumerate(streams):
            y = ys[u]
            base = LRU_WIDTH + h * HD
            q = _silu(y[:, base:base + HD])
            kk_ = _silu(y[:, base + GRP:base + GRP + HD])
            q = q * lax.rsqrt(jnp.sum(q * q, axis=-1, keepdims=True) + EPS) * (HD ** -0.5)
            kn.append(kk_ * lax.rsqrt(jnp.sum(kk_ * kk_, axis=-1, keepdims=True) + EPS))
            knb = kn[n].astype(BF16)
            qk_in = jnp.concatenate([q.astype(BF16), knb], axis=0)
            both_k.append(_dot_nt(qk_in, knb))
            both_s.append(_dot(qk_in, states[n].astype(BF16)))
        gcol, eg, e_incl, amat, rhs = [], [], [], [], []
        for n, (u, h) in enumerate(streams):
            y = ys[u]
            v = _silu(y[:, LRU_WIDTH + 2 * GRP + h * HD:LRU_WIDTH + 2 * GRP + (h + 1) * HD])
            gcol.append(g_t[u][:, HEADS + h:HEADS + h + 1])
            rel = gcol[n] - g_tt[u][HEADS + h:HEADS + h + 1, :]
            e_strict = jnp.exp(jnp.where(ri > ci, rel, -jnp.inf))
            e_incl.append(jnp.where(ri == ci, 1.0, e_strict))
            beta = beta_t[u][:, h:h + 1]
            eg.append(jnp.exp(gcol[n]))
            amat.append(beta * both_k[n][CHUNK:] * e_strict)
            rhs.append(beta * (v - eg[n] * both_s[n][CHUNK:]))
        dblk = [jnp.where(same_blk, amat[n], 0.0) for n in ns_]
        tinv = [eye_c for _ in ns_]
        for jj in range(DN_SUB - 1):
            for n in ns_:
                colv = jnp.concatenate(
                    [dblk[n][b0:b0 + DN_SUB, b0 + jj:b0 + jj + 1]
                     for b0 in range(0, CHUNK, DN_SUB)], axis=0)
                rowm = jnp.concatenate(
                    [jnp.broadcast_to(tinv[n][b0 + jj:b0 + jj + 1, :], (DN_SUB, CHUNK))
                     for b0 in range(0, CHUNK, DN_SUB)], axis=0)
                tinv[n] = tinv[n] - colv * rowm
        for lane in range(lanes):
            rg_lru_branch(lane)
        tb = [tinv[n].astype(BF16) for n in ns_]
        pend = [[_dot(tb[n], jnp.where(lm, amat[n], 0.0).astype(BF16)) for lm in merge_masks]
                for n in ns_]
        for _ in merge_masks:
            xb_ = [pend[n][0].astype(BF16) for n in ns_]
            tb = [tinv[n].astype(BF16) for n in ns_]
            tinv = [tinv[n] - _dot(xb_[n], tb[n]) for n in ns_]
            pend = [[p - _dot(xb_[n], p.astype(BF16)) for p in pend[n][1:]] for n in ns_]
        w = [_dot(tinv[n].astype(BF16), rhs[n].astype(BF16)).astype(BF16) for n in ns_]
        for n, (u, h) in enumerate(streams):
            s, rows = lane_seq[u], lane_rows[u]
            o = eg[n] * both_s[n][:CHUNK] + _dot((both_k[n][:CHUNK] * e_incl[n]).astype(BF16), w[n])
            gend = gcol[n][CHUNK - 1:CHUNK, :]
            kd = (kn[n] * jnp.exp(gend - gcol[n])).astype(BF16)
            ds_o[s, h] = states[n] * jnp.exp(gend) + _dot_tn(kd, w[n])
            o = _head_norm(o, dg_ref[:, pl.ds(h * HD, HD)]) * sg_ref[rows, pl.ds(h * HD, HD)]
            mix_ref[rows, pl.ds(LRU_WIDTH + h * HD, HD)] = o.astype(BF16)
        return carry

    lanes = ODD_LANES if ns % ODD_LANES == 0 else 1
    for i in range(ns * nc // lanes):
        chunk_body(i, 0)

    xo_ref[...] = (x + _dot(mix_ref[...], wout_ref[...])).reshape(xo_ref.shape)


def _odd_mixer(x, norm_g, w_in, w_tail, w_out, cw, lcb, wa, ba, wx, bx, lam, dnp, dn_g,
               s_lh, s_lc, s_dn, s_dc):
    b, t, _ = x.shape
    ns, nc = _seq_tiling(b, t, ODD_LANES, ODD_SEQ_TILE)
    tt = ns * nc * CHUNK
    grid = (b // ns, t // (nc * CHUNK))
    x_spec = pl.BlockSpec((ns, nc * CHUNK, D_MODEL), lambda i, j: (i, j, 0))

    def full(a):
        return _resident(a.shape, lambda i, j: (0,) * a.ndim)

    def st_spec(a):
        return pl.BlockSpec((ns,) + a.shape[1:], lambda i, j: (i,) + (0,) * (a.ndim - 1))

    params = (norm_g.reshape(1, D_MODEL), w_in, w_tail, w_out, cw, lcb, wa, ba, wx, bx, lam, dnp,
              dn_g)
    states = (s_lh, s_lc, s_dn, s_dc)
    grp_f32 = pltpu.VMEM((tt, GRP), F32)
    head_f32 = pltpu.VMEM((tt, HD), F32)
    return pl.pallas_call(
        functools.partial(_odd_kernel, ns=ns, nc=nc),
        grid=grid,
        in_specs=[x_spec] + [full(a) for a in params] + [st_spec(a) for a in states],
        out_specs=[x_spec] + [st_spec(a) for a in states],
        out_shape=[jax.ShapeDtypeStruct(x.shape, F32)]
        + [jax.ShapeDtypeStruct(a.shape, F32) for a in states],
        scratch_shapes=[pltpu.VMEM((tt, LRU_WIDTH + 3 * GRP), F32), grp_f32, grp_f32,
                        head_f32, head_f32, pltpu.VMEM((ns * nc, HD, CHUNK), F32),
                        pltpu.VMEM((tt, D_MODEL), BF16),
                        pltpu.VMEM((ODD_LANES, 8 + CHUNK, LRU_WIDTH + 3 * GRP), F32)],
        compiler_params=pltpu.CompilerParams(
            dimension_semantics=("parallel", "arbitrary"), vmem_limit_bytes=VMEM_LIMIT),
        name="odd_mixer",
    )(x, *params, *states)


def _rope_tables(pos0, n):
    half = HD // 2
    freq = ROPE_BASE ** (-jnp.arange(half, dtype=F32) / half)
    ang_a = (pos0 + CHUNK * jnp.arange(n // CHUNK)).astype(F32)[:, None, None] * freq
    ang_b = jnp.arange(CHUNK, dtype=F32)[None, :, None] * freq
    cos_a, sin_a, cos_b, sin_b = jnp.cos(ang_a), jnp.sin(ang_a), jnp.cos(ang_b), jnp.sin(ang_b)
    cos = (cos_a * cos_b - sin_a * sin_b).reshape(n, half)
    sin = (sin_a * cos_b + cos_a * sin_b).reshape(n, half)
    return jnp.concatenate([cos, cos], axis=-1), jnp.concatenate([-sin, sin], axis=-1)


def _block_diag(w):
    n, bs, _ = w.shape
    eye = jnp.eye(n, dtype=w.dtype)
    return (eye[:, None, :, None] * w[:, :, None, :]).reshape(n * bs, n * bs)


def kernel(x_prompt, x_sample, state_ret, state_hgrn, state_lru_h, state_lru_conv, state_dn,
           state_dn_conv, ffn1_norm, ffn1_w_in, ffn1_w_out, mix_norm, ffn2_norm, ffn2_w_in,
           ffn2_w_out, final_norm, even_w_in, even_w_out, ret_out_norm, hg_out_norm,
           hg_lb_logits, odd_w_in, odd_w_out, lru_conv_w, lru_conv_b, lru_w_a, lru_b_a, lru_w_x,
           lru_b_x, lru_lambda, dn_conv_w, dn_a_log, dn_dt_bias, dn_out_norm):
    depth = ffn1_norm.shape[0]
    bp, tp, _ = x_prompt.shape
    bs, ts, _ = x_sample.shape
    tabs = (_rope_tables(0, tp), _rope_tables(PAST_LEN, ts))
    xs = [x_prompt, x_sample]
    nb = (bp, bs)
    outs = {k: ([], []) for k in ("ret", "hg", "lh", "lc", "dn", "dc")}

    ffn_sets = []
    for l in range(depth):
        ffn_sets += [(ffn1_norm[l], ffn1_w_in, ffn1_w_out, l),
                     (ffn2_norm[l], ffn2_w_in, ffn2_w_out, l)]
    ffn_w = [(ffn1_w_in[0].astype(BF16), ffn1_w_out[0].astype(BF16))]

    def run_ffn(apply_final, mixer_casts=()):
        k = len(ffn_w) - 1
        casts = []
        if k + 1 < len(ffn_sets):
            _, nwi, nwo, nl = ffn_sets[k + 1]
            casts = [(nwi, nl, FFN_CAST_ROWS, nwi.shape[2], nwi.shape[1]),
                     (nwo, nl, FFN_CAST_ROWS, nwo.shape[2], nwo.shape[1])]
        casts += list(mixer_casts)
        xs[0], xs[1], cast = _ffn(xs[0], xs[1], ffn_sets[k][0], ffn_w[k][0], ffn_w[k][1],
                                  final_norm, apply_final, casts)
        ffn_w.append(tuple(cast[:2]) if k + 1 < len(ffn_sets) else ())
        return cast[len(cast) - len(mixer_casts):]

    for l in range(depth):
        j = l // 2
        last = l == depth - 1
        if l % 2 == 0:
            w_in, w_out = run_ffn(False, [(even_w_in, j, None, EVEN_IN, D_MODEL),
                                          (even_w_out, j, None, D_MODEL, 2 * GRP)])
            for g in range(2):
                if g == 0:
                    s_ret = jnp.zeros((bp, HEADS, HD, HD), F32)
                    s_hg = jnp.zeros((bp, HEADS, HD, HD), F32)
                else:
                    s_ret, s_hg = state_ret[j], state_hgrn[j]
                xs[g], n_ret, n_hg = _even_mixer(
                    xs[g], tabs[g][0], tabs[g][1], mix_norm[l], w_in, w_out, ret_out_norm[j],
                    hg_out_norm[j], hg_lb_logits, s_ret, s_hg, j)
                outs["ret"][g].append(n_ret)
                outs["hg"][g].append(n_hg)
        else:
            w_in_t = jnp.swapaxes(odd_w_in, 1, 2)
            w_in, w_out = run_ffn(False, [(w_in_t, j, None, D_MODEL, 6 * GRP),
                                          (odd_w_out, j, None, D_MODEL, 2 * GRP)])
            n_tail = odd_w_in.shape[-1] - 6 * GRP
            tail = lax.optimization_barrier(w_in_t[j][6 * GRP:, :])
            w_tail = jnp.pad(tail, ((0, HD - n_tail), (0, 0))).astype(BF16)
            cw = jnp.concatenate([lru_conv_w[j], dn_conv_w[j]], axis=-1)
            wa = _block_diag(lru_w_a[j]).astype(BF16)
            wx = _block_diag(lru_w_x[j]).astype(BF16)
            dnp = jnp.zeros((2, HD), F32)
            dnp = dnp.at[0, HEADS:2 * HEADS].set(dn_dt_bias[j])
            dnp = dnp.at[1, HEADS:2 * HEADS].set(dn_a_log[j])
            for g in range(2):
                if g == 0:
                    s_lh = jnp.zeros((bp, 1, LRU_WIDTH), F32)
                    s_lc = jnp.zeros((bp, CONV_W - 1, LRU_WIDTH), F32)
                    s_dn = jnp.zeros((bp, HEADS, HD, HD), F32)
                    s_dc = jnp.zeros((bp, CONV_W - 1, 3 * GRP), F32)
                else:
                    s_lh = state_lru_h[j].reshape(bs, 1, LRU_WIDTH)
                    s_lc, s_dn, s_dc = state_lru_conv[j], state_dn[j], state_dn_conv[j]
                xs[g], n_lh, n_lc, n_dn, n_dc = _odd_mixer(
                    xs[g], mix_norm[l], w_in, w_tail, w_out, cw, lru_conv_b[j].reshape(1, -1), wa,
                    lru_b_a[j].reshape(1, -1), wx, lru_b_x[j].reshape(1, -1),
                    lru_lambda[j].reshape(1, -1), dnp, dn_out_norm[j].reshape(1, -1),
                    s_lh, s_lc, s_dn, s_dc)
                outs["lh"][g].append(n_lh.reshape(nb[g], LRU_WIDTH))
                outs["lc"][g].append(n_lc)
                outs["dn"][g].append(n_dn)
                outs["dc"][g].append(n_dc)
        run_ffn(last)

    res = [xs[0], xs[1]]
    for k in ("ret", "hg", "lh", "lc", "dn", "dc"):
        for g in range(2):
            res.append(jnp.stack(outs[k][g]))
    return tuple(res)
```
